```python
import jax, jax.numpy as jnp
from jax import lax
import numpy as np

D_MODEL = 1024
BATCH = 2
SEQ = 16384
DEPTH = 2

GRID_W = 64
CTX_LEN = 256
EPS = 1e-6
MASK_VALUE = -1e30
TINY = 1e-30

HG_HEADS = 4
HG_KDIM = 128
HG_VDIM = 128
HG_KW = HG_HEADS * HG_KDIM
HG_VW = HG_HEADS * HG_VDIM
HG_CHUNK = 64

HEAD_DIM = 64
ATT_HEADS = 8
ATT_KV = 2
SWA_HEADS = 8
SWA_KV = 2
WINDOW = 128
Q_BLOCK = 128
ROPE_THETA = 10000.0
ATTN_SCALE = HEAD_DIM ** -0.5
N_BRANCH = 3

N_GROUPS = 4
EXP_PER_GROUP = 8
N_EXPERTS = N_GROUPS * EXP_PER_GROUP
TOP_K = 2
D_EXPERT = 512
MOE_BLOCK = 128

IN_SIZES = (HG_KW, HG_KW, HG_KW, HG_VW, HG_VW,
            ATT_HEADS * HEAD_DIM, ATT_KV * HEAD_DIM, ATT_KV * HEAD_DIM,
            SWA_HEADS * HEAD_DIM, SWA_KV * HEAD_DIM, SWA_KV * HEAD_DIM,
            N_BRANCH * D_MODEL)
D_IN = sum(IN_SIZES)

kernel_name = "hybrid_flow_backbone_block"


def rms_norm(x, g):
    xf = x.astype(jnp.float32)
    y = xf * lax.rsqrt(jnp.mean(xf * xf, axis=-1, keepdims=True) + EPS)
    return y.astype(x.dtype) * g


def split_projection(p):
    idx, acc = [], 0
    for s in IN_SIZES[:-1]:
        acc += s
        idx.append(acc)
    return jnp.split(p, idx, axis=-1)


def split_heads(a, n_heads):
    return a.reshape(a.shape[0], a.shape[1], n_heads, HEAD_DIM)


def axial_rope(n_tokens, dtype):
    n_rows = n_tokens // GRID_W
    row = jnp.repeat(jnp.arange(n_rows), GRID_W).astype(jnp.float32)
    col = jnp.tile(jnp.arange(GRID_W), n_rows).astype(jnp.float32)
    axis_pairs = HEAD_DIM // 4
    inv = ROPE_THETA ** (-jnp.arange(axis_pairs, dtype=jnp.float32) / axis_pairs)
    ang = jnp.concatenate([row[:, None] * inv, col[:, None] * inv], axis=-1)
    return jnp.cos(ang).astype(dtype)[:, None, :], jnp.sin(ang).astype(dtype)[:, None, :]


def apply_rope(x, cos, sin):
    x1, x2 = jnp.split(x, 2, axis=-1)
    return jnp.concatenate([x1 * cos - x2 * sin, x2 * cos + x1 * sin], axis=-1)


def gla_chunk_scan(q, k, v, log_f, s0):
    b, l, h, _ = q.shape
    n = l // HG_CHUNK

    def chunks(a):
        return a.astype(jnp.float32).reshape(b, n, HG_CHUNK, h, a.shape[-1]).transpose(1, 0, 3, 2, 4)

    lower = jnp.tril(jnp.ones((HG_CHUNK, HG_CHUNK), bool))[None, None, :, :, None]

    def step(state, inp):
        qc, kc, vc, gc = inp
        g_cum = jnp.cumsum(gc, axis=2)
        diff = jnp.where(lower, g_cum[:, :, :, None, :] - g_cum[:, :, None, :, :], 0.0)
        rel = jnp.where(lower, jnp.exp(diff), 0.0)
        scores = jnp.einsum('bhtk,bhsk,bhtsk->bhts', qc, kc, rel)
        out = (jnp.einsum('bhts,bhsv->bhtv', scores, vc)
               + jnp.einsum('bhtk,bhkv->bhtv', qc * jnp.exp(g_cum), state))
        g_end = g_cum[:, :, -1:, :]
        state = (jnp.exp(g_end[:, :, 0, :, None]) * state
                 + jnp.einsum('bhsk,bhsv->bhkv', kc * jnp.exp(g_end - g_cum), vc))
        return state, out

    state, out = lax.scan(step, s0, (chunks(q), chunks(k), chunks(v), chunks(log_f)))
    out = out.transpose(1, 0, 3, 2, 4).reshape(b, l, h, v.shape[-1])
    return out.astype(v.dtype), state


def hgrn2_branch(parts_l, parts_c, lower_bound, out_norm_g, ctx_out):
    lb = lower_bound.reshape(HG_HEADS, HG_KDIM)

    def prep(q, f_fwd, f_bwd, i):
        bsz, n = q.shape[:2]
        heads = lambda a: a.reshape(bsz, n, HG_HEADS, -1)
        qh = heads(jax.nn.silu(q)) * HG_KDIM ** -0.5

        def decay(f):
            sig = jax.nn.sigmoid(heads(f).astype(jnp.float32))
            forget = lb + (1.0 - lb) * sig
            log_forget = jnp.log(jnp.maximum(forget, TINY))
            key = (1.0 - lb) * (1.0 - sig)
            return key, log_forget
        return qh, heads(i), decay(f_fwd), decay(f_bwd)

    flip = lambda a: jnp.flip(a, axis=1)
    s0 = jnp.zeros((parts_l[0].shape[0], HG_HEADS, HG_KDIM, HG_VDIM), jnp.float32)
    q_c, v_c, (kf_c, gf_c), (kb_c, gb_c) = prep(*parts_c[:4])
    q_l, v_l, (kf_l, gf_l), (kb_l, gb_l) = prep(*parts_l[:4])
    of_c, state_f = gla_chunk_scan(q_c, kf_c, v_c, gf_c, s0)
    ob_c, state_b = gla_chunk_scan(flip(q_c), flip(kb_c), flip(v_c), flip(gb_c), s0)
    of_l, _ = gla_chunk_scan(q_l, kf_l, v_l, gf_l, state_f)
    ob_l, _ = gla_chunk_scan(flip(q_l), flip(kb_l), flip(v_l), flip(gb_l), state_b)

    def readout(o, gate):
        return rms_norm(o, out_norm_g).reshape(gate.shape) * jax.nn.silu(gate)

    out_l = readout(of_l + flip(ob_l), parts_l[4])
    out_c = readout(of_c + flip(ob_c), parts_c[4]) if ctx_out else None
    return out_l, out_c


def dense_attention_branch(q_l, k_l, v_l, q_c, k_c, v_c, q_norm_g, k_norm_g, cos, sin, ctx_out):
    bsz, s = q_l.shape[:2]
    group = ATT_HEADS // ATT_KV
    ql = apply_rope(rms_norm(split_heads(q_l, ATT_HEADS), q_norm_g), cos, sin)
    kl = apply_rope(rms_norm(split_heads(k_l, ATT_KV), k_norm_g), cos, sin)
    kc = rms_norm(split_heads(k_c, ATT_KV), k_norm_g)
    vl, vc = split_heads(v_l, ATT_KV), split_heads(v_c, ATT_KV)
    k_all = jnp.concatenate([kc, kl], axis=1)
    v_all = jnp.concatenate([vc, vl], axis=1)

    def attend(q, keys, vals):
        qg = q.reshape(q.shape[0], q.shape[1], ATT_KV, group, HEAD_DIM)
        logits = jnp.einsum('bqhgd,bkhd->bhgqk', qg, keys).astype(jnp.float32) * ATTN_SCALE
        p = jax.nn.softmax(logits, axis=-1).astype(vals.dtype)
        o = jnp.einsum('bhgqk,bkhd->bqhgd', p, vals)
        return o.reshape(q.shape[0], q.shape[1], ATT_HEADS * HEAD_DIM)

    def block(n):
        return attend(lax.dynamic_slice_in_dim(ql, n * Q_BLOCK, Q_BLOCK, axis=1), k_all, v_all)

    out_l = lax.map(block, jnp.arange(s // Q_BLOCK)).transpose(1, 0, 2, 3).reshape(bsz, s, -1)
    out_c = attend(rms_norm(split_heads(q_c, ATT_HEADS), q_norm_g), kc, vc) if ctx_out else None
    return out_l, out_c


def window_attention_branch(q_l, k_l, v_l, q_c, k_c, v_c, sink, cos, sin, ctx_out):
    bsz, s = q_l.shape[:2]
    group = SWA_HEADS // SWA_KV
    band = Q_BLOCK + 2 * WINDOW
    ql = apply_rope(split_heads(q_l, SWA_HEADS), cos, sin)
    kl = apply_rope(split_heads(k_l, SWA_KV), cos, sin)
    vl = split_heads(v_l, SWA_KV)
    kc, vc = split_heads(k_c, SWA_KV), split_heads(v_c, SWA_KV)
    pad = ((0, 0), (WINDOW, WINDOW), (0, 0), (0, 0))
    kp, vp = jnp.pad(kl, pad), jnp.pad(vl, pad)
    sink_logit = sink.astype(jnp.float32).reshape(1, SWA_KV, group, 1, 1)
    offs_q = jnp.arange(Q_BLOCK)
    offs_k = jnp.arange(band) - WINDOW

    def attend(q, keys, vals, mask):
        qg = q.reshape(q.shape[0], q.shape[1], SWA_KV, group, HEAD_DIM)
        logits = [jnp.einsum('bqhgd,bkhd->bhgqk', qg, kk).astype(jnp.float32) * ATTN_SCALE for kk in keys]
        if mask is not None:
            logits[-1] = jnp.where(mask, logits[-1], MASK_VALUE)
        sink_col = jnp.broadcast_to(sink_logit, logits[0].shape[:-1] + (1,))
        p = jax.nn.softmax(jnp.concatenate(logits + [sink_col], axis=-1), axis=-1)
        outs, lo = [], 0
        for vv in vals:
            hi = lo + vv.shape[1]
            outs.append(jnp.einsum('bhgqk,bkhd->bqhgd', p[..., lo:hi].astype(vv.dtype), vv))
            lo = hi
        return sum(outs).reshape(q.shape[0], q.shape[1], SWA_HEADS * HEAD_DIM)

    def block(n):
        start = n * Q_BLOCK
        qb = lax.dynamic_slice_in_dim(ql, start, Q_BLOCK, axis=1)
        kb = lax.dynamic_slice_in_dim(kp, start, band, axis=1)
        vb = lax.dynamic_slice_in_dim(vp, start, band, axis=1)
        key_pos = start + offs_k
        mask = ((jnp.abs(offs_k[None, :] - offs_q[:, None]) <= WINDOW)
                & (key_pos >= 0)[None, :] & (key_pos < s)[None, :])
        return attend(qb, [kc, kb], [vc, vb], mask)

    out_l = lax.map(block, jnp.arange(s // Q_BLOCK)).transpose(1, 0, 2, 3).reshape(bsz, s, -1)
    out_c = attend(split_heads(q_c, SWA_HEADS), [kc], [vc], None) if ctx_out else None
    return out_l, out_c


def hier_moe(h, w_group, w_router, w_gate, w_up, w_down):
    n_tok, d = h.shape
    n_assign = n_tok * TOP_K
    grp_prob = jax.nn.softmax((h @ w_group).astype(jnp.float32), axis=-1)
    grp_w, grp_idx = lax.top_k(grp_prob, 1)
    exp_logits = (h @ w_router).astype(jnp.float32).reshape(n_tok, N_GROUPS, EXP_PER_GROUP)
    in_grp = jnp.take_along_axis(exp_logits, grp_idx[:, :, None], axis=1)[:, 0]
    top_logit, top_j = lax.top_k(in_grp, TOP_K)
    weight = (jax.nn.softmax(top_logit, axis=-1) * grp_w).reshape(-1)
    expert = (grp_idx * EXP_PER_GROUP + top_j).reshape(-1)
    token = jnp.repeat(jnp.arange(n_tok), TOP_K)
    order = jnp.argsort(expert, stable=True)
    e_s, t_s, w_s = expert[order], token[order], weight[order]
    counts = jnp.bincount(expert, length=N_EXPERTS)
    padded = (counts + MOE_BLOCK - 1) // MOE_BLOCK * MOE_BLOCK
    pad_end = jnp.cumsum(padded)
    pad_start = pad_end - padded
    start = jnp.cumsum(counts) - counts
    dest = pad_start[e_s] + jnp.arange(n_assign) - start[e_s]
    n_blocks = -(-(n_assign + N_EXPERTS * (MOE_BLOCK - 1)) // MOE_BLOCK)
    rows = jnp.zeros((n_blocks * MOE_BLOCK, d), h.dtype).at[dest].set(h[t_s])
    block_expert = jnp.minimum(jnp.searchsorted(pad_end, jnp.arange(n_blocks) * MOE_BLOCK, side='right'),
                               N_EXPERTS - 1)

    def expert_block(args):
        xb, e = args
        return (jax.nn.silu(xb @ w_gate[e]) * (xb @ w_up[e])) @ w_down[e]

    out_rows = lax.map(expert_block, (rows.reshape(n_blocks, MOE_BLOCK, d), block_expert)).reshape(-1, d)
    return jnp.zeros_like(h).at[t_s].add(out_rows[dest] * w_s[:, None].astype(h.dtype))


def trunk_layer(xl, xc, mod_l, mod_c, cos, sin, norm1_g, norm2_g, w_in, lower_bound, hgrn_g,
                q_norm_g, k_norm_g, sink, w_br_a, w_br_b, w_br_c, w_out,
                w_group, w_router, w_gate, w_up, w_down, ctx_out):
    d = xl.shape[-1]
    sh1, sc1, ga1, sh2, sc2, ga2 = jnp.split(mod_l[:, None, :], 6, axis=-1)
    csh1, csc1, cga1, csh2, csc2, cga2 = jnp.split(mod_c, 6, axis=-1)

    pl = split_projection((rms_norm(xl, norm1_g) * (1 + sc1) + sh1) @ w_in)
    pc = split_projection((rms_norm(xc, norm1_g) * (1 + csc1) + csh1) @ w_in)
    a_l, a_c = hgrn2_branch(pl[0:5], pc[0:5], lower_bound, hgrn_g, ctx_out)
    b_l, b_c = dense_attention_branch(*pl[5:8], *pc[5:8], q_norm_g, k_norm_g, cos, sin, ctx_out)
    c_l, c_c = window_attention_branch(*pl[8:11], *pc[8:11], sink, cos, sin, ctx_out)

    def merge(a, bb, cc, gate_logits):
        g_a, g_b, g_c = jnp.split(jax.nn.sigmoid(gate_logits), N_BRANCH, axis=-1)
        return (g_a * (a @ w_br_a) + g_b * (bb @ w_br_b) + g_c * (cc @ w_br_c)) @ w_out

    xl = xl + ga1 * merge(a_l, b_l, c_l, pl[11])
    if ctx_out:
        xc = xc + cga1 * merge(a_c, b_c, c_c, pc[11])

    hl = (rms_norm(xl, norm2_g) * (1 + sc2) + sh2).reshape(-1, d)
    if ctx_out:
        hc = (rms_norm(xc, norm2_g) * (1 + csc2) + csh2).reshape(-1, d)
        y = hier_moe(jnp.concatenate([hc, hl], axis=0), w_group, w_router, w_gate, w_up, w_down)
        n_c = hc.shape[0]
        xc = xc + cga2 * y[:n_c].reshape(xc.shape)
        y_l = y[n_c:]
    else:
        y_l = hier_moe(hl, w_group, w_router, w_gate, w_up, w_down)
    xl = xl + ga2 * y_l.reshape(xl.shape)
    return xl, xc


def setup_inputs(seed: int = 0) -> dict:
    key = jax.random.key(seed)
    ks = jax.random.split(key, 24)
    d = D_MODEL

    def nrm(k, shape, scale):
        return jax.random.normal(k, shape, jnp.float32) * scale

    return {
        "x": nrm(ks[0], (BATCH, SEQ, d), 1.0),
        "c": nrm(ks[1], (BATCH, d), 1.0),
        "ctx": nrm(ks[2], (BATCH, CTX_LEN, d), 1.0),
        "c_ctx": nrm(ks[3], (d,), 1.0),
        "w_mod": nrm(ks[4], (DEPTH, d, 6 * d), 0.5 * d ** -0.5),
        "b_mod": nrm(ks[5], (DEPTH, 6 * d), 0.02),
        "norm1_g": 1.0 + nrm(ks[6], (DEPTH, d), 0.02),
        "norm2_g": 1.0 + nrm(ks[7], (DEPTH, d), 0.02),
        "w_in": nrm(ks[8], (DEPTH, d, D_IN), d ** -0.5),
        "hgrn_lb_logits": nrm(ks[9], (DEPTH, HG_KW), 0.5),
        "hgrn_out_norm_g": 1.0 + nrm(ks[10], (DEPTH, HG_VDIM), 0.02),
        "attn_q_norm_g": 1.0 + nrm(ks[11], (DEPTH, HEAD_DIM), 0.02),
        "attn_k_norm_g": 1.0 + nrm(ks[12], (DEPTH, HEAD_DIM), 0.02),
        "swa_sink": nrm(ks[13], (DEPTH, SWA_HEADS), 1.0),
        "w_branch_a": nrm(ks[14], (DEPTH, HG_VW, d), HG_VW ** -0.5),
        "w_branch_b": nrm(ks[15], (DEPTH, ATT_HEADS * HEAD_DIM, d), (ATT_HEADS * HEAD_DIM) ** -0.5),
        "w_branch_c": nrm(ks[16], (DEPTH, SWA_HEADS * HEAD_DIM, d), (SWA_HEADS * HEAD_DIM) ** -0.5),
        "w_out": nrm(ks[17], (DEPTH, d, d), d ** -0.5),
        "w_group": nrm(ks[18], (DEPTH, d, N_GROUPS), d ** -0.5),
        "w_router": nrm(ks[19], (DEPTH, d, N_EXPERTS), d ** -0.5),
        "w_exp_gate": nrm(ks[20], (DEPTH, N_EXPERTS, d, D_EXPERT), d ** -0.5),
        "w_exp_up": nrm(ks[21], (DEPTH, N_EXPERTS, d, D_EXPERT), d ** -0.5),
        "w_exp_down": nrm(ks[22], (DEPTH, N_EXPERTS, D_EXPERT, d), D_EXPERT ** -0.5),
        "final_norm_g": 1.0 + nrm(ks[23], (d,), 0.02),
    }


def reference(x, c, ctx, c_ctx, w_mod, b_mod, norm1_g, norm2_g, w_in, hgrn_lb_logits, hgrn_out_norm_g,
              attn_q_norm_g, attn_k_norm_g, swa_sink, w_branch_a, w_branch_b, w_branch_c, w_out,
              w_group, w_router, w_exp_gate, w_exp_up, w_exp_down, final_norm_g):
    cos, sin = axial_rope(x.shape[1], x.dtype)
    lb_p = jax.nn.softmax(hgrn_lb_logits.astype(jnp.float32), axis=0)
    lower_bounds = jnp.cumsum(lb_p, axis=0) - lb_p[0]
    xl, xc = x, ctx
    for layer in range(DEPTH):
        mod_l = jax.nn.silu(c) @ w_mod[layer] + b_mod[layer]
        mod_c = jax.nn.silu(c_ctx) @ w_mod[layer] + b_mod[layer]
        xl, xc = trunk_layer(xl, xc, mod_l, mod_c, cos, sin, norm1_g[layer], norm2_g[layer], w_in[layer],
                             lower_bounds[layer], hgrn_out_norm_g[layer], attn_q_norm_g[layer],
                             attn_k_norm_g[layer], swa_sink[layer], w_branch_a[layer], w_branch_b[layer],
                             w_branch_c[layer], w_out[layer], w_group[layer], w_router[layer],
                             w_exp_gate[layer], w_exp_up[layer], w_exp_down[layer],
                             ctx_out=layer < DEPTH - 1)
    return rms_norm(xl, final_norm_g)
```

```python
import functools
import math

import numpy as np
import jax
import jax.numpy as jnp
from jax import lax
from jax.experimental import pallas as pl
from jax.experimental.pallas import tpu as pltpu

F32 = jnp.float32
BF16 = jnp.bfloat16

EPS = 1e-6
MASK_VALUE = -1e30
TINY = 1e-30
GRID_W = 64
ROPE_THETA = 10000.0

HG_HEADS = 4
HG_KDIM = 128
HG_W = HG_HEADS * HG_KDIM
HEAD_DIM = 64
N_HEADS = 8
N_KV = 2
GROUP = N_HEADS // N_KV
ATT_W = N_HEADS * HEAD_DIM
KV_W = N_KV * HEAD_DIM
WINDOW = 128
ATTN_SCALE = HEAD_DIM ** -0.5
N_GROUPS = 4
EXP_PER_GROUP = 8
N_EXPERTS = N_GROUPS * EXP_PER_GROUP
TOP_K = 2

LANES = 128
HG_CHUNK = 128
HG_LEVELS = int(math.log2(HG_CHUNK))
MOE_ROWS = 256
VMEM_LIMIT = 56 * 1024 * 1024


def _cparams(sem):
    return pltpu.CompilerParams(dimension_semantics=sem, vmem_limit_bytes=VMEM_LIMIT)


_REF_SIZES = (HG_W, HG_W, HG_W, HG_W, HG_W, ATT_W, KV_W, KV_W, ATT_W, KV_W, KV_W, 3 * 1024)
_REF_NAMES = ("hq", "ff", "fb", "hi", "hg", "aq", "ak", "av", "sq", "sk", "sv", "gates")
_NEW_ORDER = ("hq", "ff", "fb", "hi", "hg", "aq", "sq", "ak", "av", "sk", "sv", "gates")


def _column_permutation():
    starts, acc = {}, 0
    for n, s in zip(_REF_NAMES, _REF_SIZES):
        starts[n] = (acc, s)
        acc += s
    perm, new_starts, pos = [], {}, 0
    for n in _NEW_ORDER:
        a, s = starts[n]
        perm.append(np.arange(a, a + s))
        new_starts[n] = pos
        pos += s
    return np.concatenate(perm), new_starts


_COL_PERM, COL = _column_permutation()
D_IN = int(_COL_PERM.shape[0])


def _inproj_kernel(x_ref, g_ref, sc_ref, sh_ref, w_ref, o_ref, h_scr):
    @pl.when(pl.program_id(1) == 0)
    def _():
        x = x_ref[...]
        ms = jnp.mean(x * x, axis=-1, keepdims=True)
        y = x * lax.rsqrt(ms + EPS) * g_ref[...]
        h_scr[...] = (y * (1.0 + sc_ref[0]) + sh_ref[0]).astype(BF16)

    o_ref[...] = jnp.dot(h_scr[...], w_ref[...], preferred_element_type=F32)


def inproj(x2d, rows_per_batch, g, scale, shift, w_bf16, tm, tn):
    t, d = x2d.shape
    n = w_bf16.shape[1]
    tm = min(tm, rows_per_batch)
    per = rows_per_batch // tm
    return pl.pallas_call(
        _inproj_kernel,
        out_shape=jax.ShapeDtypeStruct((t, n), F32),
        grid=(t // tm, n // tn),
        in_specs=[
            pl.BlockSpec((tm, d), lambda i, j: (i, 0)),
            pl.BlockSpec((1, d), lambda i, j: (0, 0)),
            pl.BlockSpec((1, 1, d), lambda i, j: (i // per, 0, 0)),
            pl.BlockSpec((1, 1, d), lambda i, j: (i // per, 0, 0)),
            pl.BlockSpec((d, tn), lambda i, j: (0, j)),
        ],
        out_specs=pl.BlockSpec((tm, tn), lambda i, j: (i, j)),
        scratch_shapes=[pltpu.VMEM((tm, d), BF16)],
        compiler_params=_cparams(("parallel", "arbitrary")),
    )(x2d, g.reshape(1, d), scale, shift, w_bf16)


def _hgrn_exponent_matrices():
    c, nl = HG_CHUNK, HG_LEVELS
    rows = (nl + 2) * c + 8
    out = np.zeros((2, rows, c), np.float32)
    for d in range(2):
        pos = np.arange(c) if d == 0 else c - 1 - np.arange(c)
        for l in range(nl):
            m = 1 << l
            for t in range(c):
                p = pos[t]
                mid = (p // (2 * m)) * 2 * m + m
                if p >= mid:
                    sel = (pos >= mid) & (pos <= p)
                else:
                    sel = (pos > p) & (pos <= mid - 1)
                out[d, l * c + t, sel] = 1.0
        for t in range(c):
            out[d, nl * c + t, pos <= pos[t]] = 1.0
            out[d, (nl + 1) * c + t, pos > pos[t]] = 1.0
        out[d, (nl + 2) * c:, :] = 1.0
    return np.concatenate([out, out], axis=2)


_HG_EXP_MATS = _hgrn_exponent_matrices()


def _sigmoid(x):
    return 1.0 / (1.0 + jnp.exp(-x))


def _hgrn_kernel(q_ref, f_ref, i_ref, lb_ref, mat_ref, s0_ref, o_ref, st_ref, *, n_chunks):
    c, nl = HG_CHUNK, HG_LEVELS
    d = pl.program_id(1)

    @pl.when(pl.program_id(2) == 0)
    def _():
        st_ref[...] = s0_ref[...]

    row = lax.broadcasted_iota(jnp.int32, (c, LANES), 0)
    pos = row + d * (c - 1 - 2 * row)
    xor_rc = lax.broadcasted_iota(jnp.int32, (c, c), 0) ^ lax.broadcasted_iota(jnp.int32, (c, c), 1)
    mat = mat_ref[0]

    def chunk(ci, carry):
        cc = ci + d * (n_chunks - 1 - 2 * ci)
        r0 = pl.multiple_of(cc * c, c)
        for h in range(HG_HEADS):
            lanes = slice(h * LANES, (h + 1) * LANES)
            qraw = q_ref[0, pl.ds(r0, c), lanes]
            fz = f_ref[0, pl.ds(r0, c), lanes]
            v = i_ref[0, pl.ds(r0, c), lanes]
            lbh = lb_ref[:, lanes]
            sig = _sigmoid(fz)
            logf = jnp.log(jnp.maximum(lbh + (1.0 - lbh) * sig, TINY))
            key = (1.0 - lbh) * (1.0 - sig)
            qh = qraw * _sigmoid(qraw) * (HG_KDIM ** -0.5)
            hi = logf.astype(BF16)
            lo = (logf - hi.astype(F32)).astype(BF16)
            expo = jnp.dot(mat, jnp.concatenate([hi, lo], axis=0), preferred_element_type=F32)
            dec = jnp.exp(expo)
            a = None
            for l in range(nl):
                dl = dec[l * c:(l + 1) * c]
                is_q = ((pos >> l) & 1) == 1
                ql = jnp.where(is_q, qh * dl, 0.0).astype(BF16)
                kl = jnp.where(is_q, 0.0, key * dl).astype(BF16)
                pl_ = lax.dot_general(ql, kl, (((1,), (1,)), ((), ())), preferred_element_type=F32)
                a = pl_ if a is None else pl_ + jnp.where(xor_rc < (1 << l), a, 0.0)
            vb = v.astype(BF16)
            st = st_ref[0, 0, h]
            q_in = (qh * dec[nl * c:(nl + 1) * c]).astype(BF16)
            k_out = (key * dec[(nl + 1) * c:(nl + 2) * c]).astype(BF16)
            o = (jnp.dot(a.astype(BF16), vb, preferred_element_type=F32)
                 + jnp.sum(qh * key, axis=1, keepdims=True) * v
                 + lax.dot_general(q_in, st.astype(BF16), (((1,), (1,)), ((), ())),
                                   preferred_element_type=F32))
            o_ref[0, 0, pl.ds(r0, c), lanes] = o
            total = dec[(nl + 2) * c:(nl + 2) * c + 1]
            st_ref[0, 0, h] = total * st + lax.dot_general(
                vb, k_out, (((0,), (0,)), ((), ())), preferred_element_type=F32)
        return carry

    lax.fori_loop(0, n_chunks, chunk, 0)


def hgrn_scan(p3, lb, s0, tb):
    b, l, _ = p3.shape
    tb = min(tb, l)
    nb = l // tb
    blk = lambda d, n: n + d * (nb - 1 - 2 * n)
    wblk = HG_W
    kern = functools.partial(_hgrn_kernel, n_chunks=tb // HG_CHUNK)
    mats = jnp.asarray(_HG_EXP_MATS, BF16)
    return pl.pallas_call(
        kern,
        out_shape=(jax.ShapeDtypeStruct((2, b, l, HG_W), F32),
                   jax.ShapeDtypeStruct((b, 2, HG_HEADS, HG_KDIM, HG_KDIM), F32)),
        grid=(b, 2, nb),
        in_specs=[
            pl.BlockSpec((1, tb, wblk), lambda bi, d, n: (bi, blk(d, n), COL["hq"] // wblk)),
            pl.BlockSpec((1, tb, wblk), lambda bi, d, n: (bi, blk(d, n), COL["ff"] // wblk + d)),
            pl.BlockSpec((1, tb, wblk), lambda bi, d, n: (bi, blk(d, n), COL["hi"] // wblk)),
            pl.BlockSpec((1, HG_W), lambda bi, d, n: (0, 0)),
            pl.BlockSpec((1,) + _HG_EXP_MATS.shape[1:], lambda bi, d, n: (d, 0, 0)),
            pl.BlockSpec((1, 1, HG_HEADS, HG_KDIM, HG_KDIM), lambda bi, d, n: (bi, d, 0, 0, 0)),
        ],
        out_specs=(
            pl.BlockSpec((1, 1, tb, HG_W), lambda bi, d, n: (d, bi, blk(d, n), 0)),
            pl.BlockSpec((1, 1, HG_HEADS, HG_KDIM, HG_KDIM), lambda bi, d, n: (bi, d, 0, 0, 0)),
        ),
        compiler_params=_cparams(("parallel", "arbitrary", "arbitrary")),
    )(p3, p3, p3, lb.reshape(1, HG_W), mats, s0)


_GROUP_ONES = np.kron(np.eye(LANES // HEAD_DIM, dtype=np.float32), np.ones((HEAD_DIM, HEAD_DIM), np.float32))
_GROUP_ONES2 = np.concatenate([_GROUP_ONES, _GROUP_ONES], axis=0)


def _prep_tile(x, g, cos, sins, ones2, norm, rope):
    if norm:
        sq = x * x
        hi = sq.astype(BF16)
        lo = (sq - hi.astype(F32)).astype(BF16)
        ssq = jnp.dot(jnp.concatenate([hi, lo], axis=1), ones2, preferred_element_type=F32)
        x = x * lax.rsqrt(ssq * (1.0 / HEAD_DIM) + EPS) * g
    if rope:
        lane = lax.broadcasted_iota(jnp.int32, x.shape, 1)
        first = (lane % HEAD_DIM) < (HEAD_DIM // 2)
        other = jnp.where(first, pltpu.roll(x, LANES - HEAD_DIM // 2, 1), pltpu.roll(x, HEAD_DIM // 2, 1))
        x = x * cos + other * sins
    return x


def _prep_q_kernel(x_ref, g_ref, cos_ref, sin_ref, ones_ref, o_ref, *, norm, rope):
    lane = lax.broadcasted_iota(jnp.int32, (x_ref.shape[1], LANES), 1)
    low = lane < HEAD_DIM
    for tile in range(ATT_W // LANES):
        x = x_ref[0, :, tile * LANES:(tile + 1) * LANES]
        y = _prep_tile(x, g_ref[...], cos_ref[...], sin_ref[...], ones_ref[...], norm, rope) * ATTN_SCALE
        y_sw = pltpu.roll(y, HEAD_DIM, 1)
        for half in range(2):
            h = 2 * tile + half
            kv = h // GROUP
            src = y if half == kv else y_sw
            keep = low if kv == 0 else jnp.logical_not(low)
            o_ref[0, :, h * LANES:(h + 1) * LANES] = jnp.where(keep, src, 0.0).astype(BF16)


def _prep_kv_kernel(k_ref, v_ref, g_ref, cos_ref, sin_ref, ones_ref, kt_ref, vo_ref, *, norm, rope):
    k = _prep_tile(k_ref[0], g_ref[...], cos_ref[...], sin_ref[...], ones_ref[...], norm, rope)
    kt_ref[0] = k.T.astype(BF16)
    vo_ref[0] = v_ref[0].astype(BF16)


def prep_q(p3, col, g128, cos, sins, norm, rope, tq):
    b, l, _ = p3.shape
    tq = min(tq, l)
    ones2 = jnp.asarray(_GROUP_ONES2, BF16)
    return pl.pallas_call(
        functools.partial(_prep_q_kernel, norm=norm, rope=rope),
        out_shape=jax.ShapeDtypeStruct((b, l, N_HEADS * LANES), BF16),
        grid=(b, l // tq),
        in_specs=[
            pl.BlockSpec((1, tq, ATT_W), lambda bi, i: (bi, i, col // ATT_W)),
            pl.BlockSpec((1, LANES), lambda bi, i: (0, 0)),
            pl.BlockSpec((tq, LANES), lambda bi, i: (i, 0)),
            pl.BlockSpec((tq, LANES), lambda bi, i: (i, 0)),
            pl.BlockSpec((2 * LANES, LANES), lambda bi, i: (0, 0)),
        ],
        out_specs=pl.BlockSpec((1, tq, N_HEADS * LANES), lambda bi, i: (bi, i, 0)),
        compiler_params=_cparams(("parallel", "parallel")),
    )(p3, g128, cos, sins, ones2)


def prep_kv(p3, col_k, col_v, g128, cos, sins, norm, rope, tq):
    b, l, _ = p3.shape
    tq = min(tq, l)
    ones2 = jnp.asarray(_GROUP_ONES2, BF16)
    return pl.pallas_call(
        functools.partial(_prep_kv_kernel, norm=norm, rope=rope),
        out_shape=(jax.ShapeDtypeStruct((b, KV_W, l), BF16), jax.ShapeDtypeStruct((b, l, KV_W), BF16)),
        grid=(b, l // tq),
        in_specs=[
            pl.BlockSpec((1, tq, KV_W), lambda bi, i: (bi, i, col_k // KV_W)),
            pl.BlockSpec((1, tq, KV_W), lambda bi, i: (bi, i, col_v // KV_W)),
            pl.BlockSpec((1, LANES), lambda bi, i: (0, 0)),
            pl.BlockSpec((tq, LANES), lambda bi, i: (i, 0)),
            pl.BlockSpec((tq, LANES), lambda bi, i: (i, 0)),
            pl.BlockSpec((2 * LANES, LANES), lambda bi, i: (0, 0)),
        ],
        out_specs=(pl.BlockSpec((1, KV_W, tq), lambda bi, i: (bi, 0, i)),
                   pl.BlockSpec((1, tq, KV_W), lambda bi, i: (bi, i, 0))),
        compiler_params=_cparams(("parallel", "parallel")),
    )(p3, p3, g128, cos, sins, ones2)


def _softmax_step(qg, kt, v, m, l, acc, mask=None):
    s = jnp.dot(qg, kt, preferred_element_type=F32)
    if mask is not None:
        s = jnp.where(mask, s, MASK_VALUE)
    m_new = jnp.maximum(m, jnp.max(s, axis=1, keepdims=True))
    alpha = jnp.exp(m - m_new)
    p = jnp.exp(s - m_new)
    l = alpha * l + jnp.sum(p, axis=1, keepdims=True)
    acc = alpha * acc + jnp.dot(p.astype(BF16), v, preferred_element_type=F32)
    return m_new, l, acc


def _write_heads(o_ref, acc, l, g, tq):
    o = acc * (1.0 / l)
    lane = lax.broadcasted_iota(jnp.int32, (tq, LANES), 1)
    low = lane < HEAD_DIM
    for pair in range(GROUP // 2):
        a = o[(2 * pair) * tq:(2 * pair + 1) * tq]
        b = o[(2 * pair + 1) * tq:(2 * pair + 2) * tq]
        if g == 0:
            tile = jnp.where(low, a, pltpu.roll(b, HEAD_DIM, 1))
        else:
            tile = jnp.where(low, pltpu.roll(a, HEAD_DIM, 1), b)
        t0 = (g * GROUP // 2 + pair) * LANES
        o_ref[0, :, t0:t0 + LANES] = tile.astype(o_ref.dtype)


def _group_q(q_ref, g):
    return jnp.concatenate([q_ref[0, :, (g * GROUP + hh) * LANES:(g * GROUP + hh + 1) * LANES]
                            for hh in range(GROUP)], axis=0)


def _init_stats(sink_ref, g, tq, use_sink):
    rows = GROUP * tq
    if use_sink:
        m = jnp.concatenate([jnp.full((tq, 1), sink_ref[g * GROUP + hh], F32) for hh in range(GROUP)], axis=0)
        l = jnp.ones((rows, 1), F32)
    else:
        m = jnp.full((rows, 1), MASK_VALUE, F32)
        l = jnp.zeros((rows, 1), F32)
    return m, l, jnp.zeros((rows, LANES), F32)


def _dense_attn_kernel(sink_ref, q_ref, kt_ref, v_ref, o_ref, *, tk, use_sink):
    tq = q_ref.shape[1]
    nk = kt_ref.shape[2] // tk
    for g in range(N_KV):
        qg = _group_q(q_ref, g)

        def body(j, carry, qg=qg):
            k0 = pl.multiple_of(j * tk, tk)
            return _softmax_step(qg, kt_ref[0, :, pl.ds(k0, tk)], v_ref[0, pl.ds(k0, tk), :], *carry)

        m, l, acc = lax.fori_loop(0, nk, body, _init_stats(sink_ref, g, tq, use_sink))
        _write_heads(o_ref, acc, l, g, tq)


def _window_attn_kernel(sink_ref, q_ref, kt_ref, v_ref, o_ref, *, n_ctx, seq):
    tq = q_ref.shape[1]
    band = tq + 2 * WINDOW
    i = pl.program_id(1)
    start = pl.multiple_of(i * tq, tq)
    rows = GROUP * tq
    r = lax.broadcasted_iota(jnp.int32, (rows, band), 0) % tq
    kp = lax.broadcasted_iota(jnp.int32, (rows, band), 1)
    key_pos = start + kp - WINDOW
    mask = (jnp.abs(kp - WINDOW - r) <= WINDOW) & (key_pos >= 0) & (key_pos < seq)
    for g in range(N_KV):
        qg = _group_q(q_ref, g)
        carry = _init_stats(sink_ref, g, tq, True)
        carry = _softmax_step(qg, kt_ref[0, :, 0:n_ctx], v_ref[0, 0:n_ctx, :], *carry)
        b0 = pl.multiple_of(n_ctx + start, LANES)
        m, l, acc = _softmax_step(qg, kt_ref[0, :, pl.ds(b0, band)], v_ref[0, pl.ds(b0, band), :],
                                  *carry, mask=mask)
        _write_heads(o_ref, acc, l, g, tq)


def _attn_call(kern, q, kt, v, sink, tq):
    b, l, _ = q.shape
    sk = kt.shape[2]
    return pl.pallas_call(
        kern,
        out_shape=jax.ShapeDtypeStruct((b, l, ATT_W), BF16),
        grid_spec=pltpu.PrefetchScalarGridSpec(
            num_scalar_prefetch=1,
            grid=(b, l // tq),
            in_specs=[
                pl.BlockSpec((1, tq, N_HEADS * LANES), lambda bi, i, s: (bi, i, 0)),
                pl.BlockSpec((1, KV_W, sk), lambda bi, i, s: (bi, 0, 0)),
                pl.BlockSpec((1, sk, KV_W), lambda bi, i, s: (bi, 0, 0)),
            ],
            out_specs=pl.BlockSpec((1, tq, ATT_W), lambda bi, i, s: (bi, i, 0)),
        ),
        compiler_params=_cparams(("parallel", "arbitrary")),
    )(sink, q, kt, v)


def dense_attention(q, kt, v, sink, use_sink, tq, tk):
    tq = min(tq, q.shape[1])
    tk = min(tk, kt.shape[2])
    kern = functools.partial(_dense_attn_kernel, tk=tk, use_sink=use_sink)
    return _attn_call(kern, q, kt, v, sink, tq)


def window_attention(q, kt, v, sink, n_ctx, tq):
    kern = functools.partial(_window_attn_kernel, n_ctx=n_ctx, seq=q.shape[1])
    return _attn_call(kern, q, kt, v, sink, tq)


def _merge_kernel(of_ref, ob_ref, hg_ref, b_ref, c_ref, ga_ref, gb_ref, gc_ref, x_ref,
                  wa_ref, wb_ref, wc_ref, wo_ref, hgn_ref, gate1_ref, n2_ref, sc2_ref, sh2_ref, wr_ref,
                  xo_ref, h2_ref, lg_ref):
    o = of_ref[0] + ob_ref[0]
    tiles = []
    for h in range(HG_HEADS):
        t = o[:, h * LANES:(h + 1) * LANES]
        ms = jnp.mean(t * t, axis=-1, keepdims=True)
        tiles.append(t * lax.rsqrt(ms + EPS) * hgn_ref[...])
    hg = hg_ref[...]
    a = (jnp.concatenate(tiles, axis=1) * (hg * _sigmoid(hg))).astype(BF16)
    merged = (_sigmoid(ga_ref[...]) * jnp.dot(a, wa_ref[...], preferred_element_type=F32)
              + _sigmoid(gb_ref[...]) * jnp.dot(b_ref[...], wb_ref[...], preferred_element_type=F32)
              + _sigmoid(gc_ref[...]) * jnp.dot(c_ref[...], wc_ref[...], preferred_element_type=F32))
    y = jnp.dot(merged.astype(BF16), wo_ref[...], preferred_element_type=F32)
    x = x_ref[...] + gate1_ref[0] * y
    xo_ref[...] = x
    ms = jnp.mean(x * x, axis=-1, keepdims=True)
    h2 = (x * lax.rsqrt(ms + EPS) * n2_ref[...]) * (1.0 + sc2_ref[0]) + sh2_ref[0]
    h2_ref[...] = h2.astype(BF16)
    lg_ref[...] = jnp.dot(h2, wr_ref[...], preferred_element_type=F32, precision=lax.Precision.HIGHEST)


def merge(o_fb, p2d, b2d, c2d, x2d, rows_per_batch, wa, wb, wc, wo, hgn, gate1, n2, sc2, sh2, w_route, tm):
    t, d = x2d.shape
    tm = min(tm, rows_per_batch)
    per = rows_per_batch // tm
    row = lambda i: (i, 0)
    const = lambda i: (0, 0)
    bat = lambda i: (i // per, 0, 0)
    gcol = COL["gates"] // d
    return pl.pallas_call(
        _merge_kernel,
        out_shape=(jax.ShapeDtypeStruct((t, d), F32), jax.ShapeDtypeStruct((t, d), BF16),
                   jax.ShapeDtypeStruct((t, LANES), F32)),
        grid=(t // tm,),
        in_specs=[
            pl.BlockSpec((1, tm, HG_W), lambda i: (0, i, 0)),
            pl.BlockSpec((1, tm, HG_W), lambda i: (1, i, 0)),
            pl.BlockSpec((tm, HG_W), lambda i: (i, COL["hg"] // HG_W)),
            pl.BlockSpec((tm, ATT_W), row),
            pl.BlockSpec((tm, ATT_W), row),
            pl.BlockSpec((tm, d), lambda i: (i, gcol)),
            pl.BlockSpec((tm, d), lambda i: (i, gcol + 1)),
            pl.BlockSpec((tm, d), lambda i: (i, gcol + 2)),
            pl.BlockSpec((tm, d), row),
            pl.BlockSpec((HG_W, d), const),
            pl.BlockSpec((ATT_W, d), const),
            pl.BlockSpec((ATT_W, d), const),
            pl.BlockSpec((d, d), const),
            pl.BlockSpec((1, LANES), const),
            pl.BlockSpec((1, 1, d), bat),
            pl.BlockSpec((1, d), const),
            pl.BlockSpec((1, 1, d), bat),
            pl.BlockSpec((1, 1, d), bat),
            pl.BlockSpec((d, LANES), const),
        ],
        out_specs=(pl.BlockSpec((tm, d), row), pl.BlockSpec((tm, d), row), pl.BlockSpec((tm, LANES), row)),
        compiler_params=_cparams(("parallel",)),
    )(o_fb, o_fb, p2d, b2d, c2d, p2d, p2d, p2d, x2d, wa, wb, wc, wo, hgn.reshape(1, LANES),
      gate1, n2.reshape(1, d), sc2, sh2, w_route)


def _expert_kernel(be_ref, x_ref, wrow_ref, wg_ref, wu_ref, wd_ref, o_ref):
    x = x_ref[...]
    gte = jnp.dot(x, wg_ref[0], preferred_element_type=F32)
    up = jnp.dot(x, wu_ref[0], preferred_element_type=F32)
    hid = (gte * _sigmoid(gte) * up).astype(BF16)
    o_ref[...] = jnp.dot(hid, wd_ref[0], preferred_element_type=F32) * wrow_ref[...]


def expert_ffn(rows, row_w, block_expert, wg, wu, wd):
    n, d = rows.shape
    de = wg.shape[2]
    return pl.pallas_call(
        _expert_kernel,
        out_shape=jax.ShapeDtypeStruct((n, d), F32),
        grid_spec=pltpu.PrefetchScalarGridSpec(
            num_scalar_prefetch=1,
            grid=(n // MOE_ROWS,),
            in_specs=[
                pl.BlockSpec((MOE_ROWS, d), lambda i, be: (i, 0)),
                pl.BlockSpec((MOE_ROWS, 1), lambda i, be: (i, 0)),
                pl.BlockSpec((1, d, de), lambda i, be: (be[i], 0, 0)),
                pl.BlockSpec((1, d, de), lambda i, be: (be[i], 0, 0)),
                pl.BlockSpec((1, de, d), lambda i, be: (be[i], 0, 0)),
            ],
            out_specs=pl.BlockSpec((MOE_ROWS, d), lambda i, be: (i, 0)),
        ),
        compiler_params=_cparams(("arbitrary",)),
    )(block_expert, rows, row_w, wg, wu, wd)


def _final_norm_kernel(x_ref, r_ref, gate_ref, g_ref, o_ref):
    x = x_ref[...] + gate_ref[0] * r_ref[...]
    ms = jnp.mean(x * x, axis=-1, keepdims=True)
    o_ref[...] = x * lax.rsqrt(ms + EPS) * g_ref[...]


def _residual_kernel(x_ref, r_ref, gate_ref, o_ref):
    o_ref[...] = x_ref[...] + gate_ref[0] * r_ref[...]


def residual(x2d, r2d, gate, rows_per_batch, final_g=None, tm=512):
    t, d = x2d.shape
    tm = min(tm, rows_per_batch)
    per = rows_per_batch // tm
    row = lambda i: (i, 0)
    specs = [pl.BlockSpec((tm, d), row), pl.BlockSpec((tm, d), row),
             pl.BlockSpec((1, 1, d), lambda i: (i // per, 0, 0))]
    args = [x2d, r2d, gate]
    kern = _residual_kernel
    if final_g is not None:
        specs.append(pl.BlockSpec((1, d), lambda i: (0, 0)))
        args.append(final_g.reshape(1, d))
        kern = _final_norm_kernel
    return pl.pallas_call(
        kern, out_shape=jax.ShapeDtypeStruct((t, d), F32), grid=(t // tm,),
        in_specs=specs, out_specs=pl.BlockSpec((tm, d), row),
        compiler_params=_cparams(("parallel",)),
    )(*args)


def _route(logits):
    n_tok = logits.shape[0]
    grp_prob = jax.nn.softmax(logits[:, :N_GROUPS], axis=-1)
    grp_w, grp_idx = lax.top_k(grp_prob, 1)
    exp_logits = logits[:, N_GROUPS:N_GROUPS + N_EXPERTS].reshape(n_tok, N_GROUPS, EXP_PER_GROUP)
    in_grp = jnp.take_along_axis(exp_logits, grp_idx[:, :, None], axis=1)[:, 0]
    top_logit, top_j = lax.top_k(in_grp, TOP_K)
    weight = (jax.nn.softmax(top_logit, axis=-1) * grp_w).reshape(-1)
    expert = (grp_idx * EXP_PER_GROUP + top_j).reshape(-1).astype(jnp.int32)
    n_assign = n_tok * TOP_K
    order = jnp.argsort(expert, stable=True)
    e_s = expert[order]
    counts = jnp.bincount(expert, length=N_EXPERTS)
    padded = (counts + MOE_ROWS - 1) // MOE_ROWS * MOE_ROWS
    pad_end = jnp.cumsum(padded)
    pad_start = pad_end - padded
    start = jnp.cumsum(counts) - counts
    dest = (pad_start[e_s] + jnp.arange(n_assign) - start[e_s]).astype(jnp.int32)
    n_blocks = -(-(n_assign + N_EXPERTS * (MOE_ROWS - 1)) // MOE_ROWS)
    n_rows = n_blocks * MOE_ROWS
    src_tok = jnp.zeros((n_rows,), jnp.int32).at[dest].set((order // TOP_K).astype(jnp.int32))
    row_w = jnp.zeros((n_rows,), F32).at[dest].set(weight[order])
    pos = jnp.zeros((n_assign,), jnp.int32).at[order].set(dest).reshape(n_tok, TOP_K)
    block_expert = jnp.minimum(
        jnp.searchsorted(pad_end, jnp.arange(n_blocks) * MOE_ROWS, side="right"), N_EXPERTS - 1).astype(jnp.int32)
    return src_tok, row_w.reshape(n_rows, 1), pos, block_expert


def _rope_tables(seq):
    n_rows = seq // GRID_W
    row = jnp.repeat(jnp.arange(n_rows), GRID_W).astype(F32)
    col = jnp.tile(jnp.arange(GRID_W), n_rows).astype(F32)
    axis_pairs = HEAD_DIM // 4
    inv = ROPE_THETA ** (-jnp.arange(axis_pairs, dtype=F32) / axis_pairs)
    ang = jnp.concatenate([row[:, None] * inv, col[:, None] * inv], axis=-1)
    cos, sin = jnp.cos(ang), jnp.sin(ang)
    cos128 = jnp.concatenate([cos, cos, cos, cos], axis=-1)
    sins128 = jnp.concatenate([-sin, sin, -sin, sin], axis=-1)
    return cos128, sins128


def kernel(x, c, ctx, c_ctx, w_mod, b_mod, norm1_g, norm2_g, w_in, hgrn_lb_logits, hgrn_out_norm_g,
           attn_q_norm_g, attn_k_norm_g, swa_sink, w_branch_a, w_branch_b, w_branch_c, w_out,
           w_group, w_router, w_exp_gate, w_exp_up, w_exp_down, final_norm_g):
    bsz, seq, d = x.shape
    n_ctx = ctx.shape[1]
    depth = w_mod.shape[0]
    cos, sins = _rope_tables(seq)
    lb_p = jax.nn.softmax(hgrn_lb_logits.astype(F32), axis=0)
    lower_bounds = jnp.cumsum(lb_p, axis=0) - lb_p[0]
    zeros_sink = jnp.zeros((N_HEADS,), F32)
    ones128 = jnp.ones((1, LANES), F32)

    xl = x.reshape(bsz * seq, d)
    xc = ctx.reshape(bsz * n_ctx, d)
    for layer in range(depth):
        ctx_out = layer < depth - 1
        mod_l = jax.nn.silu(c) @ w_mod[layer] + b_mod[layer]
        mod_c = jnp.broadcast_to(jax.nn.silu(c_ctx) @ w_mod[layer] + b_mod[layer], (bsz, 6 * d))
        ml = [m.reshape(bsz, 1, d) for m in jnp.split(mod_l, 6, axis=-1)]
        mc = [m.reshape(bsz, 1, d) for m in jnp.split(mod_c, 6, axis=-1)]

        w_in_b = w_in[layer][:, _COL_PERM].astype(BF16)
        p_l = inproj(xl, seq, norm1_g[layer], ml[1], ml[0], w_in_b, 1024, 1024)
        p_c = inproj(xc, n_ctx, norm1_g[layer], mc[1], mc[0], w_in_b, 256, 1024)
        p_l3 = p_l.reshape(bsz, seq, D_IN)
        p_c3 = p_c.reshape(bsz, n_ctx, D_IN)

        s0 = jnp.zeros((bsz, 2, HG_HEADS, HG_KDIM, HG_KDIM), F32)
        o_c, s_c = hgrn_scan(p_c3, lower_bounds[layer], s0, 256)
        o_l, _ = hgrn_scan(p_l3, lower_bounds[layer], s_c, 512)

        gq = jnp.tile(attn_q_norm_g[layer], 2).reshape(1, LANES)
        gk = jnp.tile(attn_k_norm_g[layer], 2).reshape(1, LANES)
        q_l = prep_q(p_l3, COL["aq"], gq, cos, sins, True, True, 512)
        kt_l, v_l = prep_kv(p_l3, COL["ak"], COL["av"], gk, cos, sins, True, True, 512)
        kt_c, v_c = prep_kv(p_c3, COL["ak"], COL["av"], gk, cos, sins, True, False, 256)
        kt_all = jnp.concatenate([kt_c, kt_l], axis=2)
        v_all = jnp.concatenate([v_c, v_l], axis=1)
        b_l = dense_attention(q_l, kt_all, v_all, zeros_sink, False, 256, 640)

        sq_l = prep_q(p_l3, COL["sq"], ones128, cos, sins, False, True, 512)
        skt_l, sv_l = prep_kv(p_l3, COL["sk"], COL["sv"], ones128, cos, sins, False, True, 512)
        skt_c, sv_c = prep_kv(p_c3, COL["sk"], COL["sv"], ones128, cos, sins, False, False, 256)
        zk = jnp.zeros((bsz, KV_W, WINDOW), BF16)
        zv = jnp.zeros((bsz, WINDOW, KV_W), BF16)
        skt_all = jnp.concatenate([skt_c, zk, skt_l, zk], axis=2)
        sv_all = jnp.concatenate([sv_c, zv, sv_l, zv], axis=1)
        c_l = window_attention(sq_l, skt_all, sv_all, swa_sink[layer], n_ctx, 256)

        wa = w_branch_a[layer].astype(BF16)
        wb = w_branch_b[layer].astype(BF16)
        wc = w_branch_c[layer].astype(BF16)
        wo = w_out[layer].astype(BF16)
        w_route = jnp.concatenate(
            [w_group[layer], w_router[layer], jnp.zeros((d, LANES - N_GROUPS - N_EXPERTS), F32)], axis=1)
        xl, h_l, lg_l = merge(o_l.reshape(2, bsz * seq, HG_W), p_l, b_l.reshape(-1, ATT_W),
                              c_l.reshape(-1, ATT_W), xl, seq, wa, wb, wc, wo, hgrn_out_norm_g[layer],
                              ml[2], norm2_g[layer], ml[4], ml[3], w_route, 256)
        if ctx_out:
            q_c = prep_q(p_c3, COL["aq"], gq, cos, sins, True, False, 256)
            b_c = dense_attention(q_c, kt_c, v_c, zeros_sink, False, 256, 256)
            sq_c = prep_q(p_c3, COL["sq"], ones128, cos, sins, False, False, 256)
            c_c = dense_attention(sq_c, skt_c, sv_c, swa_sink[layer], True, 256, 256)
            xc, h_c, lg_c = merge(o_c.reshape(2, bsz * n_ctx, HG_W), p_c, b_c.reshape(-1, ATT_W),
                                  c_c.reshape(-1, ATT_W), xc, n_ctx, wa, wb, wc, wo, hgrn_out_norm_g[layer],
                                  mc[2], norm2_g[layer], mc[4], mc[3], w_route, 256)
            h_all = jnp.concatenate([h_c, h_l], axis=0)
            lg_all = jnp.concatenate([lg_c, lg_l], axis=0)
        else:
            h_all, lg_all = h_l, lg_l

        src_tok, row_w, pos, block_expert = _route(lg_all)
        rows = jnp.take(h_all, src_tok, axis=0)
        out_rows = expert_ffn(rows, row_w, block_expert, w_exp_gate[layer].astype(BF16),
                              w_exp_up[layer].astype(BF16), w_exp_down[layer].astype(BF16))
        y = jnp.take(out_rows, pos[:, 0], axis=0) + jnp.take(out_rows, pos[:, 1], axis=0)
        last = layer == depth - 1
        if ctx_out:
            n_c = bsz * n_ctx
            xc = residual(xc, y[:n_c], mc[5], n_ctx)
            y = y[n_c:]
        xl = residual(xl, y, ml[5], seq, final_g=final_norm_g if last else None)
    return xl.reshape(bsz, seq, d)
```

```python
import functools
import math

import numpy as np
import jax
import jax.numpy as jnp
from jax import lax
from jax.experimental import pallas as pl
from jax.experimental.pallas import tpu as pltpu

F32 = jnp.float32
BF16 = jnp.bfloat16

EPS = 1e-6
MASK_VALUE = -1e30
TINY = 1e-30
GRID_W = 64
ROPE_THETA = 10000.0

HG_HEADS = 4
HG_KDIM = 128
HG_W = HG_HEADS * HG_KDIM
HEAD_DIM = 64
N_HEADS = 8
N_KV = 2
GROUP = N_HEADS // N_KV
ATT_W = N_HEADS * HEAD_DIM
KV_W = N_KV * HEAD_DIM
WINDOW = 128
ATTN_SCALE = HEAD_DIM ** -0.5
N_GROUPS = 4
EXP_PER_GROUP = 8
N_EXPERTS = N_GROUPS * EXP_PER_GROUP
TOP_K = 2

LANES = 128
HG_CHUNK = 128
HG_LEVELS = int(math.log2(HG_CHUNK))
MOE_ROWS = 256
VMEM_LIMIT = 56 * 1024 * 1024


def _cparams(sem):
    return pltpu.CompilerParams(dimension_semantics=sem, vmem_limit_bytes=VMEM_LIMIT)


_REF_SIZES = (HG_W, HG_W, HG_W, HG_W, HG_W, ATT_W, KV_W, KV_W, ATT_W, KV_W, KV_W, 3 * 1024)
_REF_NAMES = ("hq", "ff", "fb", "hi", "hg", "aq", "ak", "av", "sq", "sk", "sv", "gates")
_NEW_ORDER = ("hq", "ff", "fb", "hi", "hg", "aq", "sq", "ak", "av", "sk", "sv", "gates")


def _column_permutation():
    starts, acc = {}, 0
    for n, s in zip(_REF_NAMES, _REF_SIZES):
        starts[n] = (acc, s)
        acc += s
    perm, new_starts, pos = [], {}, 0
    for n in _NEW_ORDER:
        a, s = starts[n]
        perm.append(np.arange(a, a + s))
        new_starts[n] = pos
        pos += s
    return np.concatenate(perm), new_starts


_COL_PERM, COL = _column_permutation()
D_IN = int(_COL_PERM.shape[0])


def _inproj_kernel(x_ref, g_ref, sc_ref, sh_ref, w_ref, o_ref, h_scr):
    @pl.when(pl.program_id(1) == 0)
    def _():
        x = x_ref[...]
        ms = jnp.mean(x * x, axis=-1, keepdims=True)
        y = x * lax.rsqrt(ms + EPS) * g_ref[...]
        h_scr[...] = (y * (1.0 + sc_ref[0]) + sh_ref[0]).astype(BF16)

    o_ref[...] = jnp.dot(h_scr[...], w_ref[...], preferred_element_type=F32)


def inproj(x2d, rows_per_batch, g, scale, shift, w_bf16, tm, tn):
    t, d = x2d.shape
    n = w_bf16.shape[1]
    tm = min(tm, rows_per_batch)
    per = rows_per_batch // tm
    return pl.pallas_call(
        _inproj_kernel,
        out_shape=jax.ShapeDtypeStruct((t, n), F32),
        grid=(t // tm, n // tn),
        in_specs=[
            pl.BlockSpec((tm, d), lambda i, j: (i, 0)),
            pl.BlockSpec((1, d), lambda i, j: (0, 0)),
            pl.BlockSpec((1, 1, d), lambda i, j: (i // per, 0, 0)),
            pl.BlockSpec((1, 1, d), lambda i, j: (i // per, 0, 0)),
            pl.BlockSpec((d, tn), lambda i, j: (0, j)),
        ],
        out_specs=pl.BlockSpec((tm, tn), lambda i, j: (i, j)),
        scratch_shapes=[pltpu.VMEM((tm, d), BF16)],
        compiler_params=_cparams(("parallel", "arbitrary")),
    )(x2d, g.reshape(1, d), scale, shift, w_bf16)


def _hgrn_exponent_matrices():
    c, nl = HG_CHUNK, HG_LEVELS
    rows = (nl + 2) * c + 8
    out = np.zeros((2, rows, c), np.float32)
    for d in range(2):
        pos = np.arange(c) if d == 0 else c - 1 - np.arange(c)
        for l in range(nl):
            m = 1 << l
            for t in range(c):
                p = pos[t]
                mid = (p // (2 * m)) * 2 * m + m
                if p >= mid:
                    sel = (pos >= mid) & (pos <= p)
                else:
                    sel = (pos > p) & (pos <= mid - 1)
                out[d, l * c + t, sel] = 1.0
        for t in range(c):
            out[d, nl * c + t, pos <= pos[t]] = 1.0
            out[d, (nl + 1) * c + t, pos > pos[t]] = 1.0
        out[d, (nl + 2) * c:, :] = 1.0
    return np.concatenate([out, out], axis=2)


_HG_EXP_MATS = _hgrn_exponent_matrices()


def _sigmoid(x):
    return 1.0 / (1.0 + jnp.exp(-x))


def _hgrn_kernel(q_ref, f_ref, i_ref, lb_ref, mat_ref, s0_ref, o_ref, st_ref, *, n_chunks):
    c, nl = HG_CHUNK, HG_LEVELS
    d = pl.program_id(1)

    @pl.when(pl.program_id(2) == 0)
    def _():
        st_ref[...] = s0_ref[...]

    row = lax.broadcasted_iota(jnp.int32, (c, LANES), 0)
    pos = row + d * (c - 1 - 2 * row)
    xor_rc = lax.broadcasted_iota(jnp.int32, (c, c), 0) ^ lax.broadcasted_iota(jnp.int32, (c, c), 1)
    mat = mat_ref[0]

    def chunk(ci, carry):
        cc = ci + d * (n_chunks - 1 - 2 * ci)
        r0 = pl.multiple_of(cc * c, c)
        for h in range(HG_HEADS):
            lanes = slice(h * LANES, (h + 1) * LANES)
            qraw = q_ref[0, pl.ds(r0, c), lanes]
            fz = f_ref[0, pl.ds(r0, c), lanes]
            v = i_ref[0, pl.ds(r0, c), lanes]
            lbh = lb_ref[:, lanes]
            sig = _sigmoid(fz)
            logf = jnp.log(jnp.maximum(lbh + (1.0 - lbh) * sig, TINY))
            key = (1.0 - lbh) * (1.0 - sig)
            qh = qraw * _sigmoid(qraw) * (HG_KDIM ** -0.5)
            hi = logf.astype(BF16)
            lo = (logf - hi.astype(F32)).astype(BF16)
            expo = jnp.dot(mat, jnp.concatenate([hi, lo], axis=0), preferred_element_type=F32)
            dec = jnp.exp(expo)
            a = None
            for l in range(nl):
                dl = dec[l * c:(l + 1) * c]
                is_q = ((pos >> l) & 1) == 1
                ql = jnp.where(is_q, qh * dl, 0.0).astype(BF16)
                kl = jnp.where(is_q, 0.0, key * dl).astype(BF16)
                pl_ = lax.dot_general(ql, kl, (((1,), (1,)), ((), ())), preferred_element_type=F32)
                a = pl_ if a is None else pl_ + jnp.where(xor_rc < (1 << l), a, 0.0)
            vb = v.astype(BF16)
            st = st_ref[0, 0, h]
            q_in = (qh * dec[nl * c:(nl + 1) * c]).astype(BF16)
            k_out = (key * dec[(nl + 1) * c:(nl + 2) * c]).astype(BF16)
            o = (jnp.dot(a.astype(BF16), vb, preferred_element_type=F32)
                 + jnp.sum(qh * key, axis=1, keepdims=True) * v
                 + lax.dot_general(q_in, st.astype(BF16), (((1,), (1,)), ((), ())),
                                   preferred_element_type=F32))
            o_ref[0, 0, pl.ds(r0, c), lanes] = o
            total = dec[(nl + 2) * c:(nl + 2) * c + 1]
            st_ref[0, 0, h] = total * st + lax.dot_general(
                vb, k_out, (((0,), (0,)), ((), ())), preferred_element_type=F32)
        return carry

    lax.fori_loop(0, n_chunks, chunk, 0)


def hgrn_scan(p3, lb, s0, tb):
    b, l, _ = p3.shape
    tb = min(tb, l)
    nb = l // tb
    blk = lambda d, n: n + d * (nb - 1 - 2 * n)
    wblk = HG_W
    kern = functools.partial(_hgrn_kernel, n_chunks=tb // HG_CHUNK)
    mats = jnp.asarray(_HG_EXP_MATS, BF16)
    return pl.pallas_call(
        kern,
        out_shape=(jax.ShapeDtypeStruct((2, b, l, HG_W), F32),
                   jax.ShapeDtypeStruct((b, 2, HG_HEADS, HG_KDIM, HG_KDIM), F32)),
        grid=(b, 2, nb),
        in_specs=[
            pl.BlockSpec((1, tb, wblk), lambda bi, d, n: (bi, blk(d, n), COL["hq"] // wblk)),
            pl.BlockSpec((1, tb, wblk), lambda bi, d, n: (bi, blk(d, n), COL["ff"] // wblk + d)),
            pl.BlockSpec((1, tb, wblk), lambda bi, d, n: (bi, blk(d, n), COL["hi"] // wblk)),
            pl.BlockSpec((1, HG_W), lambda bi, d, n: (0, 0)),
            pl.BlockSpec((1,) + _HG_EXP_MATS.shape[1:], lambda bi, d, n: (d, 0, 0)),
            pl.BlockSpec((1, 1, HG_HEADS, HG_KDIM, HG_KDIM), lambda bi, d, n: (bi, d, 0, 0, 0)),
        ],
        out_specs=(
            pl.BlockSpec((1, 1, tb, HG_W), lambda bi, d, n: (d, bi, blk(d, n), 0)),
            pl.BlockSpec((1, 1, HG_HEADS, HG_KDIM, HG_KDIM), lambda bi, d, n: (bi, d, 0, 0, 0)),
        ),
        compiler_params=_cparams(("parallel", "arbitrary", "arbitrary")),
    )(p3, p3, p3, lb.reshape(1, HG_W), mats, s0)


_GROUP_ONES = np.kron(np.eye(LANES // HEAD_DIM, dtype=np.float32), np.ones((HEAD_DIM, HEAD_DIM), np.float32))
_GROUP_ONES2 = np.concatenate([_GROUP_ONES, _GROUP_ONES], axis=0)


def _prep_tile(x, g, cos, sins, ones2, norm, rope):
    if norm:
        sq = x * x
        hi = sq.astype(BF16)
        lo = (sq - hi.astype(F32)).astype(BF16)
        ssq = jnp.dot(jnp.concatenate([hi, lo], axis=1), ones2, preferred_element_type=F32)
        x = x * lax.rsqrt(ssq * (1.0 / HEAD_DIM) + EPS) * g
    if rope:
        lane = lax.broadcasted_iota(jnp.int32, x.shape, 1)
        first = (lane % HEAD_DIM) < (HEAD_DIM // 2)
        other = jnp.where(first, pltpu.roll(x, LANES - HEAD_DIM // 2, 1), pltpu.roll(x, HEAD_DIM // 2, 1))
        x = x * cos + other * sins
    return x


def _prep_q_kernel(x_ref, g_ref, cos_ref, sin_ref, ones_ref, o_ref, *, norm, rope):
    lane = lax.broadcasted_iota(jnp.int32, (x_ref.shape[1], LANES), 1)
    low = lane < HEAD_DIM
    for tile in range(ATT_W // LANES):
        x = x_ref[0, :, tile * LANES:(tile + 1) * LANES]
        y = _prep_tile(x, g_ref[...], cos_ref[...], sin_ref[...], ones_ref[...], norm, rope) * ATTN_SCALE
        y_sw = pltpu.roll(y, HEAD_DIM, 1)
        for half in range(2):
            h = 2 * tile + half
            kv = h // GROUP
            src = y if half == kv else y_sw
            keep = low if kv == 0 else jnp.logical_not(low)
            o_ref[0, :, h * LANES:(h + 1) * LANES] = jnp.where(keep, src, 0.0).astype(BF16)


def _prep_kv_kernel(k_ref, v_ref, g_ref, cos_ref, sin_ref, ones_ref, kt_ref, vo_ref, *, norm, rope):
    k = _prep_tile(k_ref[0], g_ref[...], cos_ref[...], sin_ref[...], ones_ref[...], norm, rope)
    kt_ref[0] = k.T.astype(BF16)
    vo_ref[0] = v_ref[0].astype(BF16)


def prep_q(p3, col, g128, cos, sins, norm, rope, tq):
    b, l, _ = p3.shape
    tq = min(tq, l)
    ones2 = jnp.asarray(_GROUP_ONES2, BF16)
    return pl.pallas_call(
        functools.partial(_prep_q_kernel, norm=norm, rope=rope),
        out_shape=jax.ShapeDtypeStruct((b, l, N_HEADS * LANES), BF16),
        grid=(b, l // tq),
        in_specs=[
            pl.BlockSpec((1, tq, ATT_W), lambda bi, i: (bi, i, col // ATT_W)),
            pl.BlockSpec((1, LANES), lambda bi, i: (0, 0)),
            pl.BlockSpec((tq, LANES), lambda bi, i: (i, 0)),
            pl.BlockSpec((tq, LANES), lambda bi, i: (i, 0)),
            pl.BlockSpec((2 * LANES, LANES), lambda bi, i: (0, 0)),
        ],
        out_specs=pl.BlockSpec((1, tq, N_HEADS * LANES), lambda bi, i: (bi, i, 0)),
        compiler_params=_cparams(("parallel", "parallel")),
    )(p3, g128, cos, sins, ones2)


def prep_kv(p3, col_k, col_v, g128, cos, sins, norm, rope, tq):
    b, l, _ = p3.shape
    tq = min(tq, l)
    ones2 = jnp.asarray(_GROUP_ONES2, BF16)
    return pl.pallas_call(
        functools.partial(_prep_kv_kernel, norm=norm, rope=rope),
        out_shape=(jax.ShapeDtypeStruct((b, KV_W, l), BF16), jax.ShapeDtypeStruct((b, l, KV_W), BF16)),
        grid=(b, l // tq),
        in_specs=[
            pl.BlockSpec((1, tq, KV_W), lambda bi, i: (bi, i, col_k // KV_W)),
            pl.BlockSpec((1, tq, KV_W), lambda bi, i: (bi, i, col_v // KV_W)),
            pl.BlockSpec((1, LANES), lambda bi, i: (0, 0)),
            pl.BlockSpec((tq, LANES), lambda bi, i: (i, 0)),
            pl.BlockSpec((tq, LANES), lambda bi, i: (i, 0)),
            pl.BlockSpec((2 * LANES, LANES), lambda bi, i: (0, 0)),
        ],
        out_specs=(pl.BlockSpec((1, KV_W, tq), lambda bi, i: (bi, 0, i)),
                   pl.BlockSpec((1, tq, KV_W), lambda bi, i: (bi, i, 0))),
        compiler_params=_cparams(("parallel", "parallel")),
    )(p3, p3, g128, cos, sins, ones2)


LOG2E = 1.4426950408889634
SHIFT_HEADROOM = 57.0
SAFE_BOUND = 90.0


def _group_ssq(xb, ones2):
    sq = xb * xb
    hi = sq.astype(BF16)
    lo = (sq - hi.astype(F32)).astype(BF16)
    return jnp.dot(jnp.concatenate([hi, lo], axis=1), ones2, preferred_element_type=F32)


def _prep_kvb_kernel(k_ref, v_ref, g_ref, cos_ref, sin_ref, ones_ref, kt_ref, vo_ref, kn_ref, *, rope):
    k = _prep_tile(k_ref[0], g_ref[...], cos_ref[...], sin_ref[...], ones_ref[...], True, rope)
    kb = k.astype(BF16).astype(F32)
    ssq = _group_ssq(kb, ones_ref[...])
    kn_ref[0, 0] = jnp.broadcast_to(jnp.max(ssq, axis=0, keepdims=True), (8, LANES))
    kt = kb.T
    row = lax.broadcasted_iota(jnp.int32, kt.shape, 0)
    v = v_ref[0]
    low = lax.broadcasted_iota(jnp.int32, v.shape, 1) < HEAD_DIM
    for g in range(N_KV):
        ktg = kt if g == 0 else jnp.concatenate([kt[HEAD_DIM:], kt[:HEAD_DIM]], axis=0)
        kt_ref[0, g] = jnp.where(row < HEAD_DIM, ktg, jnp.where(row == HEAD_DIM, 1.0, 0.0)).astype(BF16)
        vg = v if g == 0 else pltpu.roll(v, HEAD_DIM, 1)
        vo_ref[0, g] = jnp.where(low, vg, 1.0).astype(BF16)


def _prep_qb_kernel(kmax_ref, x_ref, g_ref, cos_ref, sin_ref, ones_ref, o_ref, u_ref, *, rope):
    b = pl.program_id(0)
    lane = lax.broadcasted_iota(jnp.int32, (x_ref.shape[1], LANES), 1)
    umax = None
    for tile in range(ATT_W // LANES):
        kv = (2 * tile) // GROUP
        x = x_ref[0, :, tile * LANES:(tile + 1) * LANES]
        y = _prep_tile(x, g_ref[...], cos_ref[...], sin_ref[...], ones_ref[...], True, rope) * (ATTN_SCALE * LOG2E)
        yb = y.astype(BF16).astype(F32)
        u2 = jnp.sqrt(_group_ssq(yb, ones_ref[...])) * kmax_ref[b * N_KV + kv]
        tmax = jnp.max(u2, axis=0, keepdims=True)
        umax = tmax if umax is None else jnp.maximum(umax, tmax)
        neg_shift = SHIFT_HEADROOM - u2
        y_sw = pltpu.roll(yb, HEAD_DIM, 1)
        ns_sw = pltpu.roll(neg_shift, HEAD_DIM, 1)
        for half in range(2):
            h = 2 * tile + half
            data = yb if half == 0 else y_sw
            ns = ns_sw if half == 0 else neg_shift
            o_ref[0, :, h * LANES:(h + 1) * LANES] = jnp.where(
                lane < HEAD_DIM, data, jnp.where(lane == HEAD_DIM, ns, 0.0)).astype(BF16)
    u_ref[0, 0] = jnp.broadcast_to(umax, (8, LANES))


def prep_kvb(p3, col_k, col_v, g128, cos, sins, rope, tq):
    b, l, _ = p3.shape
    tq = min(tq, l)
    ones2 = jnp.asarray(_GROUP_ONES2, BF16)
    return pl.pallas_call(
        functools.partial(_prep_kvb_kernel, rope=rope),
        out_shape=(jax.ShapeDtypeStruct((b, N_KV, LANES, l), BF16),
                   jax.ShapeDtypeStruct((b, N_KV, l, LANES), BF16),
                   jax.ShapeDtypeStruct((b, l // tq, 8, LANES), F32)),
        grid=(b, l // tq),
        in_specs=[
            pl.BlockSpec((1, tq, KV_W), lambda bi, i: (bi, i, col_k // KV_W)),
            pl.BlockSpec((1, tq, KV_W), lambda bi, i: (bi, i, col_v // KV_W)),
            pl.BlockSpec((1, LANES), lambda bi, i: (0, 0)),
            pl.BlockSpec((tq, LANES), lambda bi, i: (i, 0)),
            pl.BlockSpec((tq, LANES), lambda bi, i: (i, 0)),
            pl.BlockSpec((2 * LANES, LANES), lambda bi, i: (0, 0)),
        ],
        out_specs=(pl.BlockSpec((1, N_KV, LANES, tq), lambda bi, i: (bi, 0, 0, i)),
                   pl.BlockSpec((1, N_KV, tq, LANES), lambda bi, i: (bi, 0, i, 0)),
                   pl.BlockSpec((1, 1, 8, LANES), lambda bi, i: (bi, i, 0, 0))),
        compiler_params=_cparams(("parallel", "parallel")),
    )(p3, p3, g128, cos, sins, ones2)


def prep_qb(p3, col, g128, cos, sins, kmax, rope, tq):
    b, l, _ = p3.shape
    tq = min(tq, l)
    ones2 = jnp.asarray(_GROUP_ONES2, BF16)
    return pl.pallas_call(
        functools.partial(_prep_qb_kernel, rope=rope),
        out_shape=(jax.ShapeDtypeStruct((b, l, N_HEADS * LANES), BF16),
                   jax.ShapeDtypeStruct((b, l // tq, 8, LANES), F32)),
        grid_spec=pltpu.PrefetchScalarGridSpec(
            num_scalar_prefetch=1,
            grid=(b, l // tq),
            in_specs=[
                pl.BlockSpec((1, tq, ATT_W), lambda bi, i, s: (bi, i, col // ATT_W)),
                pl.BlockSpec((1, LANES), lambda bi, i, s: (0, 0)),
                pl.BlockSpec((tq, LANES), lambda bi, i, s: (i, 0)),
                pl.BlockSpec((tq, LANES), lambda bi, i, s: (i, 0)),
                pl.BlockSpec((2 * LANES, LANES), lambda bi, i, s: (0, 0)),
            ],
            out_specs=(pl.BlockSpec((1, tq, N_HEADS * LANES), lambda bi, i, s: (bi, i, 0)),
                       pl.BlockSpec((1, 1, 8, LANES), lambda bi, i, s: (bi, i, 0, 0))),
        ),
        compiler_params=_cparams(("parallel", "parallel")),
    )(kmax, p3, g128, cos, sins, ones2)


def _group_q(q_ref, g):
    return jnp.concatenate([q_ref[0, :, (g * GROUP + hh) * LANES:(g * GROUP + hh + 1) * LANES]
                            for hh in range(GROUP)], axis=0)


def _dense_attn_kernel(q_ref, kt_ref, v_ref, o_ref, *, tk, online):
    tq = q_ref.shape[1]
    rows = GROUP * tq
    nk = kt_ref.shape[3] // tk
    lane = lax.broadcasted_iota(jnp.int32, (tq, LANES), 1)
    low = lane < HEAD_DIM
    for g in range(N_KV):
        qg = _group_q(q_ref, g)

        def body(j, carry, qg=qg, g=g):
            k0 = pl.multiple_of(j * tk, tk)
            s = jnp.dot(qg, kt_ref[0, g, :, pl.ds(k0, tk)], preferred_element_type=F32)
            vj = v_ref[0, g, pl.ds(k0, tk), :]
            if online:
                m, acc = carry
                m_new = jnp.maximum(m, jnp.max(s, axis=1, keepdims=True))
                p = jnp.exp2(s - m_new).astype(BF16)
                return m_new, jnp.exp2(m - m_new) * acc + jnp.dot(p, vj, preferred_element_type=F32)
            return carry + jnp.dot(jnp.exp2(s).astype(BF16), vj, preferred_element_type=F32)

        acc0 = jnp.zeros((rows, LANES), F32)
        if online:
            _, acc = lax.fori_loop(0, nk, body, (jnp.full((rows, 1), MASK_VALUE, F32), acc0))
        else:
            acc = lax.fori_loop(0, nk, body, acc0)
        o = acc * pltpu.roll(1.0 / acc, HEAD_DIM, 1)
        for pair in range(GROUP // 2):
            a = o[(2 * pair) * tq:(2 * pair + 1) * tq]
            b = o[(2 * pair + 1) * tq:(2 * pair + 2) * tq]
            t0 = (g * GROUP // 2 + pair) * LANES
            o_ref[0, :, t0:t0 + LANES] = jnp.where(low, a, pltpu.roll(b, HEAD_DIM, 1)).astype(o_ref.dtype)


def dense_attention(q, kt, v, tq, tk, online):
    b, l, _ = q.shape
    sk = kt.shape[3]
    tq = min(tq, l)
    tk = min(tk, sk)
    return pl.pallas_call(
        functools.partial(_dense_attn_kernel, tk=tk, online=online),
        out_shape=jax.ShapeDtypeStruct((b, l, ATT_W), BF16),
        grid=(b, l // tq),
        in_specs=[
            pl.BlockSpec((1, tq, N_HEADS * LANES), lambda bi, i: (bi, i, 0)),
            pl.BlockSpec((1, N_KV, LANES, sk), lambda bi, i: (bi, 0, 0, 0)),
            pl.BlockSpec((1, N_KV, sk, LANES), lambda bi, i: (bi, 0, 0, 0)),
        ],
        out_specs=pl.BlockSpec((1, tq, ATT_W), lambda bi, i: (bi, i, 0)),
        compiler_params=_cparams(("parallel", "arbitrary")),
    )(q, kt, v)


def dense_branch(q, kt, v, ubound, tq, tk):
    return lax.cond(ubound <= SAFE_BOUND,
                    lambda: dense_attention(q, kt, v, tq, tk, False),
                    lambda: dense_attention(q, kt, v, tq, tk, True))


def _softmax_step(qg, kt, v, m, l, acc, mask=None):
    s = jnp.dot(qg, kt, preferred_element_type=F32)
    if mask is not None:
        s = jnp.where(mask, s, MASK_VALUE)
    m_new = jnp.maximum(m, jnp.max(s, axis=1, keepdims=True))
    alpha = jnp.exp(m - m_new)
    p = jnp.exp(s - m_new)
    l = alpha * l + jnp.sum(p, axis=1, keepdims=True)
    acc = alpha * acc + jnp.dot(p.astype(BF16), v, preferred_element_type=F32)
    return m_new, l, acc


def _write_heads(o_ref, acc, l, g, tq):
    o = acc * (1.0 / l)
    lane = lax.broadcasted_iota(jnp.int32, (tq, LANES), 1)
    low = lane < HEAD_DIM
    for pair in range(GROUP // 2):
        a = o[(2 * pair) * tq:(2 * pair + 1) * tq]
        b = o[(2 * pair + 1) * tq:(2 * pair + 2) * tq]
        if g == 0:
            tile = jnp.where(low, a, pltpu.roll(b, HEAD_DIM, 1))
        else:
            tile = jnp.where(low, pltpu.roll(a, HEAD_DIM, 1), b)
        t0 = (g * GROUP // 2 + pair) * LANES
        o_ref[0, :, t0:t0 + LANES] = tile.astype(o_ref.dtype)


def _sink_stats(sink_ref, g, tq):
    m = jnp.concatenate([jnp.full((tq, 1), sink_ref[g * GROUP + hh], F32) for hh in range(GROUP)], axis=0)
    return m, jnp.ones((GROUP * tq, 1), F32), jnp.zeros((GROUP * tq, LANES), F32)


def _ctx_attn_kernel(sink_ref, q_ref, kt_ref, v_ref, o_ref):
    tq = q_ref.shape[1]
    for g in range(N_KV):
        m, l, acc = _softmax_step(_group_q(q_ref, g), kt_ref[0], v_ref[0], *_sink_stats(sink_ref, g, tq))
        _write_heads(o_ref, acc, l, g, tq)


def _window_attn_kernel(sink_ref, q_ref, kt_ref, v_ref, o_ref, *, n_ctx, seq):
    tq = q_ref.shape[1]
    band = tq + 2 * WINDOW
    i = pl.program_id(1)
    start = pl.multiple_of(i * tq, tq)
    rows = GROUP * tq
    r = lax.broadcasted_iota(jnp.int32, (rows, band), 0) % tq
    kp = lax.broadcasted_iota(jnp.int32, (rows, band), 1)
    key_pos = start + kp - WINDOW
    mask = (jnp.abs(kp - WINDOW - r) <= WINDOW) & (key_pos >= 0) & (key_pos < seq)
    for g in range(N_KV):
        qg = _group_q(q_ref, g)
        carry = _softmax_step(qg, kt_ref[0, :, 0:n_ctx], v_ref[0, 0:n_ctx, :], *_sink_stats(sink_ref, g, tq))
        b0 = pl.multiple_of(n_ctx + start, LANES)
        m, l, acc = _softmax_step(qg, kt_ref[0, :, pl.ds(b0, band)], v_ref[0, pl.ds(b0, band), :],
                                  *carry, mask=mask)
        _write_heads(o_ref, acc, l, g, tq)


def _sink_attn_call(kern, q, kt, v, sink, tq):
    b, l, _ = q.shape
    sk = kt.shape[2]
    return pl.pallas_call(
        kern,
        out_shape=jax.ShapeDtypeStruct((b, l, ATT_W), BF16),
        grid_spec=pltpu.PrefetchScalarGridSpec(
            num_scalar_prefetch=1,
            grid=(b, l // tq),
            in_specs=[
                pl.BlockSpec((1, tq, N_HEADS * LANES), lambda bi, i, s: (bi, i, 0)),
                pl.BlockSpec((1, KV_W, sk), lambda bi, i, s: (bi, 0, 0)),
                pl.BlockSpec((1, sk, KV_W), lambda bi, i, s: (bi, 0, 0)),
            ],
            out_specs=pl.BlockSpec((1, tq, ATT_W), lambda bi, i, s: (bi, i, 0)),
        ),
        compiler_params=_cparams(("parallel", "arbitrary")),
    )(sink, q, kt, v)


def ctx_sink_attention(q, kt, v, sink):
    return _sink_attn_call(_ctx_attn_kernel, q, kt, v, sink, q.shape[1])


def window_attention(q, kt, v, sink, n_ctx, tq):
    kern = functools.partial(_window_attn_kernel, n_ctx=n_ctx, seq=q.shape[1])
    return _sink_attn_call(kern, q, kt, v, sink, tq)


def _merge_kernel(of_ref, ob_ref, hg_ref, b_ref, c_ref, ga_ref, gb_ref, gc_ref, x_ref,
                  wa_ref, wb_ref, wc_ref, wo_ref, hgn_ref, gate1_ref, n2_ref, sc2_ref, sh2_ref, wr_ref,
                  xo_ref, h2_ref, lg_ref):
    o = of_ref[0] + ob_ref[0]
    tiles = []
    for h in range(HG_HEADS):
        t = o[:, h * LANES:(h + 1) * LANES]
        ms = jnp.mean(t * t, axis=-1, keepdims=True)
        tiles.append(t * lax.rsqrt(ms + EPS) * hgn_ref[...])
    hg = hg_ref[...]
    a = (jnp.concatenate(tiles, axis=1) * (hg * _sigmoid(hg))).astype(BF16)
    merged = (_sigmoid(ga_ref[...]) * jnp.dot(a, wa_ref[...], preferred_element_type=F32)
              + _sigmoid(gb_ref[...]) * jnp.dot(b_ref[...], wb_ref[...], preferred_element_type=F32)
              + _sigmoid(gc_ref[...]) * jnp.dot(c_ref[...], wc_ref[...], preferred_element_type=F32))
    y = jnp.dot(merged.astype(BF16), wo_ref[...], preferred_element_type=F32)
    x = x_ref[...] + gate1_ref[0] * y
    xo_ref[...] = x
    ms = jnp.mean(x * x, axis=-1, keepdims=True)
    h2 = (x * lax.rsqrt(ms + EPS) * n2_ref[...]) * (1.0 + sc2_ref[0]) + sh2_ref[0]
    h2_ref[...] = h2.astype(BF16)
    lg_ref[...] = jnp.dot(h2, wr_ref[...], preferred_element_type=F32, precision=lax.Precision.HIGHEST)


def merge(o_fb, p2d, b2d, c2d, x2d, rows_per_batch, wa, wb, wc, wo, hgn, gate1, n2, sc2, sh2, w_route, tm):
    t, d = x2d.shape
    tm = min(tm, rows_per_batch)
    per = rows_per_batch // tm
    row = lambda i: (i, 0)
    const = lambda i: (0, 0)
    bat = lambda i: (i // per, 0, 0)
    gcol = COL["gates"] // d
    return pl.pallas_call(
        _merge_kernel,
        out_shape=(jax.ShapeDtypeStruct((t, d), F32), jax.ShapeDtypeStruct((t, d), BF16),
                   jax.ShapeDtypeStruct((t, LANES), F32)),
        grid=(t // tm,),
        in_specs=[
            pl.BlockSpec((1, tm, HG_W), lambda i: (0, i, 0)),
            pl.BlockSpec((1, tm, HG_W), lambda i: (1, i, 0)),
            pl.BlockSpec((tm, HG_W), lambda i: (i, COL["hg"] // HG_W)),
            pl.BlockSpec((tm, ATT_W), row),
            pl.BlockSpec((tm, ATT_W), row),
            pl.BlockSpec((tm, d), lambda i: (i, gcol)),
            pl.BlockSpec((tm, d), lambda i: (i, gcol + 1)),
            pl.BlockSpec((tm, d), lambda i: (i, gcol + 2)),
            pl.BlockSpec((tm, d), row),
            pl.BlockSpec((HG_W, d), const),
            pl.BlockSpec((ATT_W, d), const),
            pl.BlockSpec((ATT_W, d), const),
            pl.BlockSpec((d, d), const),
            pl.BlockSpec((1, LANES), const),
            pl.BlockSpec((1, 1, d), bat),
            pl.BlockSpec((1, d), const),
            pl.BlockSpec((1, 1, d), bat),
            pl.BlockSpec((1, 1, d), bat),
            pl.BlockSpec((d, LANES), const),
        ],
        out_specs=(pl.BlockSpec((tm, d), row), pl.BlockSpec((tm, d), row), pl.BlockSpec((tm, LANES), row)),
        compiler_params=_cparams(("parallel",)),
    )(o_fb, o_fb, p2d, b2d, c2d, p2d, p2d, p2d, x2d, wa, wb, wc, wo, hgn.reshape(1, LANES),
      gate1, n2.reshape(1, d), sc2, sh2, w_route)


def _expert_kernel(be_ref, x_ref, wg_ref, wu_ref, wd_ref, o_ref):
    x = x_ref[...]
    gte = jnp.dot(x, wg_ref[0], preferred_element_type=F32)
    up = jnp.dot(x, wu_ref[0], preferred_element_type=F32)
    hid = (gte * _sigmoid(gte) * up).astype(BF16)
    o_ref[...] = jnp.dot(hid, wd_ref[0], preferred_element_type=F32)


def expert_ffn(rows, block_expert, wg, wu, wd):
    n, d = rows.shape
    de = wg.shape[2]
    return pl.pallas_call(
        _expert_kernel,
        out_shape=jax.ShapeDtypeStruct((n, d), F32),
        grid_spec=pltpu.PrefetchScalarGridSpec(
            num_scalar_prefetch=1,
            grid=(n // MOE_ROWS,),
            in_specs=[
                pl.BlockSpec((MOE_ROWS, d), lambda i, be: (i, 0)),
                pl.BlockSpec((1, d, de), lambda i, be: (be[i], 0, 0)),
                pl.BlockSpec((1, d, de), lambda i, be: (be[i], 0, 0)),
                pl.BlockSpec((1, de, d), lambda i, be: (be[i], 0, 0)),
            ],
            out_specs=pl.BlockSpec((MOE_ROWS, d), lambda i, be: (i, 0)),
        ),
        compiler_params=_cparams(("arbitrary",)),
    )(block_expert, rows, wg, wu, wd)


ROUTE_ROWS = 512
_STRICT_LOWER = np.tril(np.ones((ROUTE_ROWS, ROUTE_ROWS), np.float32), -1)
_NEG_BIG = -3.0e38


def _lane_argmax(x, lane):
    top = jnp.max(x, axis=1, keepdims=True)
    idx = jnp.min(jnp.where(x == top, lane, LANES), axis=1, keepdims=True)
    return top, idx


def _route_kernel(lg_ref, tri_ref, o_ref, cnt_ref, run_scr):
    @pl.when(pl.program_id(0) == 0)
    def _():
        run_scr[...] = jnp.zeros_like(run_scr)

    lg = lg_ref[...]
    lane = lax.broadcasted_iota(jnp.int32, lg.shape, 1)
    is_grp = lane < N_GROUPS
    gtop, gidx = _lane_argmax(jnp.where(is_grp, lg, _NEG_BIG), lane)
    grp_w = 1.0 / jnp.sum(jnp.where(is_grp, jnp.exp(lg - gtop), 0.0), axis=1, keepdims=True)
    lo = N_GROUPS + EXP_PER_GROUP * gidx
    x1 = jnp.where((lane >= lo) & (lane < lo + EXP_PER_GROUP), lg, _NEG_BIG)
    t1, i1 = _lane_argmax(x1, lane)
    t2, i2 = _lane_argmax(jnp.where(lane == i1, _NEG_BIG, x1), lane)
    r = jnp.exp(t2 - t1)
    w1 = grp_w / (1.0 + r)
    w2 = w1 * r
    e1 = i1 - N_GROUPS
    e2 = i2 - N_GROUPS
    hot1 = lane == e1
    hot2 = lane == e2
    hot = jnp.where(hot1 | hot2, 1.0, 0.0)
    before = run_scr[...] + jnp.dot(tri_ref[...], hot.astype(BF16), preferred_element_type=F32)
    rank1 = jnp.sum(jnp.where(hot1, before, 0.0), axis=1, keepdims=True)
    rank2 = jnp.sum(jnp.where(hot2, before, 0.0), axis=1, keepdims=True)
    run_scr[...] = run_scr[...] + jnp.sum(hot, axis=0, keepdims=True)
    cnt_ref[...] = jnp.broadcast_to(run_scr[...], cnt_ref.shape)
    out = jnp.where(lane == 0, e1.astype(F32), jnp.where(lane == 1, e2.astype(F32), 0.0))
    out = jnp.where(lane == 2, w1, jnp.where(lane == 3, w2, out))
    o_ref[...] = jnp.where(lane == 4, rank1, jnp.where(lane == 5, rank2, out))


def route(logits):
    t = logits.shape[0]
    return pl.pallas_call(
        _route_kernel,
        out_shape=(jax.ShapeDtypeStruct((t, LANES), F32), jax.ShapeDtypeStruct((8, LANES), F32)),
        grid=(t // ROUTE_ROWS,),
        in_specs=[pl.BlockSpec((ROUTE_ROWS, LANES), lambda i: (i, 0)),
                  pl.BlockSpec((ROUTE_ROWS, ROUTE_ROWS), lambda i: (0, 0))],
        out_specs=(pl.BlockSpec((ROUTE_ROWS, LANES), lambda i: (i, 0)),
                   pl.BlockSpec((8, LANES), lambda i: (0, 0))),
        scratch_shapes=[pltpu.VMEM((1, LANES), F32)],
        compiler_params=_cparams(("arbitrary",)),
    )(logits, jnp.asarray(_STRICT_LOWER, BF16))


def _final_norm_kernel(x_ref, r_ref, gate_ref, g_ref, o_ref):
    x = x_ref[...] + gate_ref[0] * r_ref[...]
    ms = jnp.mean(x * x, axis=-1, keepdims=True)
    o_ref[...] = x * lax.rsqrt(ms + EPS) * g_ref[...]


def _residual_kernel(x_ref, r_ref, gate_ref, o_ref):
    o_ref[...] = x_ref[...] + gate_ref[0] * r_ref[...]


def residual(x2d, r2d, gate, rows_per_batch, final_g=None, tm=512):
    t, d = x2d.shape
    tm = min(tm, rows_per_batch)
    per = rows_per_batch // tm
    row = lambda i: (i, 0)
    specs = [pl.BlockSpec((tm, d), row), pl.BlockSpec((tm, d), row),
             pl.BlockSpec((1, 1, d), lambda i: (i // per, 0, 0))]
    args = [x2d, r2d, gate]
    kern = _residual_kernel
    if final_g is not None:
        specs.append(pl.BlockSpec((1, d), lambda i: (0, 0)))
        args.append(final_g.reshape(1, d))
        kern = _final_norm_kernel
    return pl.pallas_call(
        kern, out_shape=jax.ShapeDtypeStruct((t, d), F32), grid=(t // tm,),
        in_specs=specs, out_specs=pl.BlockSpec((tm, d), row),
        compiler_params=_cparams(("parallel",)),
    )(*args)


def _dispatch(info, counts_row):
    n_tok = info.shape[0]
    expert = info[:, 0:TOP_K].astype(jnp.int32)
    weight = info[:, 2:2 + TOP_K]
    rank = info[:, 4:4 + TOP_K].astype(jnp.int32)
    counts = counts_row[0, :N_EXPERTS].astype(jnp.int32)
    padded = (counts + MOE_ROWS - 1) // MOE_ROWS * MOE_ROWS
    pad_end = jnp.cumsum(padded)
    pad_start = pad_end - padded
    dest = jnp.take(pad_start, expert, axis=0) + rank
    n_blocks = -(-(n_tok * TOP_K + N_EXPERTS * (MOE_ROWS - 1)) // MOE_ROWS)
    token = jnp.broadcast_to(jnp.arange(n_tok, dtype=jnp.int32)[:, None], dest.shape)
    src_tok = jnp.zeros((n_blocks * MOE_ROWS,), jnp.int32).at[dest.reshape(-1)].set(token.reshape(-1))
    block_expert = jnp.minimum(
        jnp.searchsorted(pad_end, jnp.arange(n_blocks) * MOE_ROWS, side="right"), N_EXPERTS - 1).astype(jnp.int32)
    return src_tok, dest, weight, block_expert


def _rope_tables(seq):
    n_rows = seq // GRID_W
    row = jnp.repeat(jnp.arange(n_rows), GRID_W).astype(F32)
    col = jnp.tile(jnp.arange(GRID_W), n_rows).astype(F32)
    axis_pairs = HEAD_DIM // 4
    inv = ROPE_THETA ** (-jnp.arange(axis_pairs, dtype=F32) / axis_pairs)
    ang = jnp.concatenate([row[:, None] * inv, col[:, None] * inv], axis=-1)
    cos, sin = jnp.cos(ang), jnp.sin(ang)
    cos128 = jnp.concatenate([cos, cos, cos, cos], axis=-1)
    sins128 = jnp.concatenate([-sin, sin, -sin, sin], axis=-1)
    return cos128, sins128


def kernel(x, c, ctx, c_ctx, w_mod, b_mod, norm1_g, norm2_g, w_in, hgrn_lb_logits, hgrn_out_norm_g,
           attn_q_norm_g, attn_k_norm_g, swa_sink, w_branch_a, w_branch_b, w_branch_c, w_out,
           w_group, w_router, w_exp_gate, w_exp_up, w_exp_down, final_norm_g):
    bsz, seq, d = x.shape
    n_ctx = ctx.shape[1]
    depth = w_mod.shape[0]
    cos, sins = _rope_tables(seq)
    lb_p = jax.nn.softmax(hgrn_lb_logits.astype(F32), axis=0)
    lower_bounds = jnp.cumsum(lb_p, axis=0) - lb_p[0]
    zeros_sink = jnp.zeros((N_HEADS,), F32)
    ones128 = jnp.ones((1, LANES), F32)

    xl = x.reshape(bsz * seq, d)
    xc = ctx.reshape(bsz * n_ctx, d)
    for layer in range(depth):
        ctx_out = layer < depth - 1
        mod_l = jax.nn.silu(c) @ w_mod[layer] + b_mod[layer]
        mod_c = jnp.broadcast_to(jax.nn.silu(c_ctx) @ w_mod[layer] + b_mod[layer], (bsz, 6 * d))
        ml = [m.reshape(bsz, 1, d) for m in jnp.split(mod_l, 6, axis=-1)]
        mc = [m.reshape(bsz, 1, d) for m in jnp.split(mod_c, 6, axis=-1)]

        w_in_b = w_in[layer][:, _COL_PERM].astype(BF16)
        p_l = inproj(xl, seq, norm1_g[layer], ml[1], ml[0], w_in_b, 1024, 1024)
        p_c = inproj(xc, n_ctx, norm1_g[layer], mc[1], mc[0], w_in_b, 256, 1024)
        p_l3 = p_l.reshape(bsz, seq, D_IN)
        p_c3 = p_c.reshape(bsz, n_ctx, D_IN)

        s0 = jnp.zeros((bsz, 2, HG_HEADS, HG_KDIM, HG_KDIM), F32)
        o_c, s_c = hgrn_scan(p_c3, lower_bounds[layer], s0, 256)
        o_l, _ = hgrn_scan(p_l3, lower_bounds[layer], s_c, 512)

        gq = jnp.tile(attn_q_norm_g[layer], 2).reshape(1, LANES)
        gk = jnp.tile(attn_k_norm_g[layer], 2).reshape(1, LANES)
        kt_l, v_l, kn_l = prep_kvb(p_l3, COL["ak"], COL["av"], gk, cos, sins, True, 512)
        kt_c, v_c, kn_c = prep_kvb(p_c3, COL["ak"], COL["av"], gk, cos, sins, False, 256)
        kn = jnp.maximum(jnp.max(kn_l, axis=(1, 2)), jnp.max(kn_c, axis=(1, 2)))
        kmax = jnp.sqrt(kn[:, ::HEAD_DIM]).reshape(-1)
        q_l, u_l = prep_qb(p_l3, COL["aq"], gq, cos, sins, kmax, True, 512)
        kt_all = jnp.concatenate([kt_c, kt_l], axis=3)
        v_all = jnp.concatenate([v_c, v_l], axis=2)
        b_l = dense_branch(q_l, kt_all, v_all, jnp.max(u_l), 256, 640)

        sq_l = prep_q(p_l3, COL["sq"], ones128, cos, sins, False, True, 512)
        skt_l, sv_l = prep_kv(p_l3, COL["sk"], COL["sv"], ones128, cos, sins, False, True, 512)
        skt_c, sv_c = prep_kv(p_c3, COL["sk"], COL["sv"], ones128, cos, sins, False, False, 256)
        zk = jnp.zeros((bsz, KV_W, WINDOW), BF16)
        zv = jnp.zeros((bsz, WINDOW, KV_W), BF16)
        skt_all = jnp.concatenate([skt_c, zk, skt_l, zk], axis=2)
        sv_all = jnp.concatenate([sv_c, zv, sv_l, zv], axis=1)
        c_l = window_attention(sq_l, skt_all, sv_all, swa_sink[layer], n_ctx, 256)

        wa = w_branch_a[layer].astype(BF16)
        wb = w_branch_b[layer].astype(BF16)
        wc = w_branch_c[layer].astype(BF16)
        wo = w_out[layer].astype(BF16)
        w_route = jnp.concatenate(
            [w_group[layer], w_router[layer], jnp.zeros((d, LANES - N_GROUPS - N_EXPERTS), F32)], axis=1)
        xl, h_l, lg_l = merge(o_l.reshape(2, bsz * seq, HG_W), p_l, b_l.reshape(-1, ATT_W),
                              c_l.reshape(-1, ATT_W), xl, seq, wa, wb, wc, wo, hgrn_out_norm_g[layer],
                              ml[2], norm2_g[layer], ml[4], ml[3], w_route, 256)
        if ctx_out:
            q_c, u_c = prep_qb(p_c3, COL["aq"], gq, cos, sins, kmax, False, 256)
            b_c = dense_branch(q_c, kt_c, v_c, jnp.max(u_c), 256, 256)
            sq_c = prep_q(p_c3, COL["sq"], ones128, cos, sins, False, False, 256)
            c_c = ctx_sink_attention(sq_c, skt_c, sv_c, swa_sink[layer])
            xc, h_c, lg_c = merge(o_c.reshape(2, bsz * n_ctx, HG_W), p_c, b_c.reshape(-1, ATT_W),
                                  c_c.reshape(-1, ATT_W), xc, n_ctx, wa, wb, wc, wo, hgrn_out_norm_g[layer],
                                  mc[2], norm2_g[layer], mc[4], mc[3], w_route, 256)
            h_all = jnp.concatenate([h_c, h_l], axis=0)
            lg_all = jnp.concatenate([lg_c, lg_l], axis=0)
        else:
            h_all, lg_all = h_l, lg_l

        src_tok, dest, weight, block_expert = _dispatch(*route(lg_all))
        rows = jnp.take(h_all, src_tok, axis=0)
        out_rows = expert_ffn(rows, block_expert, w_exp_gate[layer].astype(BF16),
                              w_exp_up[layer].astype(BF16), w_exp_down[layer].astype(BF16))
        y = (weight[:, 0:1] * jnp.take(out_rows, dest[:, 0], axis=0)
             + weight[:, 1:2] * jnp.take(out_rows, dest[:, 1], axis=0))
        last = layer == depth - 1
        if ctx_out:
            n_c = bsz * n_ctx
            xc = residual(xc, y[:n_c], mc[5], n_ctx)
            y = y[n_c:]
        xl = residual(xl, y, ml[5], seq, final_g=final_norm_g if last else None)
    return xl.reshape(bsz, seq, d)
```

```python
import functools
import math

import numpy as np
import jax
import jax.numpy as jnp
from jax import lax
from jax.experimental import pallas as pl
from jax.experimental.pallas import tpu as pltpu

F32 = jnp.float32
BF16 = jnp.bfloat16

EPS = 1e-6
MASK_VALUE = -1e30
TINY = 1e-30
GRID_W = 64
ROPE_THETA = 10000.0

HG_HEADS = 4
HG_KDIM = 128
HG_W = HG_HEADS * HG_KDIM
HEAD_DIM = 64
N_HEADS = 8
N_KV = 2
GROUP = N_HEADS // N_KV
ATT_W = N_HEADS * HEAD_DIM
KV_W = N_KV * HEAD_DIM
WINDOW = 128
ATTN_SCALE = HEAD_DIM ** -0.5
N_GROUPS = 4
EXP_PER_GROUP = 8
N_EXPERTS = N_GROUPS * EXP_PER_GROUP
TOP_K = 2

LANES = 128
HG_CHUNK = 128
HG_LEVELS = int(math.log2(HG_CHUNK))
MOE_ROWS = 256
ROUTE_ROWS = 512
VMEM_LIMIT = 56 * 1024 * 1024

LOG2E = 1.4426950408889634
SHIFT_HEADROOM = 57.0
SAFE_BOUND = 90.0


def _cparams(sem):
    return pltpu.CompilerParams(dimension_semantics=sem, vmem_limit_bytes=VMEM_LIMIT)


_REF_SIZES = (HG_W, HG_W, HG_W, HG_W, HG_W, ATT_W, KV_W, KV_W, ATT_W, KV_W, KV_W, 3 * 1024)
_REF_NAMES = ("hq", "ff", "fb", "hi", "hg", "aq", "ak", "av", "sq", "sk", "sv", "gates")
_NEW_ORDER = ("hq", "ff", "fb", "hi", "hg", "aq", "sq", "ak", "av", "sk", "sv", "gates")


def _column_permutation():
    starts, acc = {}, 0
    for n, s in zip(_REF_NAMES, _REF_SIZES):
        starts[n] = (acc, s)
        acc += s
    perm, new_starts, pos = [], {}, 0
    for n in _NEW_ORDER:
        a, s = starts[n]
        perm.append(np.arange(a, a + s))
        new_starts[n] = pos
        pos += s
    return np.concatenate(perm), new_starts


_COL_PERM, COL = _column_permutation()
D_IN = int(_COL_PERM.shape[0])


def _inproj_kernel(x_ref, g_ref, sc_ref, sh_ref, w_ref, o_ref, h_scr):
    @pl.when(pl.program_id(1) == 0)
    def _():
        x = x_ref[...]
        ms = jnp.mean(x * x, axis=-1, keepdims=True)
        y = x * lax.rsqrt(ms + EPS) * g_ref[...]
        h_scr[...] = (y * (1.0 + sc_ref[0]) + sh_ref[0]).astype(BF16)

    o_ref[...] = jnp.dot(h_scr[...], w_ref[...], preferred_element_type=F32).astype(o_ref.dtype)


def inproj(x2d, rows_per_batch, g, scale, shift, w_bf16, tm, tn):
    t, d = x2d.shape
    n = w_bf16.shape[1]
    tm = min(tm, rows_per_batch)
    per = rows_per_batch // tm
    return pl.pallas_call(
        _inproj_kernel,
        out_shape=jax.ShapeDtypeStruct((t, n), BF16),
        grid=(t // tm, n // tn),
        in_specs=[
            pl.BlockSpec((tm, d), lambda i, j: (i, 0)),
            pl.BlockSpec((1, d), lambda i, j: (0, 0)),
            pl.BlockSpec((1, 1, d), lambda i, j: (i // per, 0, 0)),
            pl.BlockSpec((1, 1, d), lambda i, j: (i // per, 0, 0)),
            pl.BlockSpec((d, tn), lambda i, j: (0, j)),
        ],
        out_specs=pl.BlockSpec((tm, tn), lambda i, j: (i, j)),
        scratch_shapes=[pltpu.VMEM((tm, d), BF16)],
        compiler_params=_cparams(("parallel", "arbitrary")),
    )(x2d, g.reshape(1, d), scale, shift, w_bf16)


def _hgrn_exponent_matrices():
    c, nl = HG_CHUNK, HG_LEVELS
    rows = (nl + 2) * c + 8
    out = np.zeros((2, rows, c), np.float32)
    for d in range(2):
        pos = np.arange(c) if d == 0 else c - 1 - np.arange(c)
        for l in range(nl):
            m = 1 << l
            for t in range(c):
                p = pos[t]
                mid = (p // (2 * m)) * 2 * m + m
                if p >= mid:
                    sel = (pos >= mid) & (pos <= p)
                else:
                    sel = (pos > p) & (pos <= mid - 1)
                out[d, l * c + t, sel] = 1.0
        for t in range(c):
            out[d, nl * c + t, pos <= pos[t]] = 1.0
            out[d, (nl + 1) * c + t, pos > pos[t]] = 1.0
        out[d, (nl + 2) * c:, :] = 1.0
    return np.concatenate([out, out], axis=2)


_HG_EXP_MATS = _hgrn_exponent_matrices()


def _sigmoid(x):
    return 1.0 / (1.0 + jnp.exp(-x))


def _hgrn_kernel(q_ref, f_ref, i_ref, lb_ref, mat_ref, s0_ref, o_ref, st_ref, *, n_chunks):
    c, nl = HG_CHUNK, HG_LEVELS
    d = pl.program_id(1)

    @pl.when(pl.program_id(2) == 0)
    def _():
        st_ref[...] = s0_ref[...]

    row = lax.broadcasted_iota(jnp.int32, (c, LANES), 0)
    pos = row + d * (c - 1 - 2 * row)
    xor_rc = lax.broadcasted_iota(jnp.int32, (c, c), 0) ^ lax.broadcasted_iota(jnp.int32, (c, c), 1)
    mat = mat_ref[0]

    def chunk(ci, carry):
        cc = ci + d * (n_chunks - 1 - 2 * ci)
        r0 = pl.multiple_of(cc * c, c)
        for h in range(HG_HEADS):
            lanes = slice(h * LANES, (h + 1) * LANES)
            qraw = q_ref[0, pl.ds(r0, c), lanes].astype(F32)
            fz = f_ref[0, pl.ds(r0, c), lanes].astype(F32)
            vb = i_ref[0, pl.ds(r0, c), lanes]
            v = vb.astype(F32)
            lbh = lb_ref[:, lanes]
            sig = _sigmoid(fz)
            logf = jnp.log(jnp.maximum(lbh + (1.0 - lbh) * sig, TINY))
            key = (1.0 - lbh) * (1.0 - sig)
            qh = qraw * _sigmoid(qraw) * (HG_KDIM ** -0.5)
            hi = logf.astype(BF16)
            lo = (logf - hi.astype(F32)).astype(BF16)
            expo = jnp.dot(mat, jnp.concatenate([hi, lo], axis=0), preferred_element_type=F32)
            dec = jnp.exp(expo)
            a = None
            for l in range(nl):
                dl = dec[l * c:(l + 1) * c]
                is_q = ((pos >> l) & 1) == 1
                ql = jnp.where(is_q, qh * dl, 0.0).astype(BF16)
                kl = jnp.where(is_q, 0.0, key * dl).astype(BF16)
                pl_ = lax.dot_general(ql, kl, (((1,), (1,)), ((), ())), preferred_element_type=F32)
                a = pl_ if a is None else pl_ + jnp.where(xor_rc < (1 << l), a, 0.0)
            st = st_ref[0, 0, h]
            q_in = (qh * dec[nl * c:(nl + 1) * c]).astype(BF16)
            k_out = (key * dec[(nl + 1) * c:(nl + 2) * c]).astype(BF16)
            o = (jnp.dot(a.astype(BF16), vb, preferred_element_type=F32)
                 + jnp.sum(qh * key, axis=1, keepdims=True) * v
                 + lax.dot_general(q_in, st.astype(BF16), (((1,), (1,)), ((), ())),
                                   preferred_element_type=F32))
            o_ref[0, 0, pl.ds(r0, c), lanes] = o.astype(o_ref.dtype)
            total = dec[(nl + 2) * c:(nl + 2) * c + 1]
            st_ref[0, 0, h] = total * st + lax.dot_general(
                vb, k_out, (((0,), (0,)), ((), ())), preferred_element_type=F32)
        return carry

    lax.fori_loop(0, n_chunks, chunk, 0)


def hgrn_scan(p3, lb, s0, tb):
    b, l, _ = p3.shape
    tb = min(tb, l)
    nb = l // tb
    blk = lambda d, n: n + d * (nb - 1 - 2 * n)
    wblk = HG_W
    kern = functools.partial(_hgrn_kernel, n_chunks=tb // HG_CHUNK)
    mats = jnp.asarray(_HG_EXP_MATS, BF16)
    return pl.pallas_call(
        kern,
        out_shape=(jax.ShapeDtypeStruct((2, b, l, HG_W), BF16),
                   jax.ShapeDtypeStruct((b, 2, HG_HEADS, HG_KDIM, HG_KDIM), F32)),
        grid=(b, 2, nb),
        in_specs=[
            pl.BlockSpec((1, tb, wblk), lambda bi, d, n: (bi, blk(d, n), COL["hq"] // wblk)),
            pl.BlockSpec((1, tb, wblk), lambda bi, d, n: (bi, blk(d, n), COL["ff"] // wblk + d)),
            pl.BlockSpec((1, tb, wblk), lambda bi, d, n: (bi, blk(d, n), COL["hi"] // wblk)),
            pl.BlockSpec((1, HG_W), lambda bi, d, n: (0, 0)),
            pl.BlockSpec((1,) + _HG_EXP_MATS.shape[1:], lambda bi, d, n: (d, 0, 0)),
            pl.BlockSpec((1, 1, HG_HEADS, HG_KDIM, HG_KDIM), lambda bi, d, n: (bi, d, 0, 0, 0)),
        ],
        out_specs=(
            pl.BlockSpec((1, 1, tb, HG_W), lambda bi, d, n: (d, bi, blk(d, n), 0)),
            pl.BlockSpec((1, 1, HG_HEADS, HG_KDIM, HG_KDIM), lambda bi, d, n: (bi, d, 0, 0, 0)),
        ),
        compiler_params=_cparams(("parallel", "arbitrary", "arbitrary")),
    )(p3, p3, p3, lb.reshape(1, HG_W), mats, s0)


_GROUP_ONES = np.kron(np.eye(LANES // HEAD_DIM, dtype=np.float32), np.ones((HEAD_DIM, HEAD_DIM), np.float32))
_GROUP_ONES2 = np.concatenate([_GROUP_ONES, _GROUP_ONES], axis=0)


def _group_ssq(x, ones2):
    sq = x * x
    hi = sq.astype(BF16)
    lo = (sq - hi.astype(F32)).astype(BF16)
    return jnp.dot(jnp.concatenate([hi, lo], axis=1), ones2, preferred_element_type=F32)


def _prep_tile(x, g, cos, sins, ones2, norm, rope):
    if norm:
        x = x * lax.rsqrt(_group_ssq(x, ones2) * (1.0 / HEAD_DIM) + EPS) * g
    if rope:
        lane = lax.broadcasted_iota(jnp.int32, x.shape, 1)
        first = (lane % HEAD_DIM) < (HEAD_DIM // 2)
        other = jnp.where(first, pltpu.roll(x, LANES - HEAD_DIM // 2, 1), pltpu.roll(x, HEAD_DIM // 2, 1))
        x = x * cos + other * sins
    return x


def _prep_kv_kernel(k_ref, v_ref, g_ref, cos_ref, sin_ref, ones_ref, kt_ref, vo_ref, kn_ref, *, norm, rope):
    k = _prep_tile(k_ref[0].astype(F32), g_ref[...], cos_ref[...], sin_ref[...], ones_ref[...], norm, rope)
    kb = k.astype(BF16).astype(F32)
    ssq = _group_ssq(kb, ones_ref[...])
    kn_ref[0, 0] = jnp.broadcast_to(jnp.max(ssq, axis=0, keepdims=True), (8, LANES))
    kt = kb.T
    row = lax.broadcasted_iota(jnp.int32, kt.shape, 0)
    v = v_ref[0]
    low = lax.broadcasted_iota(jnp.int32, v.shape, 1) < HEAD_DIM
    one = jnp.ones((), v.dtype)
    for g in range(N_KV):
        ktg = kt if g == 0 else jnp.concatenate([kt[HEAD_DIM:], kt[:HEAD_DIM]], axis=0)
        kt_ref[0, g] = jnp.where(row < HEAD_DIM, ktg, jnp.where(row == HEAD_DIM, 1.0, 0.0)).astype(BF16)
        vg = v if g == 0 else jnp.concatenate([v[:, HEAD_DIM:], v[:, :HEAD_DIM]], axis=1)
        vo_ref[0, g] = jnp.where(low, vg, one).astype(BF16)


def _prep_q_kernel(kmax_ref, x_ref, g_ref, cos_ref, sin_ref, ones_ref, o_ref, u_ref, *, norm, rope):
    b = pl.program_id(0)
    lane = lax.broadcasted_iota(jnp.int32, (x_ref.shape[1], LANES), 1)
    umax = None
    for tile in range(ATT_W // LANES):
        kv = (2 * tile) // GROUP
        x = x_ref[0, :, tile * LANES:(tile + 1) * LANES].astype(F32)
        y = _prep_tile(x, g_ref[...], cos_ref[...], sin_ref[...], ones_ref[...], norm, rope) * (ATTN_SCALE * LOG2E)
        yb = y.astype(BF16).astype(F32)
        u2 = jnp.sqrt(_group_ssq(yb, ones_ref[...])) * kmax_ref[b * N_KV + kv]
        tmax = jnp.max(u2, axis=0, keepdims=True)
        umax = tmax if umax is None else jnp.maximum(umax, tmax)
        neg_shift = SHIFT_HEADROOM - u2
        y_sw = pltpu.roll(yb, HEAD_DIM, 1)
        ns_sw = pltpu.roll(neg_shift, HEAD_DIM, 1)
        for half in range(2):
            h = 2 * tile + half
            data = yb if half == 0 else y_sw
            ns = ns_sw if half == 0 else neg_shift
            o_ref[0, :, h * LANES:(h + 1) * LANES] = jnp.where(
                lane < HEAD_DIM, data, jnp.where(lane == HEAD_DIM, ns, 0.0)).astype(BF16)
    u_ref[0, 0] = jnp.broadcast_to(umax, (8, LANES))


def prep_kv(p3, col_k, col_v, g128, cos, sins, norm, rope, tq):
    b, l, _ = p3.shape
    tq = min(tq, l)
    ones2 = jnp.asarray(_GROUP_ONES2, BF16)
    return pl.pallas_call(
        functools.partial(_prep_kv_kernel, norm=norm, rope=rope),
        out_shape=(jax.ShapeDtypeStruct((b, N_KV, LANES, l), BF16),
                   jax.ShapeDtypeStruct((b, N_KV, l, LANES), BF16),
                   jax.ShapeDtypeStruct((b, l // tq, 8, LANES), F32)),
        grid=(b, l // tq),
        in_specs=[
            pl.BlockSpec((1, tq, KV_W), lambda bi, i: (bi, i, col_k // KV_W)),
            pl.BlockSpec((1, tq, KV_W), lambda bi, i: (bi, i, col_v // KV_W)),
            pl.BlockSpec((1, LANES), lambda bi, i: (0, 0)),
            pl.BlockSpec((tq, LANES), lambda bi, i: (i, 0)),
            pl.BlockSpec((tq, LANES), lambda bi, i: (i, 0)),
            pl.BlockSpec((2 * LANES, LANES), lambda bi, i: (0, 0)),
        ],
        out_specs=(pl.BlockSpec((1, N_KV, LANES, tq), lambda bi, i: (bi, 0, 0, i)),
                   pl.BlockSpec((1, N_KV, tq, LANES), lambda bi, i: (bi, 0, i, 0)),
                   pl.BlockSpec((1, 1, 8, LANES), lambda bi, i: (bi, i, 0, 0))),
        compiler_params=_cparams(("parallel", "parallel")),
    )(p3, p3, g128, cos, sins, ones2)


def prep_q(p3, col, g128, cos, sins, kmax, norm, rope, tq):
    b, l, _ = p3.shape
    tq = min(tq, l)
    ones2 = jnp.asarray(_GROUP_ONES2, BF16)
    return pl.pallas_call(
        functools.partial(_prep_q_kernel, norm=norm, rope=rope),
        out_shape=(jax.ShapeDtypeStruct((b, l, N_HEADS * LANES), BF16),
                   jax.ShapeDtypeStruct((b, l // tq, 8, LANES), F32)),
        grid_spec=pltpu.PrefetchScalarGridSpec(
            num_scalar_prefetch=1,
            grid=(b, l // tq),
            in_specs=[
                pl.BlockSpec((1, tq, ATT_W), lambda bi, i, s: (bi, i, col // ATT_W)),
                pl.BlockSpec((1, LANES), lambda bi, i, s: (0, 0)),
                pl.BlockSpec((tq, LANES), lambda bi, i, s: (i, 0)),
                pl.BlockSpec((tq, LANES), lambda bi, i, s: (i, 0)),
                pl.BlockSpec((2 * LANES, LANES), lambda bi, i, s: (0, 0)),
            ],
            out_specs=(pl.BlockSpec((1, tq, N_HEADS * LANES), lambda bi, i, s: (bi, i, 0)),
                       pl.BlockSpec((1, 1, 8, LANES), lambda bi, i, s: (bi, i, 0, 0))),
        ),
        compiler_params=_cparams(("parallel", "parallel")),
    )(kmax, p3, g128, cos, sins, ones2)


def _key_norm_max(kn_list):
    kn = functools.reduce(jnp.maximum, [jnp.max(k, axis=(1, 2)) for k in kn_list])
    return jnp.sqrt(kn[:, ::HEAD_DIM]).reshape(-1)


def _attn_kernel(sink_ref, q_ref, kt_ref, v_ref, o_ref, *, tk, online, use_sink, band, seq):
    tq = q_ref.shape[1]
    rows = GROUP * tq
    lane = lax.broadcasted_iota(jnp.int32, (tq, LANES), 1)
    low = lane < HEAD_DIM
    if band is not None:
        width = tq + 2 * WINDOW
        start = pl.multiple_of(pl.program_id(1) * tq, tq)
        r = lax.broadcasted_iota(jnp.int32, (rows, width), 0) % tq
        kp = lax.broadcasted_iota(jnp.int32, (rows, width), 1)
        key_pos = start + kp - WINDOW
        mask = (jnp.abs(kp - WINDOW - r) <= WINDOW) & (key_pos >= 0) & (key_pos < seq)
    for g in range(N_KV):
        tiles = [q_ref[0, :, (g * GROUP + hh) * LANES:(g * GROUP + hh + 1) * LANES] for hh in range(GROUP)]
        qg = jnp.concatenate(tiles, axis=0)

        def step(carry, kt, v, msk=None, qg=qg):
            s = jnp.dot(qg, kt, preferred_element_type=F32)
            if msk is not None:
                s = jnp.where(msk, s, MASK_VALUE)
            if online:
                m, acc = carry
                m_new = jnp.maximum(m, jnp.max(s, axis=1, keepdims=True))
                p = jnp.exp2(s - m_new).astype(BF16)
                return m_new, jnp.exp2(m - m_new) * acc + jnp.dot(p, v, preferred_element_type=F32)
            return carry + jnp.dot(jnp.exp2(s).astype(BF16), v, preferred_element_type=F32)

        acc0 = jnp.zeros((rows, LANES), F32)
        carry = (jnp.full((rows, 1), MASK_VALUE, F32), acc0) if online else acc0
        if band is None:
            def body(j, c, g=g, step=step):
                k0 = pl.multiple_of(j * tk, tk)
                return step(c, kt_ref[0, g, :, pl.ds(k0, tk)], v_ref[0, g, pl.ds(k0, tk), :])
            carry = lax.fori_loop(0, kt_ref.shape[3] // tk, body, carry)
        else:
            carry = step(carry, kt_ref[0, g, :, 0:band], v_ref[0, g, 0:band, :])
            b0 = pl.multiple_of(band + start, LANES)
            carry = step(carry, kt_ref[0, g, :, pl.ds(b0, width)], v_ref[0, g, pl.ds(b0, width), :], mask)
        acc = carry[1] if online else carry
        if use_sink:
            e = jnp.concatenate([t[:, HEAD_DIM:HEAD_DIM + 1].astype(F32) + sink_ref[g * GROUP + hh] * LOG2E
                                 for hh, t in enumerate(tiles)], axis=0)
            if online:
                e = e - carry[0]
            lane_r = lax.broadcasted_iota(jnp.int32, (rows, LANES), 1)
            acc = acc + jnp.where(lane_r >= HEAD_DIM, jnp.exp2(e), 0.0)
        o = acc * pltpu.roll(1.0 / acc, HEAD_DIM, 1)
        for pair in range(GROUP // 2):
            a = o[(2 * pair) * tq:(2 * pair + 1) * tq]
            b = o[(2 * pair + 1) * tq:(2 * pair + 2) * tq]
            t0 = (g * GROUP // 2 + pair) * LANES
            o_ref[0, :, t0:t0 + LANES] = jnp.where(low, a, pltpu.roll(b, HEAD_DIM, 1)).astype(o_ref.dtype)


def _attention(q, kt, v, sink, tq, tk, online, use_sink, band):
    b, l, _ = q.shape
    sk = kt.shape[3]
    tq = min(tq, l)
    tk = min(tk, sk)
    kern = functools.partial(_attn_kernel, tk=tk, online=online, use_sink=use_sink, band=band, seq=l)
    return pl.pallas_call(
        kern,
        out_shape=jax.ShapeDtypeStruct((b, l, ATT_W), BF16),
        grid_spec=pltpu.PrefetchScalarGridSpec(
            num_scalar_prefetch=1,
            grid=(b, l // tq),
            in_specs=[
                pl.BlockSpec((1, tq, N_HEADS * LANES), lambda bi, i, s: (bi, i, 0)),
                pl.BlockSpec((1, N_KV, LANES, sk), lambda bi, i, s: (bi, 0, 0, 0)),
                pl.BlockSpec((1, N_KV, sk, LANES), lambda bi, i, s: (bi, 0, 0, 0)),
            ],
            out_specs=pl.BlockSpec((1, tq, ATT_W), lambda bi, i, s: (bi, i, 0)),
        ),
        compiler_params=_cparams(("parallel", "arbitrary")),
    )(sink, q, kt, v)


def attention(q, kt, v, ubound, sink, tq, tk, use_sink=False, band=None):
    return lax.cond(ubound <= SAFE_BOUND,
                    lambda: _attention(q, kt, v, sink, tq, tk, False, use_sink, band),
                    lambda: _attention(q, kt, v, sink, tq, tk, True, use_sink, band))


def _merge_kernel(of_ref, ob_ref, hg_ref, b_ref, c_ref, ga_ref, gb_ref, gc_ref, x_ref,
                  wa_ref, wb_ref, wc_ref, wo_ref, hgn_ref, gate1_ref, n2_ref, sc2_ref, sh2_ref, wr_ref,
                  xo_ref, h2_ref, lg_ref):
    o = of_ref[0].astype(F32) + ob_ref[0].astype(F32)
    tiles = []
    for h in range(HG_HEADS):
        t = o[:, h * LANES:(h + 1) * LANES]
        ms = jnp.mean(t * t, axis=-1, keepdims=True)
        tiles.append(t * lax.rsqrt(ms + EPS) * hgn_ref[...])
    hg = hg_ref[...].astype(F32)
    a = (jnp.concatenate(tiles, axis=1) * (hg * _sigmoid(hg))).astype(BF16)
    merged = (_sigmoid(ga_ref[...].astype(F32)) * jnp.dot(a, wa_ref[...], preferred_element_type=F32)
              + _sigmoid(gb_ref[...].astype(F32)) * jnp.dot(b_ref[...], wb_ref[...], preferred_element_type=F32)
              + _sigmoid(gc_ref[...].astype(F32)) * jnp.dot(c_ref[...], wc_ref[...], preferred_element_type=F32))
    y = jnp.dot(merged.astype(BF16), wo_ref[...], preferred_element_type=F32)
    x = x_ref[...] + gate1_ref[0] * y
    xo_ref[...] = x
    ms = jnp.mean(x * x, axis=-1, keepdims=True)
    h2 = (x * lax.rsqrt(ms + EPS) * n2_ref[...]) * (1.0 + sc2_ref[0]) + sh2_ref[0]
    h2_ref[...] = h2.astype(BF16)
    lg_ref[...] = jnp.dot(h2, wr_ref[...], preferred_element_type=F32, precision=lax.Precision.HIGHEST)


def merge(o_fb, p2d, b2d, c2d, x2d, rows_per_batch, wa, wb, wc, wo, hgn, gate1, n2, sc2, sh2, w_route, tm):
    t, d = x2d.shape
    tm = min(tm, rows_per_batch)
    per = rows_per_batch // tm
    row = lambda i: (i, 0)
    const = lambda i: (0, 0)
    bat = lambda i: (i // per, 0, 0)
    gcol = COL["gates"] // d
    return pl.pallas_call(
        _merge_kernel,
        out_shape=(jax.ShapeDtypeStruct((t, d), F32), jax.ShapeDtypeStruct((t, d), BF16),
                   jax.ShapeDtypeStruct((t, LANES), F32)),
        grid=(t // tm,),
        in_specs=[
            pl.BlockSpec((1, tm, HG_W), lambda i: (0, i, 0)),
            pl.BlockSpec((1, tm, HG_W), lambda i: (1, i, 0)),
            pl.BlockSpec((tm, HG_W), lambda i: (i, COL["hg"] // HG_W)),
            pl.BlockSpec((tm, ATT_W), row),
            pl.BlockSpec((tm, ATT_W), row),
            pl.BlockSpec((tm, d), lambda i: (i, gcol)),
            pl.BlockSpec((tm, d), lambda i: (i, gcol + 1)),
            pl.BlockSpec((tm, d), lambda i: (i, gcol + 2)),
            pl.BlockSpec((tm, d), row),
            pl.BlockSpec((HG_W, d), const),
            pl.BlockSpec((ATT_W, d), const),
            pl.BlockSpec((ATT_W, d), const),
            pl.BlockSpec((d, d), const),
            pl.BlockSpec((1, LANES), const),
            pl.BlockSpec((1, 1, d), bat),
            pl.BlockSpec((1, d), const),
            pl.BlockSpec((1, 1, d), bat),
            pl.BlockSpec((1, 1, d), bat),
            pl.BlockSpec((d, LANES), const),
        ],
        out_specs=(pl.BlockSpec((tm, d), row), pl.BlockSpec((tm, d), row), pl.BlockSpec((tm, LANES), row)),
        compiler_params=_cparams(("parallel",)),
    )(o_fb, o_fb, p2d, b2d, c2d, p2d, p2d, p2d, x2d, wa, wb, wc, wo, hgn.reshape(1, LANES),
      gate1, n2.reshape(1, d), sc2, sh2, w_route)


_STRICT_LOWER = np.tril(np.ones((ROUTE_ROWS, ROUTE_ROWS), np.float32), -1)
_NEG_BIG = -3.0e38


def _lane_argmax(x, lane):
    top = jnp.max(x, axis=1, keepdims=True)
    idx = jnp.min(jnp.where(x == top, lane, LANES), axis=1, keepdims=True)
    return top, idx


def _route_kernel(lg_ref, tri_ref, o_ref, cnt_ref, run_scr):
    @pl.when(pl.program_id(0) == 0)
    def _():
        run_scr[...] = jnp.zeros_like(run_scr)

    lg = lg_ref[...]
    lane = lax.broadcasted_iota(jnp.int32, lg.shape, 1)
    is_grp = lane < N_GROUPS
    gtop, gidx = _lane_argmax(jnp.where(is_grp, lg, _NEG_BIG), lane)
    grp_w = 1.0 / jnp.sum(jnp.where(is_grp, jnp.exp(lg - gtop), 0.0), axis=1, keepdims=True)
    lo = N_GROUPS + EXP_PER_GROUP * gidx
    x1 = jnp.where((lane >= lo) & (lane < lo + EXP_PER_GROUP), lg, _NEG_BIG)
    t1, i1 = _lane_argmax(x1, lane)
    t2, i2 = _lane_argmax(jnp.where(lane == i1, _NEG_BIG, x1), lane)
    r = jnp.exp(t2 - t1)
    w1 = grp_w / (1.0 + r)
    w2 = w1 * r
    e1 = i1 - N_GROUPS
    e2 = i2 - N_GROUPS
    hot1 = lane == e1
    hot2 = lane == e2
    hot = jnp.where(hot1 | hot2, 1.0, 0.0)
    before = run_scr[...] + jnp.dot(tri_ref[...], hot.astype(BF16), preferred_element_type=F32)
    rank1 = jnp.sum(jnp.where(hot1, before, 0.0), axis=1, keepdims=True)
    rank2 = jnp.sum(jnp.where(hot2, before, 0.0), axis=1, keepdims=True)
    run_scr[...] = run_scr[...] + jnp.sum(hot, axis=0, keepdims=True)
    cnt_ref[...] = jnp.broadcast_to(run_scr[...], cnt_ref.shape)
    out = jnp.where(lane == 0, e1.astype(F32), jnp.where(lane == 1, e2.astype(F32), 0.0))
    out = jnp.where(lane == 2, w1, jnp.where(lane == 3, w2, out))
    o_ref[...] = jnp.where(lane == 4, rank1, jnp.where(lane == 5, rank2, out))


def route(logits):
    t = logits.shape[0]
    return pl.pallas_call(
        _route_kernel,
        out_shape=(jax.ShapeDtypeStruct((t, LANES), F32), jax.ShapeDtypeStruct((8, LANES), F32)),
        grid=(t // ROUTE_ROWS,),
        in_specs=[pl.BlockSpec((ROUTE_ROWS, LANES), lambda i: (i, 0)),
                  pl.BlockSpec((ROUTE_ROWS, ROUTE_ROWS), lambda i: (0, 0))],
        out_specs=(pl.BlockSpec((ROUTE_ROWS, LANES), lambda i: (i, 0)),
                   pl.BlockSpec((8, LANES), lambda i: (0, 0))),
        scratch_shapes=[pltpu.VMEM((1, LANES), F32)],
        compiler_params=_cparams(("arbitrary",)),
    )(logits, jnp.asarray(_STRICT_LOWER, BF16))


def _expert_kernel(be_ref, x_ref, wg_ref, wu_ref, wd_ref, o_ref, wg_s, wu_s, wd_s):
    i = pl.program_id(0)

    @pl.when((i == 0) | (be_ref[i] != be_ref[jnp.maximum(i - 1, 0)]))
    def _():
        wg_s[...] = wg_ref[0].astype(BF16)
        wu_s[...] = wu_ref[0].astype(BF16)
        wd_s[...] = wd_ref[0].astype(BF16)

    x = x_ref[...]
    gte = jnp.dot(x, wg_s[...], preferred_element_type=F32)
    up = jnp.dot(x, wu_s[...], preferred_element_type=F32)
    hid = (gte * _sigmoid(gte) * up).astype(BF16)
    o_ref[...] = jnp.dot(hid, wd_s[...], preferred_element_type=F32).astype(o_ref.dtype)


def expert_ffn(rows, block_expert, wg, wu, wd):
    n, d = rows.shape
    de = wg.shape[2]
    return pl.pallas_call(
        _expert_kernel,
        out_shape=jax.ShapeDtypeStruct((n, d), BF16),
        grid_spec=pltpu.PrefetchScalarGridSpec(
            num_scalar_prefetch=1,
            grid=(n // MOE_ROWS,),
            in_specs=[
                pl.BlockSpec((MOE_ROWS, d), lambda i, be: (i, 0)),
                pl.BlockSpec((1, d, de), lambda i, be: (be[i], 0, 0)),
                pl.BlockSpec((1, d, de), lambda i, be: (be[i], 0, 0)),
                pl.BlockSpec((1, de, d), lambda i, be: (be[i], 0, 0)),
            ],
            out_specs=pl.BlockSpec((MOE_ROWS, d), lambda i, be: (i, 0)),
            scratch_shapes=[pltpu.VMEM((d, de), BF16), pltpu.VMEM((d, de), BF16), pltpu.VMEM((de, d), BF16)],
        ),
        compiler_params=_cparams(("arbitrary",)),
    )(block_expert, rows, wg, wu, wd)


def _combine_kernel(x_ref, r1_ref, r2_ref, info_ref, gate_ref, *rest, final):
    o_ref = rest[-1]
    info = info_ref[...]
    y = info[:, 2:3] * r1_ref[...].astype(F32) + info[:, 3:4] * r2_ref[...].astype(F32)
    x = x_ref[...] + gate_ref[0] * y
    if final:
        ms = jnp.mean(x * x, axis=-1, keepdims=True)
        x = x * lax.rsqrt(ms + EPS) * rest[0][...]
    o_ref[...] = x


def combine(x2d, r1, r2, info, row_off, gate, rows_per_batch, final_g=None):
    t, d = x2d.shape
    tm = min(ROUTE_ROWS, rows_per_batch)
    per = rows_per_batch // tm
    off = row_off // tm
    row = lambda i: (i, 0)
    shifted = lambda i: (i + off, 0)
    specs = [pl.BlockSpec((tm, d), row), pl.BlockSpec((tm, d), shifted), pl.BlockSpec((tm, d), shifted),
             pl.BlockSpec((tm, LANES), shifted), pl.BlockSpec((1, 1, d), lambda i: (i // per, 0, 0))]
    args = [x2d, r1, r2, info, gate]
    if final_g is not None:
        specs.append(pl.BlockSpec((1, d), lambda i: (0, 0)))
        args.append(final_g.reshape(1, d))
    return pl.pallas_call(
        functools.partial(_combine_kernel, final=final_g is not None),
        out_shape=jax.ShapeDtypeStruct((t, d), F32), grid=(t // tm,),
        in_specs=specs, out_specs=pl.BlockSpec((tm, d), row),
        compiler_params=_cparams(("parallel",)),
    )(*args)


def _dispatch(info, counts_row):
    n_tok = info.shape[0]
    expert = info[:, 0:TOP_K].astype(jnp.int32)
    rank = info[:, 4:4 + TOP_K].astype(jnp.int32)
    counts = counts_row[0, :N_EXPERTS].astype(jnp.int32)
    padded = (counts + MOE_ROWS - 1) // MOE_ROWS * MOE_ROWS
    pad_end = jnp.cumsum(padded)
    pad_start = pad_end - padded
    hot = expert[:, :, None] == jnp.arange(N_EXPERTS, dtype=jnp.int32)
    dest = jnp.sum(jnp.where(hot, pad_start, 0), axis=-1) + rank
    n_blocks = -(-(n_tok * TOP_K + N_EXPERTS * (MOE_ROWS - 1)) // MOE_ROWS)
    token = jnp.broadcast_to(jnp.arange(n_tok, dtype=jnp.int32)[:, None], dest.shape)
    src_tok = jnp.zeros((n_blocks * MOE_ROWS,), jnp.int32).at[dest.reshape(-1)].set(token.reshape(-1))
    block_start = jnp.arange(n_blocks, dtype=jnp.int32)[:, None] * MOE_ROWS
    block_expert = jnp.minimum(jnp.sum((pad_end[None, :] <= block_start).astype(jnp.int32), axis=1),
                               N_EXPERTS - 1)
    return src_tok, dest, block_expert


def _rope_tables(seq):
    n_rows = seq // GRID_W
    row = jnp.repeat(jnp.arange(n_rows), GRID_W).astype(F32)
    col = jnp.tile(jnp.arange(GRID_W), n_rows).astype(F32)
    axis_pairs = HEAD_DIM // 4
    inv = ROPE_THETA ** (-jnp.arange(axis_pairs, dtype=F32) / axis_pairs)
    ang = jnp.concatenate([row[:, None] * inv, col[:, None] * inv], axis=-1)
    cos, sin = jnp.cos(ang), jnp.sin(ang)
    cos128 = jnp.concatenate([cos, cos, cos, cos], axis=-1)
    sins128 = jnp.concatenate([-sin, sin, -sin, sin], axis=-1)
    return cos128, sins128


def kernel(x, c, ctx, c_ctx, w_mod, b_mod, norm1_g, norm2_g, w_in, hgrn_lb_logits, hgrn_out_norm_g,
           attn_q_norm_g, attn_k_norm_g, swa_sink, w_branch_a, w_branch_b, w_branch_c, w_out,
           w_group, w_router, w_exp_gate, w_exp_up, w_exp_down, final_norm_g):
    bsz, seq, d = x.shape
    n_ctx = ctx.shape[1]
    depth = w_mod.shape[0]
    cos, sins = _rope_tables(seq)
    lb_p = jax.nn.softmax(hgrn_lb_logits.astype(F32), axis=0)
    lower_bounds = jnp.cumsum(lb_p, axis=0) - lb_p[0]
    no_sink = jnp.zeros((N_HEADS,), F32)
    ones128 = jnp.ones((1, LANES), F32)

    xl = x.reshape(bsz * seq, d)
    xc = ctx.reshape(bsz * n_ctx, d)
    for layer in range(depth):
        ctx_out = layer < depth - 1
        mod_l = jax.nn.silu(c) @ w_mod[layer] + b_mod[layer]
        mod_c = jnp.broadcast_to(jax.nn.silu(c_ctx) @ w_mod[layer] + b_mod[layer], (bsz, 6 * d))
        ml = [m.reshape(bsz, 1, d) for m in jnp.split(mod_l, 6, axis=-1)]
        mc = [m.reshape(bsz, 1, d) for m in jnp.split(mod_c, 6, axis=-1)]

        w_in_b = w_in[layer][:, _COL_PERM].astype(BF16)
        p_l = inproj(xl, seq, norm1_g[layer], ml[1], ml[0], w_in_b, 1024, 1024)
        p_c = inproj(xc, n_ctx, norm1_g[layer], mc[1], mc[0], w_in_b, 256, 1024)
        p_l3 = p_l.reshape(bsz, seq, D_IN)
        p_c3 = p_c.reshape(bsz, n_ctx, D_IN)

        s0 = jnp.zeros((bsz, 2, HG_HEADS, HG_KDIM, HG_KDIM), F32)
        o_c, s_c = hgrn_scan(p_c3, lower_bounds[layer], s0, 256)
        o_l, _ = hgrn_scan(p_l3, lower_bounds[layer], s_c, 512)

        gq = jnp.tile(attn_q_norm_g[layer], 2).reshape(1, LANES)
        gk = jnp.tile(attn_k_norm_g[layer], 2).reshape(1, LANES)
        kt_l, v_l, kn_l = prep_kv(p_l3, COL["ak"], COL["av"], gk, cos, sins, True, True, 512)
        kt_c, v_c, kn_c = prep_kv(p_c3, COL["ak"], COL["av"], gk, cos, sins, True, False, 256)
        kmax = _key_norm_max([kn_l, kn_c])
        q_l, u_l = prep_q(p_l3, COL["aq"], gq, cos, sins, kmax, True, True, 512)
        kt_all = jnp.concatenate([kt_c, kt_l], axis=3)
        v_all = jnp.concatenate([v_c, v_l], axis=2)
        b_l = attention(q_l, kt_all, v_all, jnp.max(u_l), no_sink, 256, 1280)

        skt_l, sv_l, skn_l = prep_kv(p_l3, COL["sk"], COL["sv"], ones128, cos, sins, False, True, 512)
        skt_c, sv_c, skn_c = prep_kv(p_c3, COL["sk"], COL["sv"], ones128, cos, sins, False, False, 256)
        skmax = _key_norm_max([skn_l, skn_c])
        sq_l, su_l = prep_q(p_l3, COL["sq"], ones128, cos, sins, skmax, False, True, 512)
        zk = jnp.zeros((bsz, N_KV, LANES, WINDOW), BF16)
        zv = jnp.zeros((bsz, N_KV, WINDOW, LANES), BF16)
        skt_all = jnp.concatenate([skt_c, zk, skt_l, zk], axis=3)
        sv_all = jnp.concatenate([sv_c, zv, sv_l, zv], axis=2)
        c_l = attention(sq_l, skt_all, sv_all, jnp.max(su_l), swa_sink[layer], 256, 256,
                        use_sink=True, band=n_ctx)

        wa = w_branch_a[layer].astype(BF16)
        wb = w_branch_b[layer].astype(BF16)
        wc = w_branch_c[layer].astype(BF16)
        wo = w_out[layer].astype(BF16)
        w_route = jnp.concatenate(
            [w_group[layer], w_router[layer], jnp.zeros((d, LANES - N_GROUPS - N_EXPERTS), F32)], axis=1)
        xl, h_l, lg_l = merge(o_l.reshape(2, bsz * seq, HG_W), p_l, b_l.reshape(-1, ATT_W),
                              c_l.reshape(-1, ATT_W), xl, seq, wa, wb, wc, wo, hgrn_out_norm_g[layer],
                              ml[2], norm2_g[layer], ml[4], ml[3], w_route, 256)
        if ctx_out:
            q_c, u_c = prep_q(p_c3, COL["aq"], gq, cos, sins, kmax, True, False, 256)
            b_c = attention(q_c, kt_c, v_c, jnp.max(u_c), no_sink, 256, 256)
            sq_c, su_c = prep_q(p_c3, COL["sq"], ones128, cos, sins, skmax, False, False, 256)
            c_c = attention(sq_c, skt_c, sv_c, jnp.max(su_c), swa_sink[layer], 256, 256, use_sink=True)
            xc, h_c, lg_c = merge(o_c.reshape(2, bsz * n_ctx, HG_W), p_c, b_c.reshape(-1, ATT_W),
                                  c_c.reshape(-1, ATT_W), xc, n_ctx, wa, wb, wc, wo, hgrn_out_norm_g[layer],
                                  mc[2], norm2_g[layer], mc[4], mc[3], w_route, 256)
            h_all = jnp.concatenate([h_c, h_l], axis=0)
            lg_all = jnp.concatenate([lg_c, lg_l], axis=0)
        else:
            h_all, lg_all = h_l, lg_l

        info, counts_row = route(lg_all)
        src_tok, dest, block_expert = _dispatch(info, counts_row)
        rows = jnp.take(h_all, src_tok, axis=0)
        out_rows = expert_ffn(rows, block_expert, w_exp_gate[layer], w_exp_up[layer], w_exp_down[layer])
        r1 = jnp.take(out_rows, dest[:, 0], axis=0)
        r2 = jnp.take(out_rows, dest[:, 1], axis=0)
        n_c = bsz * n_ctx if ctx_out else 0
        if ctx_out:
            xc = combine(xc, r1, r2, info, 0, mc[5], n_ctx)
        xl = combine(xl, r1, r2, info, n_c, ml[5], seq, final_g=final_norm_g if layer == depth - 1 else None)
    return xl.reshape(bsz, seq, d)
```

```python
import functools
import math

import numpy as np
import jax
import jax.numpy as jnp
from jax import lax
from jax.experimental import pallas as pl
from jax.experimental.pallas import tpu as pltpu

F32 = jnp.float32
BF16 = jnp.bfloat16

EPS = 1e-6
MASK_VALUE = -1e30
TINY = 1e-30
GRID_W = 64
ROPE_THETA = 10000.0

HG_HEADS = 4
HG_KDIM = 128
HG_W = HG_HEADS * HG_KDIM
HEAD_DIM = 64
N_HEADS = 8
N_KV = 2
GROUP = N_HEADS // N_KV
ATT_W = N_HEADS * HEAD_DIM
KV_W = N_KV * HEAD_DIM
WINDOW = 128
ATTN_SCALE = HEAD_DIM ** -0.5
N_GROUPS = 4
EXP_PER_GROUP = 8
N_EXPERTS = N_GROUPS * EXP_PER_GROUP
TOP_K = 2

LANES = 128
HG_CHUNK = 128
HG_LEVELS = int(math.log2(HG_CHUNK))
HG_TOTAL_ROWS = 16
MOE_ROWS = 256
ROUTE_ROWS = 512
VMEM_LIMIT = 56 * 1024 * 1024

LOG2E = 1.4426950408889634
SHIFT_HEADROOM = 57.0
SAFE_BOUND = 90.0


def _cparams(sem):
    return pltpu.CompilerParams(dimension_semantics=sem, vmem_limit_bytes=VMEM_LIMIT)


_SIZES = (HG_W, HG_W, HG_W, HG_W, HG_W, ATT_W, KV_W, KV_W, ATT_W, KV_W, KV_W, 3 * 1024)
_NAMES = ("hq", "ff", "fb", "hi", "hg", "aq", "ak", "av", "sq", "sk", "sv", "gates")
COL = {n: int(sum(_SIZES[:i])) for i, n in enumerate(_NAMES)}
D_IN = int(sum(_SIZES))


def _inproj_kernel(x_ref, g_ref, sc_ref, sh_ref, w_ref, o_ref, h_scr):
    @pl.when(pl.program_id(1) == 0)
    def _():
        x = x_ref[...]
        ms = jnp.mean(x * x, axis=-1, keepdims=True)
        y = x * lax.rsqrt(ms + EPS) * g_ref[...]
        h_scr[...] = (y * (1.0 + sc_ref[0]) + sh_ref[0]).astype(BF16)

    o_ref[...] = jnp.dot(h_scr[...], w_ref[...], preferred_element_type=F32).astype(o_ref.dtype)


def inproj(x2d, rows_per_batch, g, scale, shift, w_bf16, tm, tn):
    t, d = x2d.shape
    n = w_bf16.shape[1]
    tm = min(tm, rows_per_batch)
    per = rows_per_batch // tm
    return pl.pallas_call(
        _inproj_kernel,
        out_shape=jax.ShapeDtypeStruct((t, n), BF16),
        grid=(t // tm, n // tn),
        in_specs=[
            pl.BlockSpec((tm, d), lambda i, j: (i, 0)),
            pl.BlockSpec((1, d), lambda i, j: (0, 0)),
            pl.BlockSpec((1, 1, d), lambda i, j: (i // per, 0, 0)),
            pl.BlockSpec((1, 1, d), lambda i, j: (i // per, 0, 0)),
            pl.BlockSpec((d, tn), lambda i, j: (0, j)),
        ],
        out_specs=pl.BlockSpec((tm, tn), lambda i, j: (i, j)),
        scratch_shapes=[pltpu.VMEM((tm, d), BF16)],
        compiler_params=_cparams(("parallel", "arbitrary")),
    )(x2d, g.reshape(1, d), scale, shift, w_bf16)


def _hgrn_exponent_matrices():
    c, nl = HG_CHUNK, HG_LEVELS
    rows = (nl + 2) * c + HG_TOTAL_ROWS
    out = np.zeros((2, rows, c), np.float32)
    for d in range(2):
        pos = np.arange(c) if d == 0 else c - 1 - np.arange(c)
        for l in range(nl):
            m = 1 << l
            for t in range(c):
                p = pos[t]
                mid = (p // (2 * m)) * 2 * m + m
                if p >= mid:
                    sel = (pos >= mid) & (pos <= p)
                else:
                    sel = (pos > p) & (pos <= mid - 1)
                out[d, l * c + t, sel] = 1.0
        for t in range(c):
            out[d, nl * c + t, pos <= pos[t]] = 1.0
            out[d, (nl + 1) * c + t, pos > pos[t]] = 1.0
        out[d, (nl + 2) * c:, :] = 1.0
    return np.concatenate([out, out], axis=2)


_HG_EXP_MATS = _hgrn_exponent_matrices()


def _sigmoid(x):
    return 1.0 / (1.0 + jnp.exp(-x))


def _hgrn_kernel(q_ref, f_ref, i_ref, lb_ref, mat_ref, s0_ref, o_ref, st_ref, *, n_chunks):
    c, nl = HG_CHUNK, HG_LEVELS
    d = pl.program_id(1)

    @pl.when(pl.program_id(2) == 0)
    def _():
        st_ref[...] = s0_ref[...]

    row = lax.broadcasted_iota(jnp.int32, (c, HG_W), 0)
    pos = row + d * (c - 1 - 2 * row)
    xor_rc = lax.broadcasted_iota(jnp.int32, (c, c), 0) ^ lax.broadcasted_iota(jnp.int32, (c, c), 1)

    def chunk(ci, carry):
        cc = ci + d * (n_chunks - 1 - 2 * ci)
        r0 = pl.multiple_of(cc * c, c)
        qraw = q_ref[0, pl.ds(r0, c), :].astype(F32)
        fz = f_ref[0, pl.ds(r0, c), :].astype(F32)
        vb = i_ref[0, pl.ds(r0, c), :]
        v = vb.astype(F32)
        lb = lb_ref[...]
        sig = _sigmoid(fz)
        logf = jnp.log(jnp.maximum(lb + (1.0 - lb) * sig, TINY))
        key = (1.0 - lb) * (1.0 - sig)
        qh = qraw * _sigmoid(qraw) * (HG_KDIM ** -0.5)
        hi = logf.astype(BF16)
        lo = (logf - hi.astype(F32)).astype(BF16)
        hilo = jnp.concatenate([hi, lo], axis=0)

        def decay(r_lo, r_hi):
            return jnp.exp(jnp.dot(mat_ref[0, r_lo:r_hi, :], hilo, preferred_element_type=F32))

        a = [None] * HG_HEADS
        for l in range(nl):
            dl = decay(l * c, (l + 1) * c)
            is_q = ((pos >> l) & 1) == 1
            ql = jnp.where(is_q, qh * dl, 0.0).astype(BF16)
            kl = jnp.where(is_q, 0.0, key * dl).astype(BF16)
            for h in range(HG_HEADS):
                lanes = slice(h * LANES, (h + 1) * LANES)
                pl_ = lax.dot_general(ql[:, lanes], kl[:, lanes], (((1,), (1,)), ((), ())),
                                      preferred_element_type=F32)
                a[h] = pl_ if l == 0 else pl_ + jnp.where(xor_rc < (1 << l), a[h], 0.0)
        q_in = (qh * decay(nl * c, (nl + 1) * c)).astype(BF16)
        k_out = (key * decay((nl + 1) * c, (nl + 2) * c)).astype(BF16)
        total = decay((nl + 2) * c, (nl + 2) * c + HG_TOTAL_ROWS)[0:1]
        qk = qh * key
        for h in range(HG_HEADS):
            lanes = slice(h * LANES, (h + 1) * LANES)
            st = st_ref[0, 0, h]
            o = (jnp.dot(a[h].astype(BF16), vb[:, lanes], preferred_element_type=F32)
                 + jnp.sum(qk[:, lanes], axis=1, keepdims=True) * v[:, lanes]
                 + lax.dot_general(q_in[:, lanes], st.astype(BF16), (((1,), (1,)), ((), ())),
                                   preferred_element_type=F32))
            o_ref[0, 0, pl.ds(r0, c), lanes] = o.astype(o_ref.dtype)
            st_ref[0, 0, h] = total[:, lanes] * st + lax.dot_general(
                vb[:, lanes], k_out[:, lanes], (((0,), (0,)), ((), ())), preferred_element_type=F32)
        return carry

    lax.fori_loop(0, n_chunks, chunk, 0)


def hgrn_scan(p3, lb, s0, tb):
    b, l, _ = p3.shape
    tb = min(tb, l)
    nb = l // tb
    blk = lambda d, n: n + d * (nb - 1 - 2 * n)
    wblk = HG_W
    kern = functools.partial(_hgrn_kernel, n_chunks=tb // HG_CHUNK)
    mats = jnp.asarray(_HG_EXP_MATS, BF16)
    return pl.pallas_call(
        kern,
        out_shape=(jax.ShapeDtypeStruct((2, b, l, HG_W), BF16),
                   jax.ShapeDtypeStruct((b, 2, HG_HEADS, HG_KDIM, HG_KDIM), F32)),
        grid=(b, 2, nb),
        in_specs=[
            pl.BlockSpec((1, tb, wblk), lambda bi, d, n: (bi, blk(d, n), COL["hq"] // wblk)),
            pl.BlockSpec((1, tb, wblk), lambda bi, d, n: (bi, blk(d, n), COL["ff"] // wblk + d)),
            pl.BlockSpec((1, tb, wblk), lambda bi, d, n: (bi, blk(d, n), COL["hi"] // wblk)),
            pl.BlockSpec((1, HG_W), lambda bi, d, n: (0, 0)),
            pl.BlockSpec((1,) + _HG_EXP_MATS.shape[1:], lambda bi, d, n: (d, 0, 0)),
            pl.BlockSpec((1, 1, HG_HEADS, HG_KDIM, HG_KDIM), lambda bi, d, n: (bi, d, 0, 0, 0)),
        ],
        out_specs=(
            pl.BlockSpec((1, 1, tb, HG_W), lambda bi, d, n: (d, bi, blk(d, n), 0)),
            pl.BlockSpec((1, 1, HG_HEADS, HG_KDIM, HG_KDIM), lambda bi, d, n: (bi, d, 0, 0, 0)),
        ),
        compiler_params=_cparams(("parallel", "arbitrary", "arbitrary")),
    )(p3, p3, p3, lb.reshape(1, HG_W), mats, s0)


_GROUP_ONES = np.kron(np.eye(LANES // HEAD_DIM, dtype=np.float32), np.ones((HEAD_DIM, HEAD_DIM), np.float32))
_GROUP_ONES2 = np.concatenate([_GROUP_ONES, _GROUP_ONES], axis=0)


def _group_ssq(x, ones2):
    sq = x * x
    hi = sq.astype(BF16)
    lo = (sq - hi.astype(F32)).astype(BF16)
    return jnp.dot(jnp.concatenate([hi, lo], axis=1), ones2, preferred_element_type=F32)


def _prep_tile(x, g, cos, sins, ones2, norm, rope):
    if norm:
        x = x * lax.rsqrt(_group_ssq(x, ones2) * (1.0 / HEAD_DIM) + EPS) * g
    if rope:
        lane = lax.broadcasted_iota(jnp.int32, x.shape, 1)
        first = (lane % HEAD_DIM) < (HEAD_DIM // 2)
        other = jnp.where(first, pltpu.roll(x, LANES - HEAD_DIM // 2, 1), pltpu.roll(x, HEAD_DIM // 2, 1))
        x = x * cos + other * sins
    return x


def _prep_kv_kernel(k_ref, v_ref, g_ref, cos_ref, sin_ref, ones_ref, kt_ref, vo_ref, kn_ref, *, norm, rope):
    k = _prep_tile(k_ref[0].astype(F32), g_ref[...], cos_ref[...], sin_ref[...], ones_ref[...], norm, rope)
    kb = k.astype(BF16).astype(F32)
    ssq = _group_ssq(kb, ones_ref[...])
    kn_ref[0, 0] = jnp.broadcast_to(jnp.max(ssq, axis=0, keepdims=True), (8, LANES))
    kt = kb.T
    row = lax.broadcasted_iota(jnp.int32, kt.shape, 0)
    v = v_ref[0]
    low = lax.broadcasted_iota(jnp.int32, v.shape, 1) < HEAD_DIM
    one = jnp.ones((), v.dtype)
    for g in range(N_KV):
        ktg = kt if g == 0 else jnp.concatenate([kt[HEAD_DIM:], kt[:HEAD_DIM]], axis=0)
        kt_ref[0, g] = jnp.where(row < HEAD_DIM, ktg, jnp.where(row == HEAD_DIM, 1.0, 0.0)).astype(BF16)
        vg = v if g == 0 else jnp.concatenate([v[:, HEAD_DIM:], v[:, :HEAD_DIM]], axis=1)
        vo_ref[0, g] = jnp.where(low, vg, one).astype(BF16)


def _prep_q_kernel(kmax_ref, x0_ref, x1_ref, x2_ref, x3_ref, g_ref, cos_ref, sin_ref, ones_ref, o_ref, u_ref,
                   *, norm, rope):
    b = pl.program_id(0)
    lane = lax.broadcasted_iota(jnp.int32, (x0_ref.shape[1], LANES), 1)
    umax = None
    for tile, x_ref in enumerate((x0_ref, x1_ref, x2_ref, x3_ref)):
        kv = (2 * tile) // GROUP
        x = x_ref[0].astype(F32)
        y = _prep_tile(x, g_ref[...], cos_ref[...], sin_ref[...], ones_ref[...], norm, rope) * (ATTN_SCALE * LOG2E)
        yb = y.astype(BF16).astype(F32)
        u2 = jnp.sqrt(_group_ssq(yb, ones_ref[...])) * kmax_ref[b * N_KV + kv]
        tmax = jnp.max(u2, axis=0, keepdims=True)
        umax = tmax if umax is None else jnp.maximum(umax, tmax)
        neg_shift = SHIFT_HEADROOM - u2
        y_sw = pltpu.roll(yb, HEAD_DIM, 1)
        ns_sw = pltpu.roll(neg_shift, HEAD_DIM, 1)
        for half in range(2):
            h = 2 * tile + half
            data = yb if half == 0 else y_sw
            ns = ns_sw if half == 0 else neg_shift
            o_ref[0, :, h * LANES:(h + 1) * LANES] = jnp.where(
                lane < HEAD_DIM, data, jnp.where(lane == HEAD_DIM, ns, 0.0)).astype(BF16)
    u_ref[0, 0] = jnp.broadcast_to(umax, (8, LANES))


def prep_kv(p3, col_k, col_v, g128, cos, sins, norm, rope, tq):
    b, l, _ = p3.shape
    tq = min(tq, l)
    ones2 = jnp.asarray(_GROUP_ONES2, BF16)
    return pl.pallas_call(
        functools.partial(_prep_kv_kernel, norm=norm, rope=rope),
        out_shape=(jax.ShapeDtypeStruct((b, N_KV, LANES, l), BF16),
                   jax.ShapeDtypeStruct((b, N_KV, l, LANES), BF16),
                   jax.ShapeDtypeStruct((b, l // tq, 8, LANES), F32)),
        grid=(b, l // tq),
        in_specs=[
            pl.BlockSpec((1, tq, KV_W), lambda bi, i: (bi, i, col_k // KV_W)),
            pl.BlockSpec((1, tq, KV_W), lambda bi, i: (bi, i, col_v // KV_W)),
            pl.BlockSpec((1, LANES), lambda bi, i: (0, 0)),
            pl.BlockSpec((tq, LANES), lambda bi, i: (i, 0)),
            pl.BlockSpec((tq, LANES), lambda bi, i: (i, 0)),
            pl.BlockSpec((2 * LANES, LANES), lambda bi, i: (0, 0)),
        ],
        out_specs=(pl.BlockSpec((1, N_KV, LANES, tq), lambda bi, i: (bi, 0, 0, i)),
                   pl.BlockSpec((1, N_KV, tq, LANES), lambda bi, i: (bi, 0, i, 0)),
                   pl.BlockSpec((1, 1, 8, LANES), lambda bi, i: (bi, i, 0, 0))),
        compiler_params=_cparams(("parallel", "parallel")),
    )(p3, p3, g128, cos, sins, ones2)


def prep_q(p3, col, g128, cos, sins, kmax, norm, rope, tq):
    b, l, _ = p3.shape
    tq = min(tq, l)
    ones2 = jnp.asarray(_GROUP_ONES2, BF16)
    return pl.pallas_call(
        functools.partial(_prep_q_kernel, norm=norm, rope=rope),
        out_shape=(jax.ShapeDtypeStruct((b, l, N_HEADS * LANES), BF16),
                   jax.ShapeDtypeStruct((b, l // tq, 8, LANES), F32)),
        grid_spec=pltpu.PrefetchScalarGridSpec(
            num_scalar_prefetch=1,
            grid=(b, l // tq),
            in_specs=[
                pl.BlockSpec((1, tq, LANES), lambda bi, i, s, t=t: (bi, i, col // LANES + t))
                for t in range(ATT_W // LANES)
            ] + [
                pl.BlockSpec((1, LANES), lambda bi, i, s: (0, 0)),
                pl.BlockSpec((tq, LANES), lambda bi, i, s: (i, 0)),
                pl.BlockSpec((tq, LANES), lambda bi, i, s: (i, 0)),
                pl.BlockSpec((2 * LANES, LANES), lambda bi, i, s: (0, 0)),
            ],
            out_specs=(pl.BlockSpec((1, tq, N_HEADS * LANES), lambda bi, i, s: (bi, i, 0)),
                       pl.BlockSpec((1, 1, 8, LANES), lambda bi, i, s: (bi, i, 0, 0))),
        ),
        compiler_params=_cparams(("parallel", "parallel")),
    )(kmax, p3, p3, p3, p3, g128, cos, sins, ones2)


def _key_norm_max(kn_list):
    kn = functools.reduce(jnp.maximum, [jnp.max(k, axis=(1, 2)) for k in kn_list])
    return jnp.sqrt(kn[:, ::HEAD_DIM]).reshape(-1)


def _attn_kernel(sink_ref, q_ref, kt_ref, v_ref, o_ref, *, tk, online, use_sink, band, seq):
    tq = q_ref.shape[1]
    rows = GROUP * tq
    lane = lax.broadcasted_iota(jnp.int32, (tq, LANES), 1)
    low = lane < HEAD_DIM
    if band is not None:
        width = tq + 2 * WINDOW
        start = pl.multiple_of(pl.program_id(1) * tq, tq)
        r = lax.broadcasted_iota(jnp.int32, (rows, width), 0) % tq
        kp = lax.broadcasted_iota(jnp.int32, (rows, width), 1)
        key_pos = start + kp - WINDOW
        mask = (jnp.abs(kp - WINDOW - r) <= WINDOW) & (key_pos >= 0) & (key_pos < seq)
    for g in range(N_KV):
        tiles = [q_ref[0, :, (g * GROUP + hh) * LANES:(g * GROUP + hh + 1) * LANES] for hh in range(GROUP)]
        qg = jnp.concatenate(tiles, axis=0)

        def scores(kt, qg=qg):
            return jnp.dot(qg, kt, preferred_element_type=F32)

        def update(carry, s, v, msk=None):
            if msk is not None:
                s = jnp.where(msk, s, MASK_VALUE)
            if online:
                m, acc = carry
                m_new = jnp.maximum(m, jnp.max(s, axis=1, keepdims=True))
                p = jnp.exp2(s - m_new).astype(BF16)
                return m_new, jnp.exp2(m - m_new) * acc + jnp.dot(p, v, preferred_element_type=F32)
            return carry + jnp.dot(jnp.exp2(s).astype(BF16), v, preferred_element_type=F32)

        acc0 = jnp.zeros((rows, LANES), F32)
        carry = (jnp.full((rows, 1), MASK_VALUE, F32), acc0) if online else acc0
        if band is None:
            nk = kt_ref.shape[3] // tk

            def body(j, sc, g=g, scores=scores, update=update):
                s, c = sc
                k0 = pl.multiple_of(j * tk, tk)
                s_next = scores(kt_ref[0, g, :, pl.ds(pl.multiple_of(k0 + tk, tk), tk)])
                return s_next, update(c, s, v_ref[0, g, pl.ds(k0, tk), :])

            s_last, carry = lax.fori_loop(0, nk - 1, body, (scores(kt_ref[0, g, :, 0:tk]), carry))
            carry = update(carry, s_last, v_ref[0, g, (nk - 1) * tk:nk * tk, :])
        else:
            carry = update(carry, scores(kt_ref[0, g, :, 0:band]), v_ref[0, g, 0:band, :])
            b0 = pl.multiple_of(band + start, LANES)
            carry = update(carry, scores(kt_ref[0, g, :, pl.ds(b0, width)]), v_ref[0, g, pl.ds(b0, width), :], mask)
        acc = carry[1] if online else carry
        if use_sink:
            e = jnp.concatenate([t[:, HEAD_DIM:HEAD_DIM + 1].astype(F32) + sink_ref[g * GROUP + hh] * LOG2E
                                 for hh, t in enumerate(tiles)], axis=0)
            if online:
                e = e - carry[0]
            lane_r = lax.broadcasted_iota(jnp.int32, (rows, LANES), 1)
            acc = acc + jnp.where(lane_r >= HEAD_DIM, jnp.exp2(e), 0.0)
        o = acc * pltpu.roll(1.0 / acc, HEAD_DIM, 1)
        for pair in range(GROUP // 2):
            a = o[(2 * pair) * tq:(2 * pair + 1) * tq]
            b = o[(2 * pair + 1) * tq:(2 * pair + 2) * tq]
            t0 = (g * GROUP // 2 + pair) * LANES
            o_ref[0, :, t0:t0 + LANES] = jnp.where(low, a, pltpu.roll(b, HEAD_DIM, 1)).astype(o_ref.dtype)


def _attention(q, kt, v, sink, tq, tk, online, use_sink, band):
    b, l, _ = q.shape
    sk = kt.shape[3]
    tq = min(tq, l)
    tk = min(tk, sk)
    kern = functools.partial(_attn_kernel, tk=tk, online=online, use_sink=use_sink, band=band, seq=l)
    return pl.pallas_call(
        kern,
        out_shape=jax.ShapeDtypeStruct((b, l, ATT_W), BF16),
        grid_spec=pltpu.PrefetchScalarGridSpec(
            num_scalar_prefetch=1,
            grid=(b, l // tq),
            in_specs=[
                pl.BlockSpec((1, tq, N_HEADS * LANES), lambda bi, i, s: (bi, i, 0)),
                pl.BlockSpec((1, N_KV, LANES, sk), lambda bi, i, s: (bi, 0, 0, 0)),
                pl.BlockSpec((1, N_KV, sk, LANES), lambda bi, i, s: (bi, 0, 0, 0)),
            ],
            out_specs=pl.BlockSpec((1, tq, ATT_W), lambda bi, i, s: (bi, i, 0)),
        ),
        compiler_params=_cparams(("parallel", "arbitrary")),
    )(sink, q, kt, v)


def attention(q, kt, v, ubound, sink, tq, tk, use_sink=False, band=None):
    return lax.cond(ubound <= SAFE_BOUND,
                    lambda: _attention(q, kt, v, sink, tq, tk, False, use_sink, band),
                    lambda: _attention(q, kt, v, sink, tq, tk, True, use_sink, band))


def _merge_kernel(of_ref, ob_ref, hg_ref, b_ref, c_ref, ga_ref, gb_ref, gc_ref, x_ref,
                  wa_ref, wb_ref, wc_ref, wo_ref, hgn_ref, gate1_ref, n2_ref, sc2_ref, sh2_ref, wr_ref,
                  xo_ref, h2_ref, lg_ref):
    o = of_ref[0].astype(F32) + ob_ref[0].astype(F32)
    tiles = []
    for h in range(HG_HEADS):
        t = o[:, h * LANES:(h + 1) * LANES]
        ms = jnp.mean(t * t, axis=-1, keepdims=True)
        tiles.append(t * lax.rsqrt(ms + EPS) * hgn_ref[...])
    hg = hg_ref[...].astype(F32)
    a = (jnp.concatenate(tiles, axis=1) * (hg * _sigmoid(hg))).astype(BF16)
    merged = (_sigmoid(ga_ref[...].astype(F32)) * jnp.dot(a, wa_ref[...], preferred_element_type=F32)
              + _sigmoid(gb_ref[...].astype(F32)) * jnp.dot(b_ref[...], wb_ref[...], preferred_element_type=F32)
              + _sigmoid(gc_ref[...].astype(F32)) * jnp.dot(c_ref[...], wc_ref[...], preferred_element_type=F32))
    y = jnp.dot(merged.astype(BF16), wo_ref[...], preferred_element_type=F32)
    x = x_ref[...] + gate1_ref[0] * y
    xo_ref[...] = x
    ms = jnp.mean(x * x, axis=-1, keepdims=True)
    h2 = (x * lax.rsqrt(ms + EPS) * n2_ref[...]) * (1.0 + sc2_ref[0]) + sh2_ref[0]
    h2_ref[...] = h2.astype(BF16)
    lg_ref[...] = jnp.dot(h2, wr_ref[...], preferred_element_type=F32, precision=lax.Precision.HIGHEST)


def merge(o_fb, p2d, b2d, c2d, x2d, rows_per_batch, wa, wb, wc, wo, hgn, gate1, n2, sc2, sh2, w_route, tm):
    t, d = x2d.shape
    tm = min(tm, rows_per_batch)
    per = rows_per_batch // tm
    row = lambda i: (i, 0)
    const = lambda i: (0, 0)
    bat = lambda i: (i // per, 0, 0)
    gcol = COL["gates"] // d
    return pl.pallas_call(
        _merge_kernel,
        out_shape=(jax.ShapeDtypeStruct((t, d), F32), jax.ShapeDtypeStruct((t, d), BF16),
                   jax.ShapeDtypeStruct((t, LANES), F32)),
        grid=(t // tm,),
        in_specs=[
            pl.BlockSpec((1, tm, HG_W), lambda i: (0, i, 0)),
            pl.BlockSpec((1, tm, HG_W), lambda i: (1, i, 0)),
            pl.BlockSpec((tm, HG_W), lambda i: (i, COL["hg"] // HG_W)),
            pl.BlockSpec((tm, ATT_W), row),
            pl.BlockSpec((tm, ATT_W), row),
            pl.BlockSpec((tm, d), lambda i: (i, gcol)),
            pl.BlockSpec((tm, d), lambda i: (i, gcol + 1)),
            pl.BlockSpec((tm, d), lambda i: (i, gcol + 2)),
            pl.BlockSpec((tm, d), row),
            pl.BlockSpec((HG_W, d), const),
            pl.BlockSpec((ATT_W, d), const),
            pl.BlockSpec((ATT_W, d), const),
            pl.BlockSpec((d, d), const),
            pl.BlockSpec((1, LANES), const),
            pl.BlockSpec((1, 1, d), bat),
            pl.BlockSpec((1, d), const),
            pl.BlockSpec((1, 1, d), bat),
            pl.BlockSpec((1, 1, d), bat),
            pl.BlockSpec((d, LANES), const),
        ],
        out_specs=(pl.BlockSpec((tm, d), row), pl.BlockSpec((tm, d), row), pl.BlockSpec((tm, LANES), row)),
        compiler_params=_cparams(("parallel",)),
    )(o_fb, o_fb, p2d, b2d, c2d, p2d, p2d, p2d, x2d, wa, wb, wc, wo, hgn.reshape(1, LANES),
      gate1, n2.reshape(1, d), sc2, sh2, w_route)


_STRICT_LOWER = np.tril(np.ones((ROUTE_ROWS, ROUTE_ROWS), np.float32), -1)
_NEG_BIG = -3.0e38


def _lane_argmax(x, lane):
    top = jnp.max(x, axis=1, keepdims=True)
    idx = jnp.min(jnp.where(x == top, lane, LANES), axis=1, keepdims=True)
    return top, idx


def _route_kernel(lg_ref, tri_ref, o_ref, cnt_ref, run_scr):
    @pl.when(pl.program_id(0) == 0)
    def _():
        run_scr[...] = jnp.zeros_like(run_scr)

    lg = lg_ref[...]
    lane = lax.broadcasted_iota(jnp.int32, lg.shape, 1)
    is_grp = lane < N_GROUPS
    gtop, gidx = _lane_argmax(jnp.where(is_grp, lg, _NEG_BIG), lane)
    grp_w = 1.0 / jnp.sum(jnp.where(is_grp, jnp.exp(lg - gtop), 0.0), axis=1, keepdims=True)
    lo = N_GROUPS + EXP_PER_GROUP * gidx
    x1 = jnp.where((lane >= lo) & (lane < lo + EXP_PER_GROUP), lg, _NEG_BIG)
    t1, i1 = _lane_argmax(x1, lane)
    t2, i2 = _lane_argmax(jnp.where(lane == i1, _NEG_BIG, x1), lane)
    r = jnp.exp(t2 - t1)
    w1 = grp_w / (1.0 + r)
    w2 = w1 * r
    e1 = i1 - N_GROUPS
    e2 = i2 - N_GROUPS
    hot1 = lane == e1
    hot2 = lane == e2
    hot = jnp.where(hot1 | hot2, 1.0, 0.0)
    before = run_scr[...] + jnp.dot(tri_ref[...], hot.astype(BF16), preferred_element_type=F32)
    rank1 = jnp.sum(jnp.where(hot1, before, 0.0), axis=1, keepdims=True)
    rank2 = jnp.sum(jnp.where(hot2, before, 0.0), axis=1, keepdims=True)
    run_scr[...] = run_scr[...] + jnp.sum(hot, axis=0, keepdims=True)
    cnt_ref[...] = jnp.broadcast_to(run_scr[...], cnt_ref.shape)
    out = jnp.where(lane == 0, e1.astype(F32), jnp.where(lane == 1, e2.astype(F32), 0.0))
    out = jnp.where(lane == 2, w1, jnp.where(lane == 3, w2, out))
    o_ref[...] = jnp.where(lane == 4, rank1, jnp.where(lane == 5, rank2, out))


def route(logits):
    t = logits.shape[0]
    return pl.pallas_call(
        _route_kernel,
        out_shape=(jax.ShapeDtypeStruct((t, LANES), F32), jax.ShapeDtypeStruct((8, LANES), F32)),
        grid=(t // ROUTE_ROWS,),
        in_specs=[pl.BlockSpec((ROUTE_ROWS, LANES), lambda i: (i, 0)),
                  pl.BlockSpec((ROUTE_ROWS, ROUTE_ROWS), lambda i: (0, 0))],
        out_specs=(pl.BlockSpec((ROUTE_ROWS, LANES), lambda i: (i, 0)),
                   pl.BlockSpec((8, LANES), lambda i: (0, 0))),
        scratch_shapes=[pltpu.VMEM((1, LANES), F32)],
        compiler_params=_cparams(("arbitrary",)),
    )(logits, jnp.asarray(_STRICT_LOWER, BF16))


def _expert_kernel(be_ref, x_ref, wg_ref, wu_ref, wd_ref, o_ref, wg_s, wu_s, wd_s):
    i = pl.program_id(0)

    @pl.when((i == 0) | (be_ref[i] != be_ref[jnp.maximum(i - 1, 0)]))
    def _():
        wg_s[...] = wg_ref[0].astype(BF16)
        wu_s[...] = wu_ref[0].astype(BF16)
        wd_s[...] = wd_ref[0].astype(BF16)

    x = x_ref[...]
    gte = jnp.dot(x, wg_s[...], preferred_element_type=F32)
    up = jnp.dot(x, wu_s[...], preferred_element_type=F32)
    hid = (gte * _sigmoid(gte) * up).astype(BF16)
    o_ref[...] = jnp.dot(hid, wd_s[...], preferred_element_type=F32).astype(o_ref.dtype)


def expert_ffn(rows, block_expert, wg, wu, wd):
    n, d = rows.shape
    de = wg.shape[2]
    return pl.pallas_call(
        _expert_kernel,
        out_shape=jax.ShapeDtypeStruct((n, d), BF16),
        grid_spec=pltpu.PrefetchScalarGridSpec(
            num_scalar_prefetch=1,
            grid=(n // MOE_ROWS,),
            in_specs=[
                pl.BlockSpec((MOE_ROWS, d), lambda i, be: (i, 0)),
                pl.BlockSpec((1, d, de), lambda i, be: (be[i], 0, 0)),
                pl.BlockSpec((1, d, de), lambda i, be: (be[i], 0, 0)),
                pl.BlockSpec((1, de, d), lambda i, be: (be[i], 0, 0)),
            ],
            out_specs=pl.BlockSpec((MOE_ROWS, d), lambda i, be: (i, 0)),
            scratch_shapes=[pltpu.VMEM((d, de), BF16), pltpu.VMEM((d, de), BF16), pltpu.VMEM((de, d), BF16)],
        ),
        compiler_params=_cparams(("arbitrary",)),
    )(block_expert, rows, wg, wu, wd)


def _combine_kernel(x_ref, r1_ref, r2_ref, info_ref, gate_ref, *rest, final):
    o_ref = rest[-1]
    info = info_ref[...]
    y = info[:, 2:3] * r1_ref[...].astype(F32) + info[:, 3:4] * r2_ref[...].astype(F32)
    x = x_ref[...] + gate_ref[0] * y
    if final:
        ms = jnp.mean(x * x, axis=-1, keepdims=True)
        x = x * lax.rsqrt(ms + EPS) * rest[0][...]
    o_ref[...] = x


def combine(x2d, r1, r2, info, row_off, gate, rows_per_batch, final_g=None):
    t, d = x2d.shape
    tm = min(ROUTE_ROWS, rows_per_batch)
    per = rows_per_batch // tm
    off = row_off // tm
    row = lambda i: (i, 0)
    shifted = lambda i: (i + off, 0)
    specs = [pl.BlockSpec((tm, d), row), pl.BlockSpec((tm, d), shifted), pl.BlockSpec((tm, d), shifted),
             pl.BlockSpec((tm, LANES), shifted), pl.BlockSpec((1, 1, d), lambda i: (i // per, 0, 0))]
    args = [x2d, r1, r2, info, gate]
    if final_g is not None:
        specs.append(pl.BlockSpec((1, d), lambda i: (0, 0)))
        args.append(final_g.reshape(1, d))
    return pl.pallas_call(
        functools.partial(_combine_kernel, final=final_g is not None),
        out_shape=jax.ShapeDtypeStruct((t, d), F32), grid=(t // tm,),
        in_specs=specs, out_specs=pl.BlockSpec((tm, d), row),
        compiler_params=_cparams(("parallel",)),
    )(*args)


def _dispatch(info, counts_row):
    n_tok = info.shape[0]
    expert = info[:, 0:TOP_K].astype(jnp.int32)
    rank = info[:, 4:4 + TOP_K].astype(jnp.int32)
    counts = counts_row[0, :N_EXPERTS].astype(jnp.int32)
    padded = (counts + MOE_ROWS - 1) // MOE_ROWS * MOE_ROWS
    pad_end = jnp.cumsum(padded)
    pad_start = pad_end - padded
    hot = expert[:, :, None] == jnp.arange(N_EXPERTS, dtype=jnp.int32)
    dest = jnp.sum(jnp.where(hot, pad_start, 0), axis=-1) + rank
    n_blocks = -(-(n_tok * TOP_K + N_EXPERTS * (MOE_ROWS - 1)) // MOE_ROWS)
    token = jnp.broadcast_to(jnp.arange(n_tok, dtype=jnp.int32)[:, None], dest.shape)
    src_tok = jnp.zeros((n_blocks * MOE_ROWS,), jnp.int32).at[dest.reshape(-1)].set(token.reshape(-1))
    block_start = jnp.arange(n_blocks, dtype=jnp.int32)[:, None] * MOE_ROWS
    block_expert = jnp.minimum(jnp.sum((pad_end[None, :] <= block_start).astype(jnp.int32), axis=1),
                               N_EXPERTS - 1)
    return src_tok, dest, block_expert


def _rope_tables(seq):
    n_rows = seq // GRID_W
    row = jnp.repeat(jnp.arange(n_rows), GRID_W).astype(F32)
    col = jnp.tile(jnp.arange(GRID_W), n_rows).astype(F32)
    axis_pairs = HEAD_DIM // 4
    inv = ROPE_THETA ** (-jnp.arange(axis_pairs, dtype=F32) / axis_pairs)
    ang = jnp.concatenate([row[:, None] * inv, col[:, None] * inv], axis=-1)
    cos, sin = jnp.cos(ang), jnp.sin(ang)
    cos128 = jnp.concatenate([cos, cos, cos, cos], axis=-1)
    sins128 = jnp.concatenate([-sin, sin, -sin, sin], axis=-1)
    return cos128, sins128


def kernel(x, c, ctx, c_ctx, w_mod, b_mod, norm1_g, norm2_g, w_in, hgrn_lb_logits, hgrn_out_norm_g,
           attn_q_norm_g, attn_k_norm_g, swa_sink, w_branch_a, w_branch_b, w_branch_c, w_out,
           w_group, w_router, w_exp_gate, w_exp_up, w_exp_down, final_norm_g):
    bsz, seq, d = x.shape
    n_ctx = ctx.shape[1]
    depth = w_mod.shape[0]
    cos, sins = _rope_tables(seq)
    lb_p = jax.nn.softmax(hgrn_lb_logits.astype(F32), axis=0)
    lower_bounds = jnp.cumsum(lb_p, axis=0) - lb_p[0]
    no_sink = jnp.zeros((N_HEADS,), F32)
    ones128 = jnp.ones((1, LANES), F32)

    xl = x.reshape(bsz * seq, d)
    xc = ctx.reshape(bsz * n_ctx, d)
    for layer in range(depth):
        ctx_out = layer < depth - 1
        mod_l = jax.nn.silu(c) @ w_mod[layer] + b_mod[layer]
        mod_c = jnp.broadcast_to(jax.nn.silu(c_ctx) @ w_mod[layer] + b_mod[layer], (bsz, 6 * d))
        ml = [m.reshape(bsz, 1, d) for m in jnp.split(mod_l, 6, axis=-1)]
        mc = [m.reshape(bsz, 1, d) for m in jnp.split(mod_c, 6, axis=-1)]

        w_in_b = w_in[layer].astype(BF16)
        p_l = inproj(xl, seq, norm1_g[layer], ml[1], ml[0], w_in_b, 1024, 1024)
        p_c = inproj(xc, n_ctx, norm1_g[layer], mc[1], mc[0], w_in_b, 256, 1024)
        p_l3 = p_l.reshape(bsz, seq, D_IN)
        p_c3 = p_c.reshape(bsz, n_ctx, D_IN)

        s0 = jnp.zeros((bsz, 2, HG_HEADS, HG_KDIM, HG_KDIM), F32)
        o_c, s_c = hgrn_scan(p_c3, lower_bounds[layer], s0, 256)
        o_l, _ = hgrn_scan(p_l3, lower_bounds[layer], s_c, 512)

        gq = jnp.tile(attn_q_norm_g[layer], 2).reshape(1, LANES)
        gk = jnp.tile(attn_k_norm_g[layer], 2).reshape(1, LANES)
        kt_l, v_l, kn_l = prep_kv(p_l3, COL["ak"], COL["av"], gk, cos, sins, True, True, 512)
        kt_c, v_c, kn_c = prep_kv(p_c3, COL["ak"], COL["av"], gk, cos, sins, True, False, 256)
        kmax = _key_norm_max([kn_l, kn_c])
        q_l, u_l = prep_q(p_l3, COL["aq"], gq, cos, sins, kmax, True, True, 512)
        kt_all = jnp.concatenate([kt_c, kt_l], axis=3)
        v_all = jnp.concatenate([v_c, v_l], axis=2)
        b_l = attention(q_l, kt_all, v_all, jnp.max(u_l), no_sink, 256, 1280)

        skt_l, sv_l, skn_l = prep_kv(p_l3, COL["sk"], COL["sv"], ones128, cos, sins, False, True, 512)
        skt_c, sv_c, skn_c = prep_kv(p_c3, COL["sk"], COL["sv"], ones128, cos, sins, False, False, 256)
        skmax = _key_norm_max([skn_l, skn_c])
        sq_l, su_l = prep_q(p_l3, COL["sq"], ones128, cos, sins, skmax, False, True, 512)
        zk = jnp.zeros((bsz, N_KV, LANES, WINDOW), BF16)
        zv = jnp.zeros((bsz, N_KV, WINDOW, LANES), BF16)
        skt_all = jnp.concatenate([skt_c, zk, skt_l, zk], axis=3)
        sv_all = jnp.concatenate([sv_c, zv, sv_l, zv], axis=2)
        c_l = attention(sq_l, skt_all, sv_all, jnp.max(su_l), swa_sink[layer], 256, 256,
                        use_sink=True, band=n_ctx)

        wa = w_branch_a[layer].astype(BF16)
        wb = w_branch_b[layer].astype(BF16)
        wc = w_branch_c[layer].astype(BF16)
        wo = w_out[layer].astype(BF16)
        w_route = jnp.concatenate(
            [w_group[layer], w_router[layer], jnp.zeros((d, LANES - N_GROUPS - N_EXPERTS), F32)], axis=1)
        xl, h_l, lg_l = merge(o_l.reshape(2, bsz * seq, HG_W), p_l, b_l.reshape(-1, ATT_W),
                              c_l.reshape(-1, ATT_W), xl, seq, wa, wb, wc, wo, hgrn_out_norm_g[layer],
                              ml[2], norm2_g[layer], ml[4], ml[3], w_route, 256)
        if ctx_out:
            q_c, u_c = prep_q(p_c3, COL["aq"], gq, cos, sins, kmax, True, False, 256)
            b_c = attention(q_c, kt_c, v_c, jnp.max(u_c), no_sink, 256, 256)
            sq_c, su_c = prep_q(p_c3, COL["sq"], ones128, cos, sins, skmax, False, False, 256)
            c_c = attention(sq_c, skt_c, sv_c, jnp.max(su_c), swa_sink[layer], 256, 256, use_sink=True)
            xc, h_c, lg_c = merge(o_c.reshape(2, bsz * n_ctx, HG_W), p_c, b_c.reshape(-1, ATT_W),
                                  c_c.reshape(-1, ATT_W), xc, n_ctx, wa, wb, wc, wo, hgrn_out_norm_g[layer],
                                  mc[2], norm2_g[layer], mc[4], mc[3], w_route, 256)
            h_all = jnp.concatenate([h_c, h_l], axis=0)
            lg_all = jnp.concatenate([lg_c, lg_l], axis=0)
        else:
            h_all, lg_all = h_l, lg_l

        info, counts_row = route(lg_all)
        src_tok, dest, block_expert = _dispatch(info, counts_row)
        rows = jnp.take(h_all, src_tok, axis=0)
        out_rows = expert_ffn(rows, block_expert, w_exp_gate[layer], w_exp_up[layer], w_exp_down[layer])
        r1 = jnp.take(out_rows, dest[:, 0], axis=0)
        r2 = jnp.take(out_rows, dest[:, 1], axis=0)
        n_c = bsz * n_ctx if ctx_out else 0
        if ctx_out:
            xc = combine(xc, r1, r2, info, 0, mc[5], n_ctx)
        xl = combine(xl, r1, r2, info, n_c, ml[5], seq, final_g=final_norm_g if layer == depth - 1 else None)
    return xl.reshape(bsz, seq, d)
```

```python
import functools
import math

import numpy as np
import jax
import jax.numpy as jnp
from jax import lax
from jax.experimental import pallas as pl
from jax.experimental.pallas import tpu as pltpu

F32 = jnp.float32
BF16 = jnp.bfloat16

EPS = 1e-6
MASK_VALUE = -1e30
TINY = 1e-30
GRID_W = 64
ROPE_THETA = 10000.0

HG_HEADS = 4
HG_KDIM = 128
HG_W = HG_HEADS * HG_KDIM
HEAD_DIM = 64
N_HEADS = 8
N_KV = 2
GROUP = N_HEADS // N_KV
ATT_W = N_HEADS * HEAD_DIM
KV_W = N_KV * HEAD_DIM
WINDOW = 128
ATTN_SCALE = HEAD_DIM ** -0.5
N_GROUPS = 4
EXP_PER_GROUP = 8
N_EXPERTS = N_GROUPS * EXP_PER_GROUP
TOP_K = 2

LANES = 128
HG_CHUNK = 128
HG_LEVELS = int(math.log2(HG_CHUNK))
HG_TOTAL_ROWS = 16
MOE_ROWS = 256
ROUTE_ROWS = 512
VMEM_LIMIT = 56 * 1024 * 1024

LOG2E = 1.4426950408889634
SHIFT_HEADROOM = 57.0
SAFE_BOUND = 90.0


def _cparams(sem):
    return pltpu.CompilerParams(dimension_semantics=sem, vmem_limit_bytes=VMEM_LIMIT)


_SIZES = (HG_W, HG_W, HG_W, HG_W, HG_W, ATT_W, KV_W, KV_W, ATT_W, KV_W, KV_W, 3 * 1024)
_NAMES = ("hq", "ff", "fb", "hi", "hg", "aq", "ak", "av", "sq", "sk", "sv", "gates")
COL = {n: int(sum(_SIZES[:i])) for i, n in enumerate(_NAMES)}
D_IN = int(sum(_SIZES))


def _inproj_kernel(x_ref, g_ref, sc_ref, sh_ref, w_ref, o_ref, h_scr):
    @pl.when(pl.program_id(1) == 0)
    def _():
        x = x_ref[...]
        ms = jnp.mean(x * x, axis=-1, keepdims=True)
        y = x * lax.rsqrt(ms + EPS) * g_ref[...]
        h_scr[...] = (y * (1.0 + sc_ref[0]) + sh_ref[0]).astype(BF16)

    o_ref[...] = jnp.dot(h_scr[...], w_ref[...], preferred_element_type=F32).astype(o_ref.dtype)


def inproj(x2d, rows_per_batch, g, scale, shift, w_bf16, tm, tn):
    t, d = x2d.shape
    n = w_bf16.shape[1]
    tm = min(tm, rows_per_batch)
    per = rows_per_batch // tm
    return pl.pallas_call(
        _inproj_kernel,
        out_shape=jax.ShapeDtypeStruct((t, n), BF16),
        grid=(t // tm, n // tn),
        in_specs=[
            pl.BlockSpec((tm, d), lambda i, j: (i, 0)),
            pl.BlockSpec((1, d), lambda i, j: (0, 0)),
            pl.BlockSpec((1, 1, d), lambda i, j: (i // per, 0, 0)),
            pl.BlockSpec((1, 1, d), lambda i, j: (i // per, 0, 0)),
            pl.BlockSpec((d, tn), lambda i, j: (0, j)),
        ],
        out_specs=pl.BlockSpec((tm, tn), lambda i, j: (i, j)),
        scratch_shapes=[pltpu.VMEM((tm, d), BF16)],
        compiler_params=_cparams(("parallel", "arbitrary")),
    )(x2d, g.reshape(1, d), scale, shift, w_bf16)


def _hgrn_exponent_matrices():
    c, nl = HG_CHUNK, HG_LEVELS
    rows = (nl + 2) * c + HG_TOTAL_ROWS
    out = np.zeros((2, rows, c), np.float32)
    for d in range(2):
        pos = np.arange(c) if d == 0 else c - 1 - np.arange(c)
        for l in range(nl):
            m = 1 << l
            for t in range(c):
                p = pos[t]
                mid = (p // (2 * m)) * 2 * m + m
                if p >= mid:
                    sel = (pos >= mid) & (pos <= p)
                else:
                    sel = (pos > p) & (pos <= mid - 1)
                out[d, l * c + t, sel] = 1.0
        for t in range(c):
            out[d, nl * c + t, pos <= pos[t]] = 1.0
            out[d, (nl + 1) * c + t, pos > pos[t]] = 1.0
        out[d, (nl + 2) * c:, :] = 1.0
    return np.concatenate([out, out], axis=2)


_HG_EXP_MATS = _hgrn_exponent_matrices()


def _sigmoid(x):
    return 1.0 / (1.0 + jnp.exp(-x))


def _hgrn_kernel(q_ref, f_ref, i_ref, lb_ref, mat_ref, s0_ref, o_ref, st_ref, *, n_chunks):
    c, nl = HG_CHUNK, HG_LEVELS
    d = pl.program_id(1)

    @pl.when(pl.program_id(2) == 0)
    def _():
        st_ref[...] = s0_ref[...]

    row = lax.broadcasted_iota(jnp.int32, (c, HG_W), 0)
    pos = row + d * (c - 1 - 2 * row)
    r_i = lax.broadcasted_iota(jnp.int32, (c, c), 0)
    c_i = lax.broadcasted_iota(jnp.int32, (c, c), 1)
    xor_rc = r_i ^ c_i
    top_bit = sum((xor_rc >= (1 << j)).astype(jnp.int32) for j in range(1, nl))
    pair_level = jnp.where((r_i - c_i) * (1 - 2 * d) > 0, top_bit, -1)

    def chunk(ci, carry):
        cc = ci + d * (n_chunks - 1 - 2 * ci)
        r0 = pl.multiple_of(cc * c, c)
        qraw = q_ref[0, pl.ds(r0, c), :].astype(F32)
        fz = f_ref[0, pl.ds(r0, c), :].astype(F32)
        vb = i_ref[0, pl.ds(r0, c), :]
        v = vb.astype(F32)
        lb = lb_ref[...]
        sig = _sigmoid(fz)
        logf = jnp.log(jnp.maximum(lb + (1.0 - lb) * sig, TINY))
        key = (1.0 - lb) * (1.0 - sig)
        qh = qraw * _sigmoid(qraw) * (HG_KDIM ** -0.5)
        logf2 = logf * LOG2E
        hi = logf2.astype(BF16)
        lo = (logf2 - hi.astype(F32)).astype(BF16)
        hilo = jnp.concatenate([hi, lo], axis=0)

        def decay(r_lo, r_hi):
            return jnp.exp2(jnp.dot(mat_ref[0, r_lo:r_hi, :], hilo, preferred_element_type=F32))

        a = [jnp.zeros((c, c), F32)] * HG_HEADS
        for l in range(nl):
            x = (jnp.where(((pos >> l) & 1) == 1, qh, key) * decay(l * c, (l + 1) * c)).astype(BF16)
            for h in range(HG_HEADS):
                xh = x[:, h * LANES:(h + 1) * LANES]
                prod = lax.dot_general(xh, xh, (((1,), (1,)), ((), ())), preferred_element_type=F32)
                a[h] = jnp.where(pair_level == l, prod, a[h])
        q_in = (qh * decay(nl * c, (nl + 1) * c)).astype(BF16)
        k_out = (key * decay((nl + 1) * c, (nl + 2) * c)).astype(BF16)
        total = decay((nl + 2) * c, (nl + 2) * c + HG_TOTAL_ROWS)[0:1]
        qk = qh * key
        for h in range(HG_HEADS):
            lanes = slice(h * LANES, (h + 1) * LANES)
            st = st_ref[0, 0, h]
            o = (jnp.dot(a[h].astype(BF16), vb[:, lanes], preferred_element_type=F32)
                 + jnp.sum(qk[:, lanes], axis=1, keepdims=True) * v[:, lanes]
                 + lax.dot_general(q_in[:, lanes], st.astype(BF16), (((1,), (1,)), ((), ())),
                                   preferred_element_type=F32))
            o_ref[0, 0, pl.ds(r0, c), lanes] = o.astype(o_ref.dtype)
            st_ref[0, 0, h] = total[:, lanes] * st + lax.dot_general(
                vb[:, lanes], k_out[:, lanes], (((0,), (0,)), ((), ())), preferred_element_type=F32)
        return carry

    lax.fori_loop(0, n_chunks, chunk, 0)


def hgrn_scan(p3, lb, s0, tb):
    b, l, _ = p3.shape
    tb = min(tb, l)
    nb = l // tb
    blk = lambda d, n: n + d * (nb - 1 - 2 * n)
    wblk = HG_W
    kern = functools.partial(_hgrn_kernel, n_chunks=tb // HG_CHUNK)
    mats = jnp.asarray(_HG_EXP_MATS, BF16)
    return pl.pallas_call(
        kern,
        out_shape=(jax.ShapeDtypeStruct((2, b, l, HG_W), BF16),
                   jax.ShapeDtypeStruct((b, 2, HG_HEADS, HG_KDIM, HG_KDIM), F32)),
        grid=(b, 2, nb),
        in_specs=[
            pl.BlockSpec((1, tb, wblk), lambda bi, d, n: (bi, blk(d, n), COL["hq"] // wblk)),
            pl.BlockSpec((1, tb, wblk), lambda bi, d, n: (bi, blk(d, n), COL["ff"] // wblk + d)),
            pl.BlockSpec((1, tb, wblk), lambda bi, d, n: (bi, blk(d, n), COL["hi"] // wblk)),
            pl.BlockSpec((1, HG_W), lambda bi, d, n: (0, 0)),
            pl.BlockSpec((1,) + _HG_EXP_MATS.shape[1:], lambda bi, d, n: (d, 0, 0)),
            pl.BlockSpec((1, 1, HG_HEADS, HG_KDIM, HG_KDIM), lambda bi, d, n: (bi, d, 0, 0, 0)),
        ],
        out_specs=(
            pl.BlockSpec((1, 1, tb, HG_W), lambda bi, d, n: (d, bi, blk(d, n), 0)),
            pl.BlockSpec((1, 1, HG_HEADS, HG_KDIM, HG_KDIM), lambda bi, d, n: (bi, d, 0, 0, 0)),
        ),
        compiler_params=_cparams(("parallel", "arbitrary", "arbitrary")),
    )(p3, p3, p3, lb.reshape(1, HG_W), mats, s0)


_GROUP_ONES = np.kron(np.eye(LANES // HEAD_DIM, dtype=np.float32), np.ones((HEAD_DIM, HEAD_DIM), np.float32))
_GROUP_ONES2 = np.concatenate([_GROUP_ONES, _GROUP_ONES], axis=0)


def _group_ssq(x, ones2):
    sq = x * x
    hi = sq.astype(BF16)
    lo = (sq - hi.astype(F32)).astype(BF16)
    return jnp.dot(jnp.concatenate([hi, lo], axis=1), ones2, preferred_element_type=F32)


def _prep_tile(x, g, cos, sins, ones2, norm, rope):
    if norm:
        x = x * lax.rsqrt(_group_ssq(x, ones2) * (1.0 / HEAD_DIM) + EPS) * g
    if rope:
        lane = lax.broadcasted_iota(jnp.int32, x.shape, 1)
        first = (lane % HEAD_DIM) < (HEAD_DIM // 2)
        other = jnp.where(first, pltpu.roll(x, LANES - HEAD_DIM // 2, 1), pltpu.roll(x, HEAD_DIM // 2, 1))
        x = x * cos + other * sins
    return x


def _prep_kv_kernel(k_ref, v_ref, g_ref, cos_ref, sin_ref, ones_ref, kt_ref, vo_ref, kn_ref, *, norm, rope):
    k = _prep_tile(k_ref[0].astype(F32), g_ref[...], cos_ref[...], sin_ref[...], ones_ref[...], norm, rope)
    kb = k.astype(BF16).astype(F32)
    ssq = _group_ssq(kb, ones_ref[...])
    kn_ref[0, 0] = jnp.broadcast_to(jnp.max(ssq, axis=0, keepdims=True), (8, LANES))
    kt = kb.T
    row = lax.broadcasted_iota(jnp.int32, kt.shape, 0)
    v = v_ref[0]
    low = lax.broadcasted_iota(jnp.int32, v.shape, 1) < HEAD_DIM
    one = jnp.ones((), v.dtype)
    for g in range(N_KV):
        ktg = kt if g == 0 else jnp.concatenate([kt[HEAD_DIM:], kt[:HEAD_DIM]], axis=0)
        kt_ref[0, g] = jnp.where(row < HEAD_DIM, ktg, jnp.where(row == HEAD_DIM, 1.0, 0.0)).astype(BF16)
        vg = v if g == 0 else jnp.concatenate([v[:, HEAD_DIM:], v[:, :HEAD_DIM]], axis=1)
        vo_ref[0, g] = jnp.where(low, vg, one).astype(BF16)


def _prep_q_kernel(kmax_ref, x0_ref, x1_ref, x2_ref, x3_ref, g_ref, cos_ref, sin_ref, ones_ref, o_ref, u_ref,
                   *, norm, rope):
    b = pl.program_id(0)
    lane = lax.broadcasted_iota(jnp.int32, (x0_ref.shape[1], LANES), 1)
    umax = None
    for tile, x_ref in enumerate((x0_ref, x1_ref, x2_ref, x3_ref)):
        kv = (2 * tile) // GROUP
        x = x_ref[0].astype(F32)
        y = _prep_tile(x, g_ref[...], cos_ref[...], sin_ref[...], ones_ref[...], norm, rope) * (ATTN_SCALE * LOG2E)
        yb = y.astype(BF16).astype(F32)
        u2 = jnp.sqrt(_group_ssq(yb, ones_ref[...])) * kmax_ref[b * N_KV + kv]
        tmax = jnp.max(u2, axis=0, keepdims=True)
        umax = tmax if umax is None else jnp.maximum(umax, tmax)
        neg_shift = SHIFT_HEADROOM - u2
        y_sw = pltpu.roll(yb, HEAD_DIM, 1)
        ns_sw = pltpu.roll(neg_shift, HEAD_DIM, 1)
        for half in range(2):
            h = 2 * tile + half
            data = yb if half == 0 else y_sw
            ns = ns_sw if half == 0 else neg_shift
            o_ref[0, :, h * LANES:(h + 1) * LANES] = jnp.where(
                lane < HEAD_DIM, data, jnp.where(lane == HEAD_DIM, ns, 0.0)).astype(BF16)
    u_ref[0, 0] = jnp.broadcast_to(umax, (8, LANES))


def prep_kv(p3, col_k, col_v, g128, cos, sins, norm, rope, tq):
    b, l, _ = p3.shape
    tq = min(tq, l)
    ones2 = jnp.asarray(_GROUP_ONES2, BF16)
    return pl.pallas_call(
        functools.partial(_prep_kv_kernel, norm=norm, rope=rope),
        out_shape=(jax.ShapeDtypeStruct((b, N_KV, LANES, l), BF16),
                   jax.ShapeDtypeStruct((b, N_KV, l, LANES), BF16),
                   jax.ShapeDtypeStruct((b, l // tq, 8, LANES), F32)),
        grid=(b, l // tq),
        in_specs=[
            pl.BlockSpec((1, tq, KV_W), lambda bi, i: (bi, i, col_k // KV_W)),
            pl.BlockSpec((1, tq, KV_W), lambda bi, i: (bi, i, col_v // KV_W)),
            pl.BlockSpec((1, LANES), lambda bi, i: (0, 0)),
            pl.BlockSpec((tq, LANES), lambda bi, i: (i, 0)),
            pl.BlockSpec((tq, LANES), lambda bi, i: (i, 0)),
            pl.BlockSpec((2 * LANES, LANES), lambda bi, i: (0, 0)),
        ],
        out_specs=(pl.BlockSpec((1, N_KV, LANES, tq), lambda bi, i: (bi, 0, 0, i)),
                   pl.BlockSpec((1, N_KV, tq, LANES), lambda bi, i: (bi, 0, i, 0)),
                   pl.BlockSpec((1, 1, 8, LANES), lambda bi, i: (bi, i, 0, 0))),
        compiler_params=_cparams(("parallel", "parallel")),
    )(p3, p3, g128, cos, sins, ones2)


def prep_q(p3, col, g128, cos, sins, kmax, norm, rope, tq):
    b, l, _ = p3.shape
    tq = min(tq, l)
    ones2 = jnp.asarray(_GROUP_ONES2, BF16)
    return pl.pallas_call(
        functools.partial(_prep_q_kernel, norm=norm, rope=rope),
        out_shape=(jax.ShapeDtypeStruct((b, l, N_HEADS * LANES), BF16),
                   jax.ShapeDtypeStruct((b, l // tq, 8, LANES), F32)),
        grid_spec=pltpu.PrefetchScalarGridSpec(
            num_scalar_prefetch=1,
            grid=(b, l // tq),
            in_specs=[
                pl.BlockSpec((1, tq, LANES), lambda bi, i, s, t=t: (bi, i, col // LANES + t))
                for t in range(ATT_W // LANES)
            ] + [
                pl.BlockSpec((1, LANES), lambda bi, i, s: (0, 0)),
                pl.BlockSpec((tq, LANES), lambda bi, i, s: (i, 0)),
                pl.BlockSpec((tq, LANES), lambda bi, i, s: (i, 0)),
                pl.BlockSpec((2 * LANES, LANES), lambda bi, i, s: (0, 0)),
            ],
            out_specs=(pl.BlockSpec((1, tq, N_HEADS * LANES), lambda bi, i, s: (bi, i, 0)),
                       pl.BlockSpec((1, 1, 8, LANES), lambda bi, i, s: (bi, i, 0, 0))),
        ),
        compiler_params=_cparams(("parallel", "parallel")),
    )(kmax, p3, p3, p3, p3, g128, cos, sins, ones2)


def _key_norm_max(kn_list):
    kn = functools.reduce(jnp.maximum, [jnp.max(k, axis=(1, 2)) for k in kn_list])
    return jnp.sqrt(kn[:, ::HEAD_DIM]).reshape(-1)


def _attn_kernel(sink_ref, q_ref, kt_ref, v_ref, o_ref, *, tk, online, use_sink, band, seq):
    tq = q_ref.shape[1]
    rows = GROUP * tq
    lane = lax.broadcasted_iota(jnp.int32, (tq, LANES), 1)
    low = lane < HEAD_DIM
    if band is not None:
        width = tq + 2 * WINDOW
        start = pl.multiple_of(pl.program_id(1) * tq, tq)
        r = lax.broadcasted_iota(jnp.int32, (rows, width), 0) % tq
        kp = lax.broadcasted_iota(jnp.int32, (rows, width), 1)
        key_pos = start + kp - WINDOW
        mask = (jnp.abs(kp - WINDOW - r) <= WINDOW) & (key_pos >= 0) & (key_pos < seq)
    for g in range(N_KV):
        tiles = [q_ref[0, :, (g * GROUP + hh) * LANES:(g * GROUP + hh + 1) * LANES] for hh in range(GROUP)]
        qg = jnp.concatenate(tiles, axis=0)

        def scores(kt, qg=qg):
            return jnp.dot(qg, kt, preferred_element_type=F32)

        def update(carry, s, v, msk=None):
            if msk is not None:
                s = jnp.where(msk, s, MASK_VALUE)
            if online:
                m, acc = carry
                m_new = jnp.maximum(m, jnp.max(s, axis=1, keepdims=True))
                p = jnp.exp2(s - m_new).astype(BF16)
                return m_new, jnp.exp2(m - m_new) * acc + jnp.dot(p, v, preferred_element_type=F32)
            return carry + jnp.dot(jnp.exp2(s).astype(BF16), v, preferred_element_type=F32)

        acc0 = jnp.zeros((rows, LANES), F32)
        carry = (jnp.full((rows, 1), MASK_VALUE, F32), acc0) if online else acc0
        if band is None:
            def body(j, c, g=g, scores=scores, update=update):
                k0 = pl.multiple_of(j * tk, tk)
                return update(c, scores(kt_ref[0, g, :, pl.ds(k0, tk)]), v_ref[0, g, pl.ds(k0, tk), :])

            carry = lax.fori_loop(0, kt_ref.shape[3] // tk, body, carry)
        else:
            carry = update(carry, scores(kt_ref[0, g, :, 0:band]), v_ref[0, g, 0:band, :])
            b0 = pl.multiple_of(band + start, LANES)
            carry = update(carry, scores(kt_ref[0, g, :, pl.ds(b0, width)]), v_ref[0, g, pl.ds(b0, width), :], mask)
        acc = carry[1] if online else carry
        if use_sink:
            e = jnp.concatenate([t[:, HEAD_DIM:HEAD_DIM + 1].astype(F32) + sink_ref[g * GROUP + hh] * LOG2E
                                 for hh, t in enumerate(tiles)], axis=0)
            if online:
                e = e - carry[0]
            lane_r = lax.broadcasted_iota(jnp.int32, (rows, LANES), 1)
            acc = acc + jnp.where(lane_r >= HEAD_DIM, jnp.exp2(e), 0.0)
        o = acc * pltpu.roll(1.0 / acc, HEAD_DIM, 1)
        for pair in range(GROUP // 2):
            a = o[(2 * pair) * tq:(2 * pair + 1) * tq]
            b = o[(2 * pair + 1) * tq:(2 * pair + 2) * tq]
            t0 = (g * GROUP // 2 + pair) * LANES
            o_ref[0, :, t0:t0 + LANES] = jnp.where(low, a, pltpu.roll(b, HEAD_DIM, 1)).astype(o_ref.dtype)


def _attention(q, kt, v, sink, tq, tk, online, use_sink, band):
    b, l, _ = q.shape
    sk = kt.shape[3]
    tq = min(tq, l)
    tk = min(tk, sk)
    kern = functools.partial(_attn_kernel, tk=tk, online=online, use_sink=use_sink, band=band, seq=l)
    return pl.pallas_call(
        kern,
        out_shape=jax.ShapeDtypeStruct((b, l, ATT_W), BF16),
        grid_spec=pltpu.PrefetchScalarGridSpec(
            num_scalar_prefetch=1,
            grid=(b, l // tq),
            in_specs=[
                pl.BlockSpec((1, tq, N_HEADS * LANES), lambda bi, i, s: (bi, i, 0)),
                pl.BlockSpec((1, N_KV, LANES, sk), lambda bi, i, s: (bi, 0, 0, 0)),
                pl.BlockSpec((1, N_KV, sk, LANES), lambda bi, i, s: (bi, 0, 0, 0)),
            ],
            out_specs=pl.BlockSpec((1, tq, ATT_W), lambda bi, i, s: (bi, i, 0)),
        ),
        compiler_params=_cparams(("parallel", "arbitrary")),
    )(sink, q, kt, v)


def attention(q, kt, v, ubound, sink, tq, tk, use_sink=False, band=None):
    return lax.cond(ubound <= SAFE_BOUND,
                    lambda: _attention(q, kt, v, sink, tq, tk, False, use_sink, band),
                    lambda: _attention(q, kt, v, sink, tq, tk, True, use_sink, band))


def _merge_kernel(of_ref, ob_ref, hg_ref, b_ref, c_ref, ga_ref, gb_ref, gc_ref, x_ref,
                  wa_ref, wb_ref, wc_ref, wo_ref, hgn_ref, gate1_ref, n2_ref, sc2_ref, sh2_ref, wr_ref,
                  xo_ref, h2_ref, lg_ref):
    o = of_ref[0].astype(F32) + ob_ref[0].astype(F32)
    tiles = []
    for h in range(HG_HEADS):
        t = o[:, h * LANES:(h + 1) * LANES]
        ms = jnp.mean(t * t, axis=-1, keepdims=True)
        tiles.append(t * lax.rsqrt(ms + EPS) * hgn_ref[...])
    hg = hg_ref[...].astype(F32)
    a = (jnp.concatenate(tiles, axis=1) * (hg * _sigmoid(hg))).astype(BF16)
    merged = (_sigmoid(ga_ref[...].astype(F32)) * jnp.dot(a, wa_ref[...], preferred_element_type=F32)
              + _sigmoid(gb_ref[...].astype(F32)) * jnp.dot(b_ref[...], wb_ref[...], preferred_element_type=F32)
              + _sigmoid(gc_ref[...].astype(F32)) * jnp.dot(c_ref[...], wc_ref[...], preferred_element_type=F32))
    y = jnp.dot(merged.astype(BF16), wo_ref[...], preferred_element_type=F32)
    x = x_ref[...] + gate1_ref[0] * y
    xo_ref[...] = x
    ms = jnp.mean(x * x, axis=-1, keepdims=True)
    h2 = (x * lax.rsqrt(ms + EPS) * n2_ref[...]) * (1.0 + sc2_ref[0]) + sh2_ref[0]
    h2_ref[...] = h2.astype(BF16)
    lg_ref[...] = jnp.dot(h2, wr_ref[...], preferred_element_type=F32, precision=lax.Precision.HIGHEST)


def merge(o_fb, p2d, b2d, c2d, x2d, rows_per_batch, wa, wb, wc, wo, hgn, gate1, n2, sc2, sh2, w_route, tm):
    t, d = x2d.shape
    tm = min(tm, rows_per_batch)
    per = rows_per_batch // tm
    row = lambda i: (i, 0)
    const = lambda i: (0, 0)
    bat = lambda i: (i // per, 0, 0)
    gcol = COL["gates"] // d
    return pl.pallas_call(
        _merge_kernel,
        out_shape=(jax.ShapeDtypeStruct((t, d), F32), jax.ShapeDtypeStruct((t, d), BF16),
                   jax.ShapeDtypeStruct((t, LANES), F32)),
        grid=(t // tm,),
        in_specs=[
            pl.BlockSpec((1, tm, HG_W), lambda i: (0, i, 0)),
            pl.BlockSpec((1, tm, HG_W), lambda i: (1, i, 0)),
            pl.BlockSpec((tm, HG_W), lambda i: (i, COL["hg"] // HG_W)),
            pl.BlockSpec((tm, ATT_W), row),
            pl.BlockSpec((tm, ATT_W), row),
            pl.BlockSpec((tm, d), lambda i: (i, gcol)),
            pl.BlockSpec((tm, d), lambda i: (i, gcol + 1)),
            pl.BlockSpec((tm, d), lambda i: (i, gcol + 2)),
            pl.BlockSpec((tm, d), row),
            pl.BlockSpec((HG_W, d), const),
            pl.BlockSpec((ATT_W, d), const),
            pl.BlockSpec((ATT_W, d), const),
            pl.BlockSpec((d, d), const),
            pl.BlockSpec((1, LANES), const),
            pl.BlockSpec((1, 1, d), bat),
            pl.BlockSpec((1, d), const),
            pl.BlockSpec((1, 1, d), bat),
            pl.BlockSpec((1, 1, d), bat),
            pl.BlockSpec((d, LANES), const),
        ],
        out_specs=(pl.BlockSpec((tm, d), row), pl.BlockSpec((tm, d), row), pl.BlockSpec((tm, LANES), row)),
        compiler_params=_cparams(("parallel",)),
    )(o_fb, o_fb, p2d, b2d, c2d, p2d, p2d, p2d, x2d, wa, wb, wc, wo, hgn.reshape(1, LANES),
      gate1, n2.reshape(1, d), sc2, sh2, w_route)


_STRICT_LOWER = np.tril(np.ones((ROUTE_ROWS, ROUTE_ROWS), np.float32), -1)
_NEG_BIG = -3.0e38


def _lane_argmax(x, lane):
    top = jnp.max(x, axis=1, keepdims=True)
    idx = jnp.min(jnp.where(x == top, lane, LANES), axis=1, keepdims=True)
    return top, idx


def _route_kernel(lg_ref, tri_ref, o_ref, cnt_ref, run_scr):
    @pl.when(pl.program_id(0) == 0)
    def _():
        run_scr[...] = jnp.zeros_like(run_scr)

    lg = lg_ref[...]
    lane = lax.broadcasted_iota(jnp.int32, lg.shape, 1)
    is_grp = lane < N_GROUPS
    gtop, gidx = _lane_argmax(jnp.where(is_grp, lg, _NEG_BIG), lane)
    grp_w = 1.0 / jnp.sum(jnp.where(is_grp, jnp.exp(lg - gtop), 0.0), axis=1, keepdims=True)
    lo = N_GROUPS + EXP_PER_GROUP * gidx
    x1 = jnp.where((lane >= lo) & (lane < lo + EXP_PER_GROUP), lg, _NEG_BIG)
    t1, i1 = _lane_argmax(x1, lane)
    t2, i2 = _lane_argmax(jnp.where(lane == i1, _NEG_BIG, x1), lane)
    r = jnp.exp(t2 - t1)
    w1 = grp_w / (1.0 + r)
    w2 = w1 * r
    e1 = i1 - N_GROUPS
    e2 = i2 - N_GROUPS
    hot1 = lane == e1
    hot2 = lane == e2
    hot = jnp.where(hot1 | hot2, 1.0, 0.0)
    before = run_scr[...] + jnp.dot(tri_ref[...], hot.astype(BF16), preferred_element_type=F32)
    rank1 = jnp.sum(jnp.where(hot1, before, 0.0), axis=1, keepdims=True)
    rank2 = jnp.sum(jnp.where(hot2, before, 0.0), axis=1, keepdims=True)
    run_scr[...] = run_scr[...] + jnp.sum(hot, axis=0, keepdims=True)
    cnt_ref[...] = jnp.broadcast_to(run_scr[...], cnt_ref.shape)
    out = jnp.where(lane == 0, e1.astype(F32), jnp.where(lane == 1, e2.astype(F32), 0.0))
    out = jnp.where(lane == 2, w1, jnp.where(lane == 3, w2, out))
    o_ref[...] = jnp.where(lane == 4, rank1, jnp.where(lane == 5, rank2, out))


def route(logits):
    t = logits.shape[0]
    return pl.pallas_call(
        _route_kernel,
        out_shape=(jax.ShapeDtypeStruct((t, LANES), F32), jax.ShapeDtypeStruct((8, LANES), F32)),
        grid=(t // ROUTE_ROWS,),
        in_specs=[pl.BlockSpec((ROUTE_ROWS, LANES), lambda i: (i, 0)),
                  pl.BlockSpec((ROUTE_ROWS, ROUTE_ROWS), lambda i: (0, 0))],
        out_specs=(pl.BlockSpec((ROUTE_ROWS, LANES), lambda i: (i, 0)),
                   pl.BlockSpec((8, LANES), lambda i: (0, 0))),
        scratch_shapes=[pltpu.VMEM((1, LANES), F32)],
        compiler_params=_cparams(("arbitrary",)),
    )(logits, jnp.asarray(_STRICT_LOWER, BF16))


def _expert_kernel(be_ref, x_ref, wg_ref, wu_ref, wd_ref, o_ref, wg_s, wu_s, wd_s):
    i = pl.program_id(0)

    @pl.when((i == 0) | (be_ref[i] != be_ref[jnp.maximum(i - 1, 0)]))
    def _():
        wg_s[...] = wg_ref[0, 0].astype(BF16)
        wu_s[...] = wu_ref[0, 0].astype(BF16)
        wd_s[...] = wd_ref[0, 0].astype(BF16)

    x = x_ref[...]
    gte = jnp.dot(x, wg_s[...], preferred_element_type=F32)
    up = jnp.dot(x, wu_s[...], preferred_element_type=F32)
    hid = (gte * _sigmoid(gte) * up).astype(BF16)
    o_ref[...] = jnp.dot(hid, wd_s[...], preferred_element_type=F32).astype(o_ref.dtype)


def expert_ffn(rows, block_expert, wg, wu, wd, layer):
    n, d = rows.shape
    de = wg.shape[3]
    return pl.pallas_call(
        _expert_kernel,
        out_shape=jax.ShapeDtypeStruct((n, d), BF16),
        grid_spec=pltpu.PrefetchScalarGridSpec(
            num_scalar_prefetch=1,
            grid=(n // MOE_ROWS,),
            in_specs=[
                pl.BlockSpec((MOE_ROWS, d), lambda i, be: (i, 0)),
                pl.BlockSpec((1, 1, d, de), lambda i, be: (layer, be[i], 0, 0)),
                pl.BlockSpec((1, 1, d, de), lambda i, be: (layer, be[i], 0, 0)),
                pl.BlockSpec((1, 1, de, d), lambda i, be: (layer, be[i], 0, 0)),
            ],
            out_specs=pl.BlockSpec((MOE_ROWS, d), lambda i, be: (i, 0)),
            scratch_shapes=[pltpu.VMEM((d, de), BF16), pltpu.VMEM((d, de), BF16), pltpu.VMEM((de, d), BF16)],
        ),
        compiler_params=_cparams(("arbitrary",)),
    )(block_expert, rows, wg, wu, wd)


def _combine_kernel(x_ref, r1_ref, r2_ref, info_ref, gate_ref, *rest, final):
    o_ref = rest[-1]
    info = info_ref[...]
    y = info[:, 2:3] * r1_ref[...].astype(F32) + info[:, 3:4] * r2_ref[...].astype(F32)
    x = x_ref[...] + gate_ref[0] * y
    if final:
        ms = jnp.mean(x * x, axis=-1, keepdims=True)
        x = x * lax.rsqrt(ms + EPS) * rest[0][...]
    o_ref[...] = x


def combine(x2d, r1, r2, info, row_off, gate, rows_per_batch, final_g=None):
    t, d = x2d.shape
    tm = min(ROUTE_ROWS, rows_per_batch)
    per = rows_per_batch // tm
    off = row_off // tm
    row = lambda i: (i, 0)
    shifted = lambda i: (i + off, 0)
    specs = [pl.BlockSpec((tm, d), row), pl.BlockSpec((tm, d), shifted), pl.BlockSpec((tm, d), shifted),
             pl.BlockSpec((tm, LANES), shifted), pl.BlockSpec((1, 1, d), lambda i: (i // per, 0, 0))]
    args = [x2d, r1, r2, info, gate]
    if final_g is not None:
        specs.append(pl.BlockSpec((1, d), lambda i: (0, 0)))
        args.append(final_g.reshape(1, d))
    return pl.pallas_call(
        functools.partial(_combine_kernel, final=final_g is not None),
        out_shape=jax.ShapeDtypeStruct((t, d), F32), grid=(t // tm,),
        in_specs=specs, out_specs=pl.BlockSpec((tm, d), row),
        compiler_params=_cparams(("parallel",)),
    )(*args)


def _dispatch(info, counts_row):
    n_tok = info.shape[0]
    expert = info[:, 0:TOP_K].astype(jnp.int32)
    rank = info[:, 4:4 + TOP_K].astype(jnp.int32)
    counts = counts_row[0, :N_EXPERTS].astype(jnp.int32)
    padded = (counts + MOE_ROWS - 1) // MOE_ROWS * MOE_ROWS
    pad_end = jnp.cumsum(padded)
    pad_start = pad_end - padded
    hot = expert[:, :, None] == jnp.arange(N_EXPERTS, dtype=jnp.int32)
    dest = jnp.sum(jnp.where(hot, pad_start, 0), axis=-1) + rank
    n_blocks = -(-(n_tok * TOP_K + N_EXPERTS * (MOE_ROWS - 1)) // MOE_ROWS)
    token = jnp.broadcast_to(jnp.arange(n_tok, dtype=jnp.int32)[:, None], dest.shape)
    src_tok = jnp.zeros((n_blocks * MOE_ROWS,), jnp.int32).at[dest.reshape(-1)].set(token.reshape(-1))
    block_start = jnp.arange(n_blocks, dtype=jnp.int32)[:, None] * MOE_ROWS
    block_expert = jnp.minimum(jnp.sum((pad_end[None, :] <= block_start).astype(jnp.int32), axis=1),
                               N_EXPERTS - 1)
    return src_tok, dest, block_expert


def _rope_tables(seq):
    n_rows = seq // GRID_W
    row = jnp.repeat(jnp.arange(n_rows), GRID_W).astype(F32)
    col = jnp.tile(jnp.arange(GRID_W), n_rows).astype(F32)
    axis_pairs = HEAD_DIM // 4
    inv = ROPE_THETA ** (-jnp.arange(axis_pairs, dtype=F32) / axis_pairs)
    ang = jnp.concatenate([row[:, None] * inv, col[:, None] * inv], axis=-1)
    cos, sin = jnp.cos(ang), jnp.sin(ang)
    cos128 = jnp.concatenate([cos, cos, cos, cos], axis=-1)
    sins128 = jnp.concatenate([-sin, sin, -sin, sin], axis=-1)
    return cos128, sins128


def kernel(x, c, ctx, c_ctx, w_mod, b_mod, norm1_g, norm2_g, w_in, hgrn_lb_logits, hgrn_out_norm_g,
           attn_q_norm_g, attn_k_norm_g, swa_sink, w_branch_a, w_branch_b, w_branch_c, w_out,
           w_group, w_router, w_exp_gate, w_exp_up, w_exp_down, final_norm_g):
    bsz, seq, d = x.shape
    n_ctx = ctx.shape[1]
    depth = w_mod.shape[0]
    cos, sins = _rope_tables(seq)
    lb_p = jax.nn.softmax(hgrn_lb_logits.astype(F32), axis=0)
    lower_bounds = jnp.cumsum(lb_p, axis=0) - lb_p[0]
    no_sink = jnp.zeros((N_HEADS,), F32)
    ones128 = jnp.ones((1, LANES), F32)

    xl = x.reshape(bsz * seq, d)
    xc = ctx.reshape(bsz * n_ctx, d)
    for layer in range(depth):
        ctx_out = layer < depth - 1
        mod_l = jax.nn.silu(c) @ w_mod[layer] + b_mod[layer]
        mod_c = jnp.broadcast_to(jax.nn.silu(c_ctx) @ w_mod[layer] + b_mod[layer], (bsz, 6 * d))
        ml = [m.reshape(bsz, 1, d) for m in jnp.split(mod_l, 6, axis=-1)]
        mc = [m.reshape(bsz, 1, d) for m in jnp.split(mod_c, 6, axis=-1)]

        w_in_b = w_in[layer].astype(BF16)
        p_l = inproj(xl, seq, norm1_g[layer], ml[1], ml[0], w_in_b, 1024, 1024)
        p_c = inproj(xc, n_ctx, norm1_g[layer], mc[1], mc[0], w_in_b, 256, 1024)
        p_l3 = p_l.reshape(bsz, seq, D_IN)
        p_c3 = p_c.reshape(bsz, n_ctx, D_IN)

        s0 = jnp.zeros((bsz, 2, HG_HEADS, HG_KDIM, HG_KDIM), F32)
        o_c, s_c = hgrn_scan(p_c3, lower_bounds[layer], s0, 256)
        o_l, _ = hgrn_scan(p_l3, lower_bounds[layer], s_c, 512)

        gq = jnp.tile(attn_q_norm_g[layer], 2).reshape(1, LANES)
        gk = jnp.tile(attn_k_norm_g[layer], 2).reshape(1, LANES)
        kt_l, v_l, kn_l = prep_kv(p_l3, COL["ak"], COL["av"], gk, cos, sins, True, True, 512)
        kt_c, v_c, kn_c = prep_kv(p_c3, COL["ak"], COL["av"], gk, cos, sins, True, False, 256)
        kmax = _key_norm_max([kn_l, kn_c])
        q_l, u_l = prep_q(p_l3, COL["aq"], gq, cos, sins, kmax, True, True, 512)
        kt_all = jnp.concatenate([kt_c, kt_l], axis=3)
        v_all = jnp.concatenate([v_c, v_l], axis=2)
        b_l = attention(q_l, kt_all, v_all, jnp.max(u_l), no_sink, 256, 1280)

        skt_l, sv_l, skn_l = prep_kv(p_l3, COL["sk"], COL["sv"], ones128, cos, sins, False, True, 512)
        skt_c, sv_c, skn_c = prep_kv(p_c3, COL["sk"], COL["sv"], ones128, cos, sins, False, False, 256)
        skmax = _key_norm_max([skn_l, skn_c])
        sq_l, su_l = prep_q(p_l3, COL["sq"], ones128, cos, sins, skmax, False, True, 512)
        zk = jnp.zeros((bsz, N_KV, LANES, WINDOW), BF16)
        zv = jnp.zeros((bsz, N_KV, WINDOW, LANES), BF16)
        skt_all = jnp.concatenate([skt_c, zk, skt_l, zk], axis=3)
        sv_all = jnp.concatenate([sv_c, zv, sv_l, zv], axis=2)
        c_l = attention(sq_l, skt_all, sv_all, jnp.max(su_l), swa_sink[layer], 256, 256,
                        use_sink=True, band=n_ctx)

        wa = w_branch_a[layer].astype(BF16)
        wb = w_branch_b[layer].astype(BF16)
        wc = w_branch_c[layer].astype(BF16)
        wo = w_out[layer].astype(BF16)
        w_route = jnp.concatenate(
            [w_group[layer], w_router[layer], jnp.zeros((d, LANES - N_GROUPS - N_EXPERTS), F32)], axis=1)
        xl, h_l, lg_l = merge(o_l.reshape(2, bsz * seq, HG_W), p_l, b_l.reshape(-1, ATT_W),
                              c_l.reshape(-1, ATT_W), xl, seq, wa, wb, wc, wo, hgrn_out_norm_g[layer],
                              ml[2], norm2_g[layer], ml[4], ml[3], w_route, 256)
        if ctx_out:
            q_c, u_c = prep_q(p_c3, COL["aq"], gq, cos, sins, kmax, True, False, 256)
            b_c = attention(q_c, kt_c, v_c, jnp.max(u_c), no_sink, 256, 256)
            sq_c, su_c = prep_q(p_c3, COL["sq"], ones128, cos, sins, skmax, False, False, 256)
            c_c = attention(sq_c, skt_c, sv_c, jnp.max(su_c), swa_sink[layer], 256, 256, use_sink=True)
            xc, h_c, lg_c = merge(o_c.reshape(2, bsz * n_ctx, HG_W), p_c, b_c.reshape(-1, ATT_W),
                                  c_c.reshape(-1, ATT_W), xc, n_ctx, wa, wb, wc, wo, hgrn_out_norm_g[layer],
                                  mc[2], norm2_g[layer], mc[4], mc[3], w_route, 256)
            h_all = jnp.concatenate([h_c, h_l], axis=0)
            lg_all = jnp.concatenate([lg_c, lg_l], axis=0)
        else:
            h_all, lg_all = h_l, lg_l

        info, counts_row = route(lg_all)
        src_tok, dest, block_expert = _dispatch(info, counts_row)
        rows = jnp.take(h_all, src_tok, axis=0, mode="clip")
        out_rows = expert_ffn(rows, block_expert, w_exp_gate, w_exp_up, w_exp_down, layer)
        r1 = jnp.take(out_rows, dest[:, 0], axis=0, mode="clip")
        r2 = jnp.take(out_rows, dest[:, 1], axis=0, mode="clip")
        n_c = bsz * n_ctx if ctx_out else 0
        if ctx_out:
            xc = combine(xc, r1, r2, info, 0, mc[5], n_ctx)
        xl = combine(xl, r1, r2, info, n_c, ml[5], seq, final_g=final_norm_g if layer == depth - 1 else None)
    return xl.reshape(bsz, seq, d)
```

```python
import functools
import math

import numpy as np
import jax
import jax.numpy as jnp
from jax import lax
from jax.experimental import pallas as pl
from jax.experimental.pallas import tpu as pltpu

F32 = jnp.float32
BF16 = jnp.bfloat16

EPS = 1e-6
MASK_VALUE = -1e30
TINY = 1e-30
GRID_W = 64
ROPE_THETA = 10000.0

HG_HEADS = 4
HG_KDIM = 128
HG_W = HG_HEADS * HG_KDIM
HEAD_DIM = 64
N_HEADS = 8
N_KV = 2
GROUP = N_HEADS // N_KV
ATT_W = N_HEADS * HEAD_DIM
KV_W = N_KV * HEAD_DIM
WINDOW = 128
ATTN_SCALE = HEAD_DIM ** -0.5
N_GROUPS = 4
EXP_PER_GROUP = 8
N_EXPERTS = N_GROUPS * EXP_PER_GROUP
TOP_K = 2

LANES = 128
HG_CHUNK = 128
HG_LEVELS = int(math.log2(HG_CHUNK))
HG_TOTAL_ROWS = 16
MOE_ROWS = 256
ROUTE_ROWS = 512
VMEM_LIMIT = 56 * 1024 * 1024

LOG2E = 1.4426950408889634
SHIFT_HEADROOM = 57.0
SAFE_BOUND = 90.0


def _cparams(sem):
    return pltpu.CompilerParams(dimension_semantics=sem, vmem_limit_bytes=VMEM_LIMIT)


_SIZES = (HG_W, HG_W, HG_W, HG_W, HG_W, ATT_W, KV_W, KV_W, ATT_W, KV_W, KV_W, 3 * 1024)
_NAMES = ("hq", "ff", "fb", "hi", "hg", "aq", "ak", "av", "sq", "sk", "sv", "gates")
COL = {n: int(sum(_SIZES[:i])) for i, n in enumerate(_NAMES)}
D_IN = int(sum(_SIZES))


def _inproj_kernel(x_ref, g_ref, sc_ref, sh_ref, w_ref, o_ref, h_scr):
    @pl.when(pl.program_id(1) == 0)
    def _():
        x = x_ref[...]
        ms = jnp.mean(x * x, axis=-1, keepdims=True)
        y = x * lax.rsqrt(ms + EPS) * g_ref[...]
        h_scr[...] = (y * (1.0 + sc_ref[0]) + sh_ref[0]).astype(BF16)

    o_ref[...] = jnp.dot(h_scr[...], w_ref[...], preferred_element_type=F32).astype(o_ref.dtype)


def inproj(x2d, rows_per_batch, g, scale, shift, w_bf16, tm, tn):
    t, d = x2d.shape
    n = w_bf16.shape[1]
    tm = min(tm, rows_per_batch)
    per = rows_per_batch // tm
    return pl.pallas_call(
        _inproj_kernel,
        out_shape=jax.ShapeDtypeStruct((t, n), BF16),
        grid=(t // tm, n // tn),
        in_specs=[
            pl.BlockSpec((tm, d), lambda i, j: (i, 0)),
            pl.BlockSpec((1, d), lambda i, j: (0, 0)),
            pl.BlockSpec((1, 1, d), lambda i, j: (i // per, 0, 0)),
            pl.BlockSpec((1, 1, d), lambda i, j: (i // per, 0, 0)),
            pl.BlockSpec((d, tn), lambda i, j: (0, j)),
        ],
        out_specs=pl.BlockSpec((tm, tn), lambda i, j: (i, j)),
        scratch_shapes=[pltpu.VMEM((tm, d), BF16)],
        compiler_params=_cparams(("parallel", "arbitrary")),
    )(x2d, g.reshape(1, d), scale, shift, w_bf16)


def _hgrn_exponent_matrices():
    c, nl = HG_CHUNK, HG_LEVELS
    rows = (nl + 2) * c + HG_TOTAL_ROWS
    out = np.zeros((2, rows, c), np.float32)
    for d in range(2):
        pos = np.arange(c) if d == 0 else c - 1 - np.arange(c)
        for l in range(nl):
            m = 1 << l
            for t in range(c):
                p = pos[t]
                mid = (p // (2 * m)) * 2 * m + m
                if p >= mid:
                    sel = (pos >= mid) & (pos <= p)
                else:
                    sel = (pos > p) & (pos <= mid - 1)
                out[d, l * c + t, sel] = 1.0
        for t in range(c):
            out[d, nl * c + t, pos <= pos[t]] = 1.0
            out[d, (nl + 1) * c + t, pos > pos[t]] = 1.0
        out[d, (nl + 2) * c:, :] = 1.0
    return np.concatenate([out, out], axis=2)


_HG_EXP_MATS = _hgrn_exponent_matrices()


def _sigmoid(x):
    return 1.0 / (1.0 + jnp.exp(-x))


def _hgrn_kernel(q_ref, f_ref, i_ref, lb_ref, mat_ref, s0_ref, o_ref, st_ref, *, n_chunks):
    c, nl = HG_CHUNK, HG_LEVELS
    d = pl.program_id(1)

    @pl.when(pl.program_id(2) == 0)
    def _():
        st_ref[...] = s0_ref[...]

    row = lax.broadcasted_iota(jnp.int32, (c, HG_W), 0)
    pos = row + d * (c - 1 - 2 * row)
    r_i = lax.broadcasted_iota(jnp.int32, (c, c), 0)
    c_i = lax.broadcasted_iota(jnp.int32, (c, c), 1)
    xor_rc = r_i ^ c_i
    top_bit = sum((xor_rc >= (1 << j)).astype(jnp.int32) for j in range(1, nl))
    pair_level = jnp.where((r_i - c_i) * (1 - 2 * d) > 0, top_bit, -1)

    def chunk(ci, carry):
        cc = ci + d * (n_chunks - 1 - 2 * ci)
        r0 = pl.multiple_of(cc * c, c)
        qraw = q_ref[0, pl.ds(r0, c), :].astype(F32)
        fz = f_ref[0, pl.ds(r0, c), :].astype(F32)
        vb = i_ref[0, pl.ds(r0, c), :]
        v = vb.astype(F32)
        lb = lb_ref[...]
        sig = _sigmoid(fz)
        logf = jnp.log(jnp.maximum(lb + (1.0 - lb) * sig, TINY))
        key = (1.0 - lb) * (1.0 - sig)
        qh = qraw * _sigmoid(qraw) * (HG_KDIM ** -0.5)
        logf2 = logf * LOG2E
        hi = logf2.astype(BF16)
        lo = (logf2 - hi.astype(F32)).astype(BF16)
        hilo = jnp.concatenate([hi, lo], axis=0)

        dec_all = jnp.exp2(jnp.dot(mat_ref[0], hilo, preferred_element_type=F32))

        def decay(r_lo, r_hi):
            return dec_all[r_lo:r_hi]

        a = [jnp.zeros((c, c), F32)] * HG_HEADS
        for l in range(nl):
            x = (jnp.where(((pos >> l) & 1) == 1, qh, key) * decay(l * c, (l + 1) * c)).astype(BF16)
            for h in range(HG_HEADS):
                xh = x[:, h * LANES:(h + 1) * LANES]
                prod = lax.dot_general(xh, xh, (((1,), (1,)), ((), ())), preferred_element_type=F32)
                a[h] = jnp.where(pair_level == l, prod, a[h])
        q_in = (qh * decay(nl * c, (nl + 1) * c)).astype(BF16)
        k_out = (key * decay((nl + 1) * c, (nl + 2) * c)).astype(BF16)
        total = decay((nl + 2) * c, (nl + 2) * c + HG_TOTAL_ROWS)[0:1]
        qk = qh * key
        for h in range(HG_HEADS):
            lanes = slice(h * LANES, (h + 1) * LANES)
            st = st_ref[0, 0, h]
            o = (jnp.dot(a[h].astype(BF16), vb[:, lanes], preferred_element_type=F32)
                 + jnp.sum(qk[:, lanes], axis=1, keepdims=True) * v[:, lanes]
                 + lax.dot_general(q_in[:, lanes], st.astype(BF16), (((1,), (1,)), ((), ())),
                                   preferred_element_type=F32))
            o_ref[0, 0, pl.ds(r0, c), lanes] = o.astype(o_ref.dtype)
            st_ref[0, 0, h] = total[:, lanes] * st + lax.dot_general(
                vb[:, lanes], k_out[:, lanes], (((0,), (0,)), ((), ())), preferred_element_type=F32)
        return carry

    lax.fori_loop(0, n_chunks, chunk, 0)


def hgrn_scan(p3, lb, s0, tb):
    b, l, _ = p3.shape
    tb = min(tb, l)
    nb = l // tb
    blk = lambda d, n: n + d * (nb - 1 - 2 * n)
    wblk = HG_W
    kern = functools.partial(_hgrn_kernel, n_chunks=tb // HG_CHUNK)
    mats = jnp.asarray(_HG_EXP_MATS, BF16)
    return pl.pallas_call(
        kern,
        out_shape=(jax.ShapeDtypeStruct((2, b, l, HG_W), BF16),
                   jax.ShapeDtypeStruct((b, 2, HG_HEADS, HG_KDIM, HG_KDIM), F32)),
        grid=(b, 2, nb),
        in_specs=[
            pl.BlockSpec((1, tb, wblk), lambda bi, d, n: (bi, blk(d, n), COL["hq"] // wblk)),
            pl.BlockSpec((1, tb, wblk), lambda bi, d, n: (bi, blk(d, n), COL["ff"] // wblk + d)),
            pl.BlockSpec((1, tb, wblk), lambda bi, d, n: (bi, blk(d, n), COL["hi"] // wblk)),
            pl.BlockSpec((1, HG_W), lambda bi, d, n: (0, 0)),
            pl.BlockSpec((1,) + _HG_EXP_MATS.shape[1:], lambda bi, d, n: (d, 0, 0)),
            pl.BlockSpec((1, 1, HG_HEADS, HG_KDIM, HG_KDIM), lambda bi, d, n: (bi, d, 0, 0, 0)),
        ],
        out_specs=(
            pl.BlockSpec((1, 1, tb, HG_W), lambda bi, d, n: (d, bi, blk(d, n), 0)),
            pl.BlockSpec((1, 1, HG_HEADS, HG_KDIM, HG_KDIM), lambda bi, d, n: (bi, d, 0, 0, 0)),
        ),
        compiler_params=_cparams(("parallel", "arbitrary", "arbitrary")),
    )(p3, p3, p3, lb.reshape(1, HG_W), mats, s0)


_GROUP_ONES = np.kron(np.eye(LANES // HEAD_DIM, dtype=np.float32), np.ones((HEAD_DIM, HEAD_DIM), np.float32))
_GROUP_ONES2 = np.concatenate([_GROUP_ONES, _GROUP_ONES], axis=0)


def _group_ssq(x, ones2):
    sq = x * x
    hi = sq.astype(BF16)
    lo = (sq - hi.astype(F32)).astype(BF16)
    return jnp.dot(jnp.concatenate([hi, lo], axis=1), ones2, preferred_element_type=F32)


def _prep_tile(x, g, cos, sins, ones2, norm, rope):
    if norm:
        x = x * lax.rsqrt(_group_ssq(x, ones2) * (1.0 / HEAD_DIM) + EPS) * g
    if rope:
        lane = lax.broadcasted_iota(jnp.int32, x.shape, 1)
        first = (lane % HEAD_DIM) < (HEAD_DIM // 2)
        other = jnp.where(first, pltpu.roll(x, LANES - HEAD_DIM // 2, 1), pltpu.roll(x, HEAD_DIM // 2, 1))
        x = x * cos + other * sins
    return x


def _prep_kv_kernel(k_ref, v_ref, g_ref, cos_ref, sin_ref, ones_ref, kt_ref, vo_ref, kn_ref, *, norm, rope):
    k = _prep_tile(k_ref[0].astype(F32), g_ref[...], cos_ref[...], sin_ref[...], ones_ref[...], norm, rope)
    kb = k.astype(BF16).astype(F32)
    ssq = _group_ssq(kb, ones_ref[...])
    kn_ref[0, 0] = jnp.broadcast_to(jnp.max(ssq, axis=0, keepdims=True), (8, LANES))
    kt = kb.T
    row = lax.broadcasted_iota(jnp.int32, kt.shape, 0)
    v = v_ref[0]
    low = lax.broadcasted_iota(jnp.int32, v.shape, 1) < HEAD_DIM
    one = jnp.ones((), v.dtype)
    for g in range(N_KV):
        ktg = kt if g == 0 else jnp.concatenate([kt[HEAD_DIM:], kt[:HEAD_DIM]], axis=0)
        kt_ref[0, g] = jnp.where(row < HEAD_DIM, ktg, jnp.where(row == HEAD_DIM, 1.0, 0.0)).astype(BF16)
        vg = v if g == 0 else jnp.concatenate([v[:, HEAD_DIM:], v[:, :HEAD_DIM]], axis=1)
        vo_ref[0, g] = jnp.where(low, vg, one).astype(BF16)


def _prep_q_kernel(kmax_ref, x0_ref, x1_ref, x2_ref, x3_ref, g_ref, cos_ref, sin_ref, ones_ref, o_ref, u_ref,
                   *, norm, rope):
    b = pl.program_id(0)
    lane = lax.broadcasted_iota(jnp.int32, (x0_ref.shape[1], LANES), 1)
    umax = None
    for tile, x_ref in enumerate((x0_ref, x1_ref, x2_ref, x3_ref)):
        kv = (2 * tile) // GROUP
        x = x_ref[0].astype(F32)
        y = _prep_tile(x, g_ref[...], cos_ref[...], sin_ref[...], ones_ref[...], norm, rope) * (ATTN_SCALE * LOG2E)
        yb = y.astype(BF16).astype(F32)
        u2 = jnp.sqrt(_group_ssq(yb, ones_ref[...])) * kmax_ref[b * N_KV + kv]
        tmax = jnp.max(u2, axis=0, keepdims=True)
        umax = tmax if umax is None else jnp.maximum(umax, tmax)
        neg_shift = SHIFT_HEADROOM - u2
        y_sw = pltpu.roll(yb, HEAD_DIM, 1)
        ns_sw = pltpu.roll(neg_shift, HEAD_DIM, 1)
        for half in range(2):
            h = 2 * tile + half
            data = yb if half == 0 else y_sw
            ns = ns_sw if half == 0 else neg_shift
            o_ref[0, :, h * LANES:(h + 1) * LANES] = jnp.where(
                lane < HEAD_DIM, data, jnp.where(lane == HEAD_DIM, ns, 0.0)).astype(BF16)
    u_ref[0, 0] = jnp.broadcast_to(umax, (8, LANES))


def prep_kv(p3, col_k, col_v, g128, cos, sins, norm, rope, tq):
    b, l, _ = p3.shape
    tq = min(tq, l)
    ones2 = jnp.asarray(_GROUP_ONES2, BF16)
    return pl.pallas_call(
        functools.partial(_prep_kv_kernel, norm=norm, rope=rope),
        out_shape=(jax.ShapeDtypeStruct((b, N_KV, LANES, l), BF16),
                   jax.ShapeDtypeStruct((b, N_KV, l, LANES), BF16),
                   jax.ShapeDtypeStruct((b, l // tq, 8, LANES), F32)),
        grid=(b, l // tq),
        in_specs=[
            pl.BlockSpec((1, tq, KV_W), lambda bi, i: (bi, i, col_k // KV_W)),
            pl.BlockSpec((1, tq, KV_W), lambda bi, i: (bi, i, col_v // KV_W)),
            pl.BlockSpec((1, LANES), lambda bi, i: (0, 0)),
            pl.BlockSpec((tq, LANES), lambda bi, i: (i, 0)),
            pl.BlockSpec((tq, LANES), lambda bi, i: (i, 0)),
            pl.BlockSpec((2 * LANES, LANES), lambda bi, i: (0, 0)),
        ],
        out_specs=(pl.BlockSpec((1, N_KV, LANES, tq), lambda bi, i: (bi, 0, 0, i)),
                   pl.BlockSpec((1, N_KV, tq, LANES), lambda bi, i: (bi, 0, i, 0)),
                   pl.BlockSpec((1, 1, 8, LANES), lambda bi, i: (bi, i, 0, 0))),
        compiler_params=_cparams(("parallel", "parallel")),
    )(p3, p3, g128, cos, sins, ones2)


def prep_q(p3, col, g128, cos, sins, kmax, norm, rope, tq):
    b, l, _ = p3.shape
    tq = min(tq, l)
    ones2 = jnp.asarray(_GROUP_ONES2, BF16)
    return pl.pallas_call(
        functools.partial(_prep_q_kernel, norm=norm, rope=rope),
        out_shape=(jax.ShapeDtypeStruct((b, l, N_HEADS * LANES), BF16),
                   jax.ShapeDtypeStruct((b, l // tq, 8, LANES), F32)),
        grid_spec=pltpu.PrefetchScalarGridSpec(
            num_scalar_prefetch=1,
            grid=(b, l // tq),
            in_specs=[
                pl.BlockSpec((1, tq, LANES), lambda bi, i, s, t=t: (bi, i, col // LANES + t))
                for t in range(ATT_W // LANES)
            ] + [
                pl.BlockSpec((1, LANES), lambda bi, i, s: (0, 0)),
                pl.BlockSpec((tq, LANES), lambda bi, i, s: (i, 0)),
                pl.BlockSpec((tq, LANES), lambda bi, i, s: (i, 0)),
                pl.BlockSpec((2 * LANES, LANES), lambda bi, i, s: (0, 0)),
            ],
            out_specs=(pl.BlockSpec((1, tq, N_HEADS * LANES), lambda bi, i, s: (bi, i, 0)),
                       pl.BlockSpec((1, 1, 8, LANES), lambda bi, i, s: (bi, i, 0, 0))),
        ),
        compiler_params=_cparams(("parallel", "parallel")),
    )(kmax, p3, p3, p3, p3, g128, cos, sins, ones2)


def _key_norm_max(kn_list):
    kn = functools.reduce(jnp.maximum, [jnp.max(k, axis=(1, 2)) for k in kn_list])
    return jnp.sqrt(kn[:, ::HEAD_DIM]).reshape(-1)


def _attn_kernel(sink_ref, q_ref, kt_ref, v_ref, o_ref, *, tk, online, use_sink, band, seq):
    tq = q_ref.shape[1]
    rows = GROUP * tq
    lane = lax.broadcasted_iota(jnp.int32, (tq, LANES), 1)
    low = lane < HEAD_DIM
    if band is not None:
        width = tq + 2 * WINDOW
        start = pl.multiple_of(pl.program_id(1) * tq, tq)
        r = lax.broadcasted_iota(jnp.int32, (rows, width), 0) % tq
        kp = lax.broadcasted_iota(jnp.int32, (rows, width), 1)
        key_pos = start + kp - WINDOW
        mask = (jnp.abs(kp - WINDOW - r) <= WINDOW) & (key_pos >= 0) & (key_pos < seq)
    for g in range(N_KV):
        tiles = [q_ref[0, :, (g * GROUP + hh) * LANES:(g * GROUP + hh + 1) * LANES] for hh in range(GROUP)]
        qg = jnp.concatenate(tiles, axis=0)

        def scores(kt, qg=qg):
            return jnp.dot(qg, kt, preferred_element_type=F32)

        def update(carry, s, v, msk=None):
            if msk is not None:
                s = jnp.where(msk, s, MASK_VALUE)
            if online:
                m, acc = carry
                m_new = jnp.maximum(m, jnp.max(s, axis=1, keepdims=True))
                p = jnp.exp2(s - m_new).astype(BF16)
                return m_new, jnp.exp2(m - m_new) * acc + jnp.dot(p, v, preferred_element_type=F32)
            return carry + jnp.dot(jnp.exp2(s).astype(BF16), v, preferred_element_type=F32)

        acc0 = jnp.zeros((rows, LANES), F32)
        carry = (jnp.full((rows, 1), MASK_VALUE, F32), acc0) if online else acc0
        if band is None:
            def body(j, c, g=g, scores=scores, update=update):
                k0 = pl.multiple_of(j * tk, tk)
                return update(c, scores(kt_ref[0, g, :, pl.ds(k0, tk)]), v_ref[0, g, pl.ds(k0, tk), :])

            carry = lax.fori_loop(0, kt_ref.shape[3] // tk, body, carry)
        else:
            carry = update(carry, scores(kt_ref[0, g, :, 0:band]), v_ref[0, g, 0:band, :])
            b0 = pl.multiple_of(band + start, LANES)
            carry = update(carry, scores(kt_ref[0, g, :, pl.ds(b0, width)]), v_ref[0, g, pl.ds(b0, width), :], mask)
        acc = carry[1] if online else carry
        if use_sink:
            e = jnp.concatenate([t[:, HEAD_DIM:HEAD_DIM + 1].astype(F32) + sink_ref[g * GROUP + hh] * LOG2E
                                 for hh, t in enumerate(tiles)], axis=0)
            if online:
                e = e - carry[0]
            lane_r = lax.broadcasted_iota(jnp.int32, (rows, LANES), 1)
            acc = acc + jnp.where(lane_r >= HEAD_DIM, jnp.exp2(e), 0.0)
        o = acc * pltpu.roll(1.0 / acc, HEAD_DIM, 1)
        for pair in range(GROUP // 2):
            a = o[(2 * pair) * tq:(2 * pair + 1) * tq]
            b = o[(2 * pair + 1) * tq:(2 * pair + 2) * tq]
            t0 = (g * GROUP // 2 + pair) * LANES
            o_ref[0, :, t0:t0 + LANES] = jnp.where(low, a, pltpu.roll(b, HEAD_DIM, 1)).astype(o_ref.dtype)


def _attention(q, kt, v, sink, tq, tk, online, use_sink, band):
    b, l, _ = q.shape
    sk = kt.shape[3]
    tq = min(tq, l)
    tk = min(tk, sk)
    kern = functools.partial(_attn_kernel, tk=tk, online=online, use_sink=use_sink, band=band, seq=l)
    return pl.pallas_call(
        kern,
        out_shape=jax.ShapeDtypeStruct((b, l, ATT_W), BF16),
        grid_spec=pltpu.PrefetchScalarGridSpec(
            num_scalar_prefetch=1,
            grid=(b, l // tq),
            in_specs=[
                pl.BlockSpec((1, tq, N_HEADS * LANES), lambda bi, i, s: (bi, i, 0)),
                pl.BlockSpec((1, N_KV, LANES, sk), lambda bi, i, s: (bi, 0, 0, 0)),
                pl.BlockSpec((1, N_KV, sk, LANES), lambda bi, i, s: (bi, 0, 0, 0)),
            ],
            out_specs=pl.BlockSpec((1, tq, ATT_W), lambda bi, i, s: (bi, i, 0)),
        ),
        compiler_params=_cparams(("parallel", "arbitrary")),
    )(sink, q, kt, v)


def attention(q, kt, v, ubound, sink, tq, tk, use_sink=False, band=None):
    return lax.cond(ubound <= SAFE_BOUND,
                    lambda: _attention(q, kt, v, sink, tq, tk, False, use_sink, band),
                    lambda: _attention(q, kt, v, sink, tq, tk, True, use_sink, band))


def _merge_kernel(of_ref, ob_ref, hg_ref, b_ref, c_ref, ga_ref, gb_ref, gc_ref, x_ref,
                  wa_ref, wb_ref, wc_ref, wo_ref, hgn_ref, gate1_ref, n2_ref, sc2_ref, sh2_ref, wr_ref,
                  xo_ref, h2_ref, lg_ref):
    o = of_ref[0].astype(F32) + ob_ref[0].astype(F32)
    tiles = []
    for h in range(HG_HEADS):
        t = o[:, h * LANES:(h + 1) * LANES]
        ms = jnp.mean(t * t, axis=-1, keepdims=True)
        tiles.append(t * lax.rsqrt(ms + EPS) * hgn_ref[...])
    hg = hg_ref[...].astype(F32)
    a = (jnp.concatenate(tiles, axis=1) * (hg * _sigmoid(hg))).astype(BF16)
    merged = (_sigmoid(ga_ref[...].astype(F32)) * jnp.dot(a, wa_ref[...], preferred_element_type=F32)
              + _sigmoid(gb_ref[...].astype(F32)) * jnp.dot(b_ref[...], wb_ref[...], preferred_element_type=F32)
              + _sigmoid(gc_ref[...].astype(F32)) * jnp.dot(c_ref[...], wc_ref[...], preferred_element_type=F32))
    y = jnp.dot(merged.astype(BF16), wo_ref[...], preferred_element_type=F32)
    x = x_ref[...] + gate1_ref[0] * y
    xo_ref[...] = x
    ms = jnp.mean(x * x, axis=-1, keepdims=True)
    h2 = (x * lax.rsqrt(ms + EPS) * n2_ref[...]) * (1.0 + sc2_ref[0]) + sh2_ref[0]
    h2_ref[...] = h2.astype(BF16)
    lg_ref[...] = jnp.dot(h2, wr_ref[...], preferred_element_type=F32, precision=lax.Precision.HIGHEST)


def merge(o_fb, p2d, b2d, c2d, x2d, rows_per_batch, wa, wb, wc, wo, hgn, gate1, n2, sc2, sh2, w_route, tm):
    t, d = x2d.shape
    tm = min(tm, rows_per_batch)
    per = rows_per_batch // tm
    row = lambda i: (i, 0)
    const = lambda i: (0, 0)
    bat = lambda i: (i // per, 0, 0)
    gcol = COL["gates"] // d
    return pl.pallas_call(
        _merge_kernel,
        out_shape=(jax.ShapeDtypeStruct((t, d), F32), jax.ShapeDtypeStruct((t, d), BF16),
                   jax.ShapeDtypeStruct((t, LANES), F32)),
        grid=(t // tm,),
        in_specs=[
            pl.BlockSpec((1, tm, HG_W), lambda i: (0, i, 0)),
            pl.BlockSpec((1, tm, HG_W), lambda i: (1, i, 0)),
            pl.BlockSpec((tm, HG_W), lambda i: (i, COL["hg"] // HG_W)),
            pl.BlockSpec((tm, ATT_W), row),
            pl.BlockSpec((tm, ATT_W), row),
            pl.BlockSpec((tm, d), lambda i: (i, gcol)),
            pl.BlockSpec((tm, d), lambda i: (i, gcol + 1)),
            pl.BlockSpec((tm, d), lambda i: (i, gcol + 2)),
            pl.BlockSpec((tm, d), row),
            pl.BlockSpec((HG_W, d), const),
            pl.BlockSpec((ATT_W, d), const),
            pl.BlockSpec((ATT_W, d), const),
            pl.BlockSpec((d, d), const),
            pl.BlockSpec((1, LANES), const),
            pl.BlockSpec((1, 1, d), bat),
            pl.BlockSpec((1, d), const),
            pl.BlockSpec((1, 1, d), bat),
            pl.BlockSpec((1, 1, d), bat),
            pl.BlockSpec((d, LANES), const),
        ],
        out_specs=(pl.BlockSpec((tm, d), row), pl.BlockSpec((tm, d), row), pl.BlockSpec((tm, LANES), row)),
        compiler_params=_cparams(("parallel",)),
    )(o_fb, o_fb, p2d, b2d, c2d, p2d, p2d, p2d, x2d, wa, wb, wc, wo, hgn.reshape(1, LANES),
      gate1, n2.reshape(1, d), sc2, sh2, w_route)


_STRICT_LOWER = np.tril(np.ones((ROUTE_ROWS, ROUTE_ROWS), np.float32), -1)
_NEG_BIG = -3.0e38


def _lane_argmax(x, lane):
    top = jnp.max(x, axis=1, keepdims=True)
    idx = jnp.min(jnp.where(x == top, lane, LANES), axis=1, keepdims=True)
    return top, idx


def _route_kernel(lg_ref, tri_ref, o_ref, cnt_ref, run_scr):
    @pl.when(pl.program_id(0) == 0)
    def _():
        run_scr[...] = jnp.zeros_like(run_scr)

    lg = lg_ref[...]
    lane = lax.broadcasted_iota(jnp.int32, lg.shape, 1)
    is_grp = lane < N_GROUPS
    gtop, gidx = _lane_argmax(jnp.where(is_grp, lg, _NEG_BIG), lane)
    grp_w = 1.0 / jnp.sum(jnp.where(is_grp, jnp.exp(lg - gtop), 0.0), axis=1, keepdims=True)
    lo = N_GROUPS + EXP_PER_GROUP * gidx
    x1 = jnp.where((lane >= lo) & (lane < lo + EXP_PER_GROUP), lg, _NEG_BIG)
    t1, i1 = _lane_argmax(x1, lane)
    t2, i2 = _lane_argmax(jnp.where(lane == i1, _NEG_BIG, x1), lane)
    r = jnp.exp(t2 - t1)
    w1 = grp_w / (1.0 + r)
    w2 = w1 * r
    e1 = i1 - N_GROUPS
    e2 = i2 - N_GROUPS
    hot1 = lane == e1
    hot2 = lane == e2
    hot = jnp.where(hot1 | hot2, 1.0, 0.0)
    before = run_scr[...] + jnp.dot(tri_ref[...], hot.astype(BF16), preferred_element_type=F32)
    rank1 = jnp.sum(jnp.where(hot1, before, 0.0), axis=1, keepdims=True)
    rank2 = jnp.sum(jnp.where(hot2, before, 0.0), axis=1, keepdims=True)
    run_scr[...] = run_scr[...] + jnp.sum(hot, axis=0, keepdims=True)
    cnt_ref[...] = jnp.broadcast_to(run_scr[...], cnt_ref.shape)
    out = jnp.where(lane == 0, e1.astype(F32), jnp.where(lane == 1, e2.astype(F32), 0.0))
    out = jnp.where(lane == 2, w1, jnp.where(lane == 3, w2, out))
    o_ref[...] = jnp.where(lane == 4, rank1, jnp.where(lane == 5, rank2, out))


def route(logits):
    t = logits.shape[0]
    return pl.pallas_call(
        _route_kernel,
        out_shape=(jax.ShapeDtypeStruct((t, LANES), F32), jax.ShapeDtypeStruct((8, LANES), F32)),
        grid=(t // ROUTE_ROWS,),
        in_specs=[pl.BlockSpec((ROUTE_ROWS, LANES), lambda i: (i, 0)),
                  pl.BlockSpec((ROUTE_ROWS, ROUTE_ROWS), lambda i: (0, 0))],
        out_specs=(pl.BlockSpec((ROUTE_ROWS, LANES), lambda i: (i, 0)),
                   pl.BlockSpec((8, LANES), lambda i: (0, 0))),
        scratch_shapes=[pltpu.VMEM((1, LANES), F32)],
        compiler_params=_cparams(("arbitrary",)),
    )(logits, jnp.asarray(_STRICT_LOWER, BF16))


def _expert_kernel(be_ref, x_ref, wg_ref, wu_ref, wd_ref, *rest):
    o_ref, wg_s, wu_s, wd_s = rest[-4:]
    i = pl.program_id(0)

    @pl.when((i == 0) | (be_ref[i] != be_ref[jnp.maximum(i - 1, 0)]))
    def _():
        wg_s[...] = wg_ref[0, 0].astype(BF16)
        wu_s[...] = wu_ref[0, 0].astype(BF16)
        wd_s[...] = wd_ref[0, 0].astype(BF16)

    x = x_ref[...]
    gte = jnp.dot(x, wg_s[...], preferred_element_type=F32)
    up = jnp.dot(x, wu_s[...], preferred_element_type=F32)
    hid = (gte * _sigmoid(gte) * up).astype(BF16)
    o_ref[...] = jnp.dot(hid, wd_s[...], preferred_element_type=F32).astype(o_ref.dtype)


def expert_ffn(rows, block_expert, wg, wu, wd, layer, out_prev, block_off, n_total):
    n, d = rows.shape
    de = wg.shape[3]
    in_specs = [
        pl.BlockSpec((MOE_ROWS, d), lambda i, be: (i, 0)),
        pl.BlockSpec((1, 1, d, de), lambda i, be: (layer, be[i], 0, 0)),
        pl.BlockSpec((1, 1, d, de), lambda i, be: (layer, be[i], 0, 0)),
        pl.BlockSpec((1, 1, de, d), lambda i, be: (layer, be[i], 0, 0)),
    ]
    args = [block_expert, rows, wg, wu, wd]
    aliases = {}
    if out_prev is not None:
        in_specs.append(pl.BlockSpec(memory_space=pl.ANY))
        args.append(out_prev)
        aliases = {5: 0}
    return pl.pallas_call(
        _expert_kernel,
        out_shape=jax.ShapeDtypeStruct((n_total, d), BF16),
        grid_spec=pltpu.PrefetchScalarGridSpec(
            num_scalar_prefetch=1,
            grid=(n // MOE_ROWS,),
            in_specs=in_specs,
            out_specs=pl.BlockSpec((MOE_ROWS, d), lambda i, be: (i + block_off, 0)),
            scratch_shapes=[pltpu.VMEM((d, de), BF16), pltpu.VMEM((d, de), BF16), pltpu.VMEM((de, d), BF16)],
        ),
        input_output_aliases=aliases,
        compiler_params=_cparams(("arbitrary",)),
    )(*args)


MOE_PARTS = 4


def experts_overlapped(h_all, src_tok, block_expert, wg, wu, wd, layer):
    n_blocks = block_expert.shape[0]
    n_total = n_blocks * MOE_ROWS
    bounds = [n_blocks * k // MOE_PARTS for k in range(MOE_PARTS + 1)]
    out = None
    for b0, b1 in zip(bounds[:-1], bounds[1:]):
        rows = jnp.take(h_all, src_tok[b0 * MOE_ROWS:b1 * MOE_ROWS], axis=0, mode="clip")
        out = expert_ffn(rows, block_expert[b0:b1], wg, wu, wd, layer, out, b0, n_total)
    return out


def _combine_kernel(x_ref, r1_ref, r2_ref, info_ref, gate_ref, *rest, final):
    o_ref = rest[-1]
    info = info_ref[...]
    y = info[:, 2:3] * r1_ref[...].astype(F32) + info[:, 3:4] * r2_ref[...].astype(F32)
    x = x_ref[...] + gate_ref[0] * y
    if final:
        ms = jnp.mean(x * x, axis=-1, keepdims=True)
        x = x * lax.rsqrt(ms + EPS) * rest[0][...]
    o_ref[...] = x


def combine(x2d, r1, r2, info, row_off, gate, rows_per_batch, final_g=None):
    t, d = x2d.shape
    tm = min(ROUTE_ROWS, rows_per_batch)
    per = rows_per_batch // tm
    off = row_off // tm
    row = lambda i: (i, 0)
    shifted = lambda i: (i + off, 0)
    specs = [pl.BlockSpec((tm, d), row), pl.BlockSpec((tm, d), shifted), pl.BlockSpec((tm, d), shifted),
             pl.BlockSpec((tm, LANES), shifted), pl.BlockSpec((1, 1, d), lambda i: (i // per, 0, 0))]
    args = [x2d, r1, r2, info, gate]
    if final_g is not None:
        specs.append(pl.BlockSpec((1, d), lambda i: (0, 0)))
        args.append(final_g.reshape(1, d))
    return pl.pallas_call(
        functools.partial(_combine_kernel, final=final_g is not None),
        out_shape=jax.ShapeDtypeStruct((t, d), F32), grid=(t // tm,),
        in_specs=specs, out_specs=pl.BlockSpec((tm, d), row),
        compiler_params=_cparams(("parallel",)),
    )(*args)


def _dispatch(info, counts_row):
    n_tok = info.shape[0]
    expert = info[:, 0:TOP_K].astype(jnp.int32)
    rank = info[:, 4:4 + TOP_K].astype(jnp.int32)
    counts = counts_row[0, :N_EXPERTS].astype(jnp.int32)
    padded = (counts + MOE_ROWS - 1) // MOE_ROWS * MOE_ROWS
    pad_end = jnp.cumsum(padded)
    pad_start = pad_end - padded
    hot = expert[:, :, None] == jnp.arange(N_EXPERTS, dtype=jnp.int32)
    dest = jnp.sum(jnp.where(hot, pad_start, 0), axis=-1) + rank
    n_blocks = -(-(n_tok * TOP_K + N_EXPERTS * (MOE_ROWS - 1)) // MOE_ROWS)
    token = jnp.broadcast_to(jnp.arange(n_tok, dtype=jnp.int32)[:, None], dest.shape)
    src_tok = jnp.zeros((n_blocks * MOE_ROWS,), jnp.int32).at[dest.reshape(-1)].set(token.reshape(-1))
    block_start = jnp.arange(n_blocks, dtype=jnp.int32)[:, None] * MOE_ROWS
    block_expert = jnp.minimum(jnp.sum((pad_end[None, :] <= block_start).astype(jnp.int32), axis=1),
                               N_EXPERTS - 1)
    return src_tok, dest, block_expert


def _rope_tables(seq):
    n_rows = seq // GRID_W
    row = jnp.repeat(jnp.arange(n_rows), GRID_W).astype(F32)
    col = jnp.tile(jnp.arange(GRID_W), n_rows).astype(F32)
    axis_pairs = HEAD_DIM // 4
    inv = ROPE_THETA ** (-jnp.arange(axis_pairs, dtype=F32) / axis_pairs)
    ang = jnp.concatenate([row[:, None] * inv, col[:, None] * inv], axis=-1)
    cos, sin = jnp.cos(ang), jnp.sin(ang)
    cos128 = jnp.concatenate([cos, cos, cos, cos], axis=-1)
    sins128 = jnp.concatenate([-sin, sin, -sin, sin], axis=-1)
    return cos128, sins128


def kernel(x, c, ctx, c_ctx, w_mod, b_mod, norm1_g, norm2_g, w_in, hgrn_lb_logits, hgrn_out_norm_g,
           attn_q_norm_g, attn_k_norm_g, swa_sink, w_branch_a, w_branch_b, w_branch_c, w_out,
           w_group, w_router, w_exp_gate, w_exp_up, w_exp_down, final_norm_g):
    bsz, seq, d = x.shape
    n_ctx = ctx.shape[1]
    depth = w_mod.shape[0]
    cos, sins = _rope_tables(seq)
    lb_p = jax.nn.softmax(hgrn_lb_logits.astype(F32), axis=0)
    lower_bounds = jnp.cumsum(lb_p, axis=0) - lb_p[0]
    no_sink = jnp.zeros((N_HEADS,), F32)
    ones128 = jnp.ones((1, LANES), F32)

    xl = x.reshape(bsz * seq, d)
    xc = ctx.reshape(bsz * n_ctx, d)
    for layer in range(depth):
        ctx_out = layer < depth - 1
        mod_l = jax.nn.silu(c) @ w_mod[layer] + b_mod[layer]
        mod_c = jnp.broadcast_to(jax.nn.silu(c_ctx) @ w_mod[layer] + b_mod[layer], (bsz, 6 * d))
        ml = [m.reshape(bsz, 1, d) for m in jnp.split(mod_l, 6, axis=-1)]
        mc = [m.reshape(bsz, 1, d) for m in jnp.split(mod_c, 6, axis=-1)]

        w_in_b = w_in[layer].astype(BF16)
        p_l = inproj(xl, seq, norm1_g[layer], ml[1], ml[0], w_in_b, 1024, 1024)
        p_c = inproj(xc, n_ctx, norm1_g[layer], mc[1], mc[0], w_in_b, 256, 1024)
        p_l3 = p_l.reshape(bsz, seq, D_IN)
        p_c3 = p_c.reshape(bsz, n_ctx, D_IN)

        s0 = jnp.zeros((bsz, 2, HG_HEADS, HG_KDIM, HG_KDIM), F32)
        o_c, s_c = hgrn_scan(p_c3, lower_bounds[layer], s0, 256)
        o_l, _ = hgrn_scan(p_l3, lower_bounds[layer], s_c, 512)

        gq = jnp.tile(attn_q_norm_g[layer], 2).reshape(1, LANES)
        gk = jnp.tile(attn_k_norm_g[layer], 2).reshape(1, LANES)
        kt_l, v_l, kn_l = prep_kv(p_l3, COL["ak"], COL["av"], gk, cos, sins, True, True, 512)
        kt_c, v_c, kn_c = prep_kv(p_c3, COL["ak"], COL["av"], gk, cos, sins, True, False, 256)
        kmax = _key_norm_max([kn_l, kn_c])
        q_l, u_l = prep_q(p_l3, COL["aq"], gq, cos, sins, kmax, True, True, 512)
        kt_all = jnp.concatenate([kt_c, kt_l], axis=3)
        v_all = jnp.concatenate([v_c, v_l], axis=2)
        b_l = attention(q_l, kt_all, v_all, jnp.max(u_l), no_sink, 256, 1280)

        skt_l, sv_l, skn_l = prep_kv(p_l3, COL["sk"], COL["sv"], ones128, cos, sins, False, True, 512)
        skt_c, sv_c, skn_c = prep_kv(p_c3, COL["sk"], COL["sv"], ones128, cos, sins, False, False, 256)
        skmax = _key_norm_max([skn_l, skn_c])
        sq_l, su_l = prep_q(p_l3, COL["sq"], ones128, cos, sins, skmax, False, True, 512)
        zk = jnp.zeros((bsz, N_KV, LANES, WINDOW), BF16)
        zv = jnp.zeros((bsz, N_KV, WINDOW, LANES), BF16)
        skt_all = jnp.concatenate([skt_c, zk, skt_l, zk], axis=3)
        sv_all = jnp.concatenate([sv_c, zv, sv_l, zv], axis=2)
        c_l = attention(sq_l, skt_all, sv_all, jnp.max(su_l), swa_sink[layer], 256, 256,
                        use_sink=True, band=n_ctx)

        wa = w_branch_a[layer].astype(BF16)
        wb = w_branch_b[layer].astype(BF16)
        wc = w_branch_c[layer].astype(BF16)
        wo = w_out[layer].astype(BF16)
        w_route = jnp.concatenate(
            [w_group[layer], w_router[layer], jnp.zeros((d, LANES - N_GROUPS - N_EXPERTS), F32)], axis=1)
        xl, h_l, lg_l = merge(o_l.reshape(2, bsz * seq, HG_W), p_l, b_l.reshape(-1, ATT_W),
                              c_l.reshape(-1, ATT_W), xl, seq, wa, wb, wc, wo, hgrn_out_norm_g[layer],
                              ml[2], norm2_g[layer], ml[4], ml[3], w_route, 256)
        if ctx_out:
            q_c, u_c = prep_q(p_c3, COL["aq"], gq, cos, sins, kmax, True, False, 256)
            b_c = attention(q_c, kt_c, v_c, jnp.max(u_c), no_sink, 256, 256)
            sq_c, su_c = prep_q(p_c3, COL["sq"], ones128, cos, sins, skmax, False, False, 256)
            c_c = attention(sq_c, skt_c, sv_c, jnp.max(su_c), swa_sink[layer], 256, 256, use_sink=True)
            xc, h_c, lg_c = merge(o_c.reshape(2, bsz * n_ctx, HG_W), p_c, b_c.reshape(-1, ATT_W),
                                  c_c.reshape(-1, ATT_W), xc, n_ctx, wa, wb, wc, wo, hgrn_out_norm_g[layer],
                                  mc[2], norm2_g[layer], mc[4], mc[3], w_route, 256)
            h_all = jnp.concatenate([h_c, h_l], axis=0)
            lg_all = jnp.concatenate([lg_c, lg_l], axis=0)
        else:
            h_all, lg_all = h_l, lg_l

        info, counts_row = route(lg_all)
        src_tok, dest, block_expert = _dispatch(info, counts_row)
        out_rows = experts_overlapped(h_all, src_tok, block_expert, w_exp_gate, w_exp_up, w_exp_down, layer)
        r1 = jnp.take(out_rows, dest[:, 0], axis=0, mode="clip")
        r2 = jnp.take(out_rows, dest[:, 1], axis=0, mode="clip")
        n_c = bsz * n_ctx if ctx_out else 0
        if ctx_out:
            xc = combine(xc, r1, r2, info, 0, mc[5], n_ctx)
        xl = combine(xl, r1, r2, info, n_c, ml[5], seq, final_g=final_norm_g if layer == depth - 1 else None)
    return xl.reshape(bsz, seq, d)
```

```python
import functools
import math

import numpy as np
import jax
import jax.numpy as jnp
from jax import lax
from jax.experimental import pallas as pl
from jax.experimental.pallas import tpu as pltpu

F32 = jnp.float32
BF16 = jnp.bfloat16

EPS = 1e-6
MASK_VALUE = -1e30
TINY = 1e-30
GRID_W = 64
ROPE_THETA = 10000.0

HG_HEADS = 4
HG_KDIM = 128
HG_W = HG_HEADS * HG_KDIM
HEAD_DIM = 64
N_HEADS = 8
N_KV = 2
GROUP = N_HEADS // N_KV
ATT_W = N_HEADS * HEAD_DIM
KV_W = N_KV * HEAD_DIM
WINDOW = 128
ATTN_SCALE = HEAD_DIM ** -0.5
N_GROUPS = 4
EXP_PER_GROUP = 8
N_EXPERTS = N_GROUPS * EXP_PER_GROUP
TOP_K = 2

LANES = 128
HG_CHUNK = 128
HG_LEVELS = int(math.log2(HG_CHUNK))
HG_TOTAL_ROWS = 16
MOE_ROWS = 256
ROUTE_ROWS = 512
VMEM_LIMIT = 56 * 1024 * 1024

LOG2E = 1.4426950408889634
SHIFT_HEADROOM = 57.0
SAFE_BOUND = 90.0


def _cparams(sem):
    return pltpu.CompilerParams(dimension_semantics=sem, vmem_limit_bytes=VMEM_LIMIT)


_SIZES = (HG_W, HG_W, HG_W, HG_W, HG_W, ATT_W, KV_W, KV_W, ATT_W, KV_W, KV_W, 3 * 1024)
_NAMES = ("hq", "ff", "fb", "hi", "hg", "aq", "ak", "av", "sq", "sk", "sv", "gates")
COL = {n: int(sum(_SIZES[:i])) for i, n in enumerate(_NAMES)}
D_IN = int(sum(_SIZES))


def _inproj_kernel(x_ref, g_ref, sc_ref, sh_ref, w_ref, o_ref, h_scr):
    @pl.when(pl.program_id(1) == 0)
    def _():
        x = x_ref[...]
        ms = jnp.mean(x * x, axis=-1, keepdims=True)
        y = x * lax.rsqrt(ms + EPS) * g_ref[...]
        h_scr[...] = (y * (1.0 + sc_ref[0]) + sh_ref[0]).astype(BF16)

    o_ref[...] = jnp.dot(h_scr[...], w_ref[...], preferred_element_type=F32).astype(o_ref.dtype)


def inproj(x2d, rows_per_batch, g, scale, shift, w_bf16, tm, tn):
    t, d = x2d.shape
    n = w_bf16.shape[1]
    tm = min(tm, rows_per_batch)
    per = rows_per_batch // tm
    return pl.pallas_call(
        _inproj_kernel,
        out_shape=jax.ShapeDtypeStruct((t, n), BF16),
        grid=(t // tm, n // tn),
        in_specs=[
            pl.BlockSpec((tm, d), lambda i, j: (i, 0)),
            pl.BlockSpec((1, d), lambda i, j: (0, 0)),
            pl.BlockSpec((1, 1, d), lambda i, j: (i // per, 0, 0)),
            pl.BlockSpec((1, 1, d), lambda i, j: (i // per, 0, 0)),
            pl.BlockSpec((d, tn), lambda i, j: (0, j)),
        ],
        out_specs=pl.BlockSpec((tm, tn), lambda i, j: (i, j)),
        scratch_shapes=[pltpu.VMEM((tm, d), BF16)],
        compiler_params=_cparams(("parallel", "arbitrary")),
    )(x2d, g.reshape(1, d), scale, shift, w_bf16)


def _hgrn_exponent_matrices():
    c, nl = HG_CHUNK, HG_LEVELS
    rows = (nl + 2) * c + HG_TOTAL_ROWS
    out = np.zeros((2, rows, c), np.float32)
    for d in range(2):
        pos = np.arange(c) if d == 0 else c - 1 - np.arange(c)
        for l in range(nl):
            m = 1 << l
            for t in range(c):
                p = pos[t]
                mid = (p // (2 * m)) * 2 * m + m
                if p >= mid:
                    sel = (pos >= mid) & (pos <= p)
                else:
                    sel = (pos > p) & (pos <= mid - 1)
                out[d, l * c + t, sel] = 1.0
        for t in range(c):
            out[d, nl * c + t, pos <= pos[t]] = 1.0
            out[d, (nl + 1) * c + t, pos > pos[t]] = 1.0
        out[d, (nl + 2) * c:, :] = 1.0
    return np.concatenate([out, out], axis=2)


_HG_EXP_MATS = _hgrn_exponent_matrices()


def _sigmoid(x):
    return 1.0 / (1.0 + jnp.exp(-x))


def _sigmoid_t(x):
    return 0.5 * jnp.tanh(0.5 * x) + 0.5


def _hgrn_kernel(q_ref, f_ref, i_ref, lb_ref, mat_ref, s0_ref, o_ref, st_ref, *, n_chunks):
    c, nl = HG_CHUNK, HG_LEVELS
    d = pl.program_id(1)

    @pl.when(pl.program_id(2) == 0)
    def _():
        st_ref[...] = s0_ref[...]

    row = lax.broadcasted_iota(jnp.int32, (c, HG_W), 0)
    pos = row + d * (c - 1 - 2 * row)
    r_i = lax.broadcasted_iota(jnp.int32, (c, c), 0)
    c_i = lax.broadcasted_iota(jnp.int32, (c, c), 1)
    xor_rc = r_i ^ c_i
    top_bit = sum((xor_rc >= (1 << j)).astype(jnp.int32) for j in range(1, nl))
    pair_level = jnp.where((r_i - c_i) * (1 - 2 * d) > 0, top_bit, -1)

    def chunk(ci, carry):
        cc = ci + d * (n_chunks - 1 - 2 * ci)
        r0 = pl.multiple_of(cc * c, c)
        qraw = q_ref[0, pl.ds(r0, c), :].astype(F32)
        fz = f_ref[0, pl.ds(r0, c), :].astype(F32)
        vb = i_ref[0, pl.ds(r0, c), :]
        v = vb.astype(F32)
        lb = lb_ref[...]
        sig = _sigmoid(fz)
        logf = jnp.log(jnp.maximum(lb + (1.0 - lb) * sig, TINY))
        key = (1.0 - lb) * (1.0 - sig)
        qh = qraw * _sigmoid(qraw) * (HG_KDIM ** -0.5)
        logf2 = logf * LOG2E
        hi = logf2.astype(BF16)
        lo = (logf2 - hi.astype(F32)).astype(BF16)
        hilo = jnp.concatenate([hi, lo], axis=0)

        dec_all = jnp.exp2(jnp.dot(mat_ref[0], hilo, preferred_element_type=F32))

        def decay(r_lo, r_hi):
            return dec_all[r_lo:r_hi]

        a = [jnp.zeros((c, c), F32)] * HG_HEADS
        for l in range(nl):
            x = (jnp.where(((pos >> l) & 1) == 1, qh, key) * decay(l * c, (l + 1) * c)).astype(BF16)
            for h in range(HG_HEADS):
                xh = x[:, h * LANES:(h + 1) * LANES]
                prod = lax.dot_general(xh, xh, (((1,), (1,)), ((), ())), preferred_element_type=F32)
                a[h] = jnp.where(pair_level == l, prod, a[h])
        q_in = (qh * decay(nl * c, (nl + 1) * c)).astype(BF16)
        k_out = (key * decay((nl + 1) * c, (nl + 2) * c)).astype(BF16)
        total = decay((nl + 2) * c, (nl + 2) * c + HG_TOTAL_ROWS)[0:1]
        qk = qh * key
        for h in range(HG_HEADS):
            lanes = slice(h * LANES, (h + 1) * LANES)
            st = st_ref[0, 0, h]
            o = (jnp.dot(a[h].astype(BF16), vb[:, lanes], preferred_element_type=F32)
                 + jnp.sum(qk[:, lanes], axis=1, keepdims=True) * v[:, lanes]
                 + lax.dot_general(q_in[:, lanes], st.astype(BF16), (((1,), (1,)), ((), ())),
                                   preferred_element_type=F32))
            o_ref[0, 0, pl.ds(r0, c), lanes] = o.astype(o_ref.dtype)
            st_ref[0, 0, h] = total[:, lanes] * st + lax.dot_general(
                vb[:, lanes], k_out[:, lanes], (((0,), (0,)), ((), ())), preferred_element_type=F32)
        return carry

    lax.fori_loop(0, n_chunks, chunk, 0)


def hgrn_scan(p3, lb, s0, tb):
    b, l, _ = p3.shape
    tb = min(tb, l)
    nb = l // tb
    blk = lambda d, n: n + d * (nb - 1 - 2 * n)
    wblk = HG_W
    kern = functools.partial(_hgrn_kernel, n_chunks=tb // HG_CHUNK)
    mats = jnp.asarray(_HG_EXP_MATS, BF16)
    return pl.pallas_call(
        kern,
        out_shape=(jax.ShapeDtypeStruct((2, b, l, HG_W), BF16),
                   jax.ShapeDtypeStruct((b, 2, HG_HEADS, HG_KDIM, HG_KDIM), F32)),
        grid=(b, 2, nb),
        in_specs=[
            pl.BlockSpec((1, tb, wblk), lambda bi, d, n: (bi, blk(d, n), COL["hq"] // wblk)),
            pl.BlockSpec((1, tb, wblk), lambda bi, d, n: (bi, blk(d, n), COL["ff"] // wblk + d)),
            pl.BlockSpec((1, tb, wblk), lambda bi, d, n: (bi, blk(d, n), COL["hi"] // wblk)),
            pl.BlockSpec((1, HG_W), lambda bi, d, n: (0, 0)),
            pl.BlockSpec((1,) + _HG_EXP_MATS.shape[1:], lambda bi, d, n: (d, 0, 0)),
            pl.BlockSpec((1, 1, HG_HEADS, HG_KDIM, HG_KDIM), lambda bi, d, n: (bi, d, 0, 0, 0)),
        ],
        out_specs=(
            pl.BlockSpec((1, 1, tb, HG_W), lambda bi, d, n: (d, bi, blk(d, n), 0)),
            pl.BlockSpec((1, 1, HG_HEADS, HG_KDIM, HG_KDIM), lambda bi, d, n: (bi, d, 0, 0, 0)),
        ),
        compiler_params=_cparams(("parallel", "arbitrary", "arbitrary")),
    )(p3, p3, p3, lb.reshape(1, HG_W), mats, s0)


_GROUP_ONES = np.kron(np.eye(LANES // HEAD_DIM, dtype=np.float32), np.ones((HEAD_DIM, HEAD_DIM), np.float32))
_GROUP_ONES2 = np.concatenate([_GROUP_ONES, _GROUP_ONES], axis=0)


def _group_ssq(x, ones2):
    sq = x * x
    hi = sq.astype(BF16)
    lo = (sq - hi.astype(F32)).astype(BF16)
    return jnp.dot(jnp.concatenate([hi, lo], axis=1), ones2, preferred_element_type=F32)


def _prep_tile(x, g, cos, sins, ones2, norm, rope):
    if norm:
        x = x * lax.rsqrt(_group_ssq(x, ones2) * (1.0 / HEAD_DIM) + EPS) * g
    if rope:
        lane = lax.broadcasted_iota(jnp.int32, x.shape, 1)
        first = (lane % HEAD_DIM) < (HEAD_DIM // 2)
        other = jnp.where(first, pltpu.roll(x, LANES - HEAD_DIM // 2, 1), pltpu.roll(x, HEAD_DIM // 2, 1))
        x = x * cos + other * sins
    return x


def _prep_kv_kernel(k_ref, v_ref, g_ref, cos_ref, sin_ref, ones_ref, kt_ref, vo_ref, kn_ref, *, norm, rope):
    k = _prep_tile(k_ref[0].astype(F32), g_ref[...], cos_ref[...], sin_ref[...], ones_ref[...], norm, rope)
    kb = k.astype(BF16).astype(F32)
    ssq = _group_ssq(kb, ones_ref[...])
    kn_ref[0, 0] = jnp.broadcast_to(jnp.max(ssq, axis=0, keepdims=True), (8, LANES))
    kt = kb.T
    row = lax.broadcasted_iota(jnp.int32, kt.shape, 0)
    v = v_ref[0]
    low = lax.broadcasted_iota(jnp.int32, v.shape, 1) < HEAD_DIM
    one = jnp.ones((), v.dtype)
    for g in range(N_KV):
        ktg = kt if g == 0 else jnp.concatenate([kt[HEAD_DIM:], kt[:HEAD_DIM]], axis=0)
        kt_ref[0, g] = jnp.where(row < HEAD_DIM, ktg, jnp.where(row == HEAD_DIM, 1.0, 0.0)).astype(BF16)
        vg = v if g == 0 else jnp.concatenate([v[:, HEAD_DIM:], v[:, :HEAD_DIM]], axis=1)
        vo_ref[0, g] = jnp.where(low, vg, one).astype(BF16)


def _prep_q_kernel(kmax_ref, x0_ref, x1_ref, x2_ref, x3_ref, g_ref, cos_ref, sin_ref, ones_ref, o_ref, u_ref,
                   *, norm, rope):
    b = pl.program_id(0)
    lane = lax.broadcasted_iota(jnp.int32, (x0_ref.shape[1], LANES), 1)
    umax = None
    for tile, x_ref in enumerate((x0_ref, x1_ref, x2_ref, x3_ref)):
        kv = (2 * tile) // GROUP
        x = x_ref[0].astype(F32)
        y = _prep_tile(x, g_ref[...], cos_ref[...], sin_ref[...], ones_ref[...], norm, rope) * (ATTN_SCALE * LOG2E)
        yb = y.astype(BF16).astype(F32)
        u2 = jnp.sqrt(_group_ssq(yb, ones_ref[...])) * kmax_ref[b * N_KV + kv]
        tmax = jnp.max(u2, axis=0, keepdims=True)
        umax = tmax if umax is None else jnp.maximum(umax, tmax)
        neg_shift = SHIFT_HEADROOM - u2
        y_sw = pltpu.roll(yb, HEAD_DIM, 1)
        ns_sw = pltpu.roll(neg_shift, HEAD_DIM, 1)
        for half in range(2):
            h = 2 * tile + half
            data = yb if half == 0 else y_sw
            ns = ns_sw if half == 0 else neg_shift
            o_ref[0, :, h * LANES:(h + 1) * LANES] = jnp.where(
                lane < HEAD_DIM, data, jnp.where(lane == HEAD_DIM, ns, 0.0)).astype(BF16)
    u_ref[0, 0] = jnp.broadcast_to(umax, (8, LANES))


def prep_kv(p3, col_k, col_v, g128, cos, sins, norm, rope, tq):
    b, l, _ = p3.shape
    tq = min(tq, l)
    ones2 = jnp.asarray(_GROUP_ONES2, BF16)
    return pl.pallas_call(
        functools.partial(_prep_kv_kernel, norm=norm, rope=rope),
        out_shape=(jax.ShapeDtypeStruct((b, N_KV, LANES, l), BF16),
                   jax.ShapeDtypeStruct((b, N_KV, l, LANES), BF16),
                   jax.ShapeDtypeStruct((b, l // tq, 8, LANES), F32)),
        grid=(b, l // tq),
        in_specs=[
            pl.BlockSpec((1, tq, KV_W), lambda bi, i: (bi, i, col_k // KV_W)),
            pl.BlockSpec((1, tq, KV_W), lambda bi, i: (bi, i, col_v // KV_W)),
            pl.BlockSpec((1, LANES), lambda bi, i: (0, 0)),
            pl.BlockSpec((tq, LANES), lambda bi, i: (i, 0)),
            pl.BlockSpec((tq, LANES), lambda bi, i: (i, 0)),
            pl.BlockSpec((2 * LANES, LANES), lambda bi, i: (0, 0)),
        ],
        out_specs=(pl.BlockSpec((1, N_KV, LANES, tq), lambda bi, i: (bi, 0, 0, i)),
                   pl.BlockSpec((1, N_KV, tq, LANES), lambda bi, i: (bi, 0, i, 0)),
                   pl.BlockSpec((1, 1, 8, LANES), lambda bi, i: (bi, i, 0, 0))),
        compiler_params=_cparams(("parallel", "parallel")),
    )(p3, p3, g128, cos, sins, ones2)


def prep_q(p3, col, g128, cos, sins, kmax, norm, rope, tq):
    b, l, _ = p3.shape
    tq = min(tq, l)
    ones2 = jnp.asarray(_GROUP_ONES2, BF16)
    return pl.pallas_call(
        functools.partial(_prep_q_kernel, norm=norm, rope=rope),
        out_shape=(jax.ShapeDtypeStruct((b, l, N_HEADS * LANES), BF16),
                   jax.ShapeDtypeStruct((b, l // tq, 8, LANES), F32)),
        grid_spec=pltpu.PrefetchScalarGridSpec(
            num_scalar_prefetch=1,
            grid=(b, l // tq),
            in_specs=[
                pl.BlockSpec((1, tq, LANES), lambda bi, i, s, t=t: (bi, i, col // LANES + t))
                for t in range(ATT_W // LANES)
            ] + [
                pl.BlockSpec((1, LANES), lambda bi, i, s: (0, 0)),
                pl.BlockSpec((tq, LANES), lambda bi, i, s: (i, 0)),
                pl.BlockSpec((tq, LANES), lambda bi, i, s: (i, 0)),
                pl.BlockSpec((2 * LANES, LANES), lambda bi, i, s: (0, 0)),
            ],
            out_specs=(pl.BlockSpec((1, tq, N_HEADS * LANES), lambda bi, i, s: (bi, i, 0)),
                       pl.BlockSpec((1, 1, 8, LANES), lambda bi, i, s: (bi, i, 0, 0))),
        ),
        compiler_params=_cparams(("parallel", "parallel")),
    )(kmax, p3, p3, p3, p3, g128, cos, sins, ones2)


def _key_norm_max(kn_list):
    kn = functools.reduce(jnp.maximum, [jnp.max(k, axis=(1, 2)) for k in kn_list])
    return jnp.sqrt(kn[:, ::HEAD_DIM]).reshape(-1)


def _attn_kernel(sink_ref, q_ref, kt_ref, v_ref, o_ref, *, tk, online, use_sink, band, seq):
    tq = q_ref.shape[1]
    rows = GROUP * tq
    lane = lax.broadcasted_iota(jnp.int32, (tq, LANES), 1)
    low = lane < HEAD_DIM
    if band is not None:
        width = tq + 2 * WINDOW
        start = pl.multiple_of(pl.program_id(1) * tq, tq)
        r = lax.broadcasted_iota(jnp.int32, (rows, width), 0) % tq
        kp = lax.broadcasted_iota(jnp.int32, (rows, width), 1)
        key_pos = start + kp - WINDOW
        mask = (jnp.abs(kp - WINDOW - r) <= WINDOW) & (key_pos >= 0) & (key_pos < seq)
    for g in range(N_KV):
        tiles = [q_ref[0, :, (g * GROUP + hh) * LANES:(g * GROUP + hh + 1) * LANES] for hh in range(GROUP)]
        qg = jnp.concatenate(tiles, axis=0)

        def scores(kt, qg=qg):
            return jnp.dot(qg, kt, preferred_element_type=F32)

        def update(carry, s, v, msk=None):
            if msk is not None:
                s = jnp.where(msk, s, MASK_VALUE)
            if online:
                m, acc = carry
                m_new = jnp.maximum(m, jnp.max(s, axis=1, keepdims=True))
                p = jnp.exp2(s - m_new).astype(BF16)
                return m_new, jnp.exp2(m - m_new) * acc + jnp.dot(p, v, preferred_element_type=F32)
            return carry + jnp.dot(jnp.exp2(s).astype(BF16), v, preferred_element_type=F32)

        acc0 = jnp.zeros((rows, LANES), F32)
        carry = (jnp.full((rows, 1), MASK_VALUE, F32), acc0) if online else acc0
        if band is None:
            def chunk(c, k0, g=g, scores=scores, update=update):
                return update(c, scores(kt_ref[0, g, :, pl.ds(k0, tk)]), v_ref[0, g, pl.ds(k0, tk), :])

            nk = kt_ref.shape[3] // tk
            for j in range(nk % 2):
                carry = chunk(carry, j * tk)

            def body(j, c, chunk=chunk, first=(nk % 2) * tk):
                k0 = pl.multiple_of(first + j * (2 * tk), tk)
                return chunk(chunk(c, k0), pl.multiple_of(k0 + tk, tk))

            carry = lax.fori_loop(0, nk // 2, body, carry)
        else:
            carry = update(carry, scores(kt_ref[0, g, :, 0:band]), v_ref[0, g, 0:band, :])
            b0 = pl.multiple_of(band + start, LANES)
            carry = update(carry, scores(kt_ref[0, g, :, pl.ds(b0, width)]), v_ref[0, g, pl.ds(b0, width), :], mask)
        acc = carry[1] if online else carry
        if use_sink:
            e = jnp.concatenate([t[:, HEAD_DIM:HEAD_DIM + 1].astype(F32) + sink_ref[g * GROUP + hh] * LOG2E
                                 for hh, t in enumerate(tiles)], axis=0)
            if online:
                e = e - carry[0]
            lane_r = lax.broadcasted_iota(jnp.int32, (rows, LANES), 1)
            acc = acc + jnp.where(lane_r >= HEAD_DIM, jnp.exp2(e), 0.0)
        o = acc * pltpu.roll(1.0 / acc, HEAD_DIM, 1)
        for pair in range(GROUP // 2):
            a = o[(2 * pair) * tq:(2 * pair + 1) * tq]
            b = o[(2 * pair + 1) * tq:(2 * pair + 2) * tq]
            t0 = (g * GROUP // 2 + pair) * LANES
            o_ref[0, :, t0:t0 + LANES] = jnp.where(low, a, pltpu.roll(b, HEAD_DIM, 1)).astype(o_ref.dtype)


def _attention(q, kt, v, sink, tq, tk, online, use_sink, band):
    b, l, _ = q.shape
    sk = kt.shape[3]
    tq = min(tq, l)
    tk = min(tk, sk)
    kern = functools.partial(_attn_kernel, tk=tk, online=online, use_sink=use_sink, band=band, seq=l)
    return pl.pallas_call(
        kern,
        out_shape=jax.ShapeDtypeStruct((b, l, ATT_W), BF16),
        grid_spec=pltpu.PrefetchScalarGridSpec(
            num_scalar_prefetch=1,
            grid=(b, l // tq),
            in_specs=[
                pl.BlockSpec((1, tq, N_HEADS * LANES), lambda bi, i, s: (bi, i, 0)),
                pl.BlockSpec((1, N_KV, LANES, sk), lambda bi, i, s: (bi, 0, 0, 0)),
                pl.BlockSpec((1, N_KV, sk, LANES), lambda bi, i, s: (bi, 0, 0, 0)),
            ],
            out_specs=pl.BlockSpec((1, tq, ATT_W), lambda bi, i, s: (bi, i, 0)),
        ),
        compiler_params=_cparams(("parallel", "arbitrary")),
    )(sink, q, kt, v)


def attention(q, kt, v, ubound, sink, tq, tk, use_sink=False, band=None):
    return lax.cond(ubound <= SAFE_BOUND,
                    lambda: _attention(q, kt, v, sink, tq, tk, False, use_sink, band),
                    lambda: _attention(q, kt, v, sink, tq, tk, True, use_sink, band))


def _merge_kernel(of_ref, ob_ref, hg_ref, b_ref, c_ref, ga_ref, gb_ref, gc_ref, x_ref,
                  wa_ref, wb_ref, wc_ref, wo_ref, hgn_ref, gate1_ref, n2_ref, sc2_ref, sh2_ref, wr_ref,
                  xo_ref, h2_ref, lg_ref):
    o = of_ref[0].astype(F32) + ob_ref[0].astype(F32)
    tiles = []
    for h in range(HG_HEADS):
        t = o[:, h * LANES:(h + 1) * LANES]
        ms = jnp.mean(t * t, axis=-1, keepdims=True)
        tiles.append(t * lax.rsqrt(ms + EPS) * hgn_ref[...])
    hg = hg_ref[...].astype(F32)
    a = (jnp.concatenate(tiles, axis=1) * (hg * _sigmoid_t(hg))).astype(BF16)
    merged = (_sigmoid_t(ga_ref[...].astype(F32)) * jnp.dot(a, wa_ref[...], preferred_element_type=F32)
              + _sigmoid_t(gb_ref[...].astype(F32)) * jnp.dot(b_ref[...], wb_ref[...], preferred_element_type=F32)
              + _sigmoid_t(gc_ref[...].astype(F32)) * jnp.dot(c_ref[...], wc_ref[...], preferred_element_type=F32))
    y = jnp.dot(merged.astype(BF16), wo_ref[...], preferred_element_type=F32)
    x = x_ref[...] + gate1_ref[0] * y
    xo_ref[...] = x
    ms = jnp.mean(x * x, axis=-1, keepdims=True)
    h2 = (x * lax.rsqrt(ms + EPS) * n2_ref[...]) * (1.0 + sc2_ref[0]) + sh2_ref[0]
    hi = h2.astype(BF16)
    lo = (h2 - hi.astype(F32)).astype(BF16)
    h2_ref[...] = hi
    lg_ref[...] = jnp.dot(jnp.concatenate([hi, lo, hi], axis=1), wr_ref[...], preferred_element_type=F32)


def merge(o_fb, p2d, b2d, c2d, x2d, rows_per_batch, wa, wb, wc, wo, hgn, gate1, n2, sc2, sh2, w_route, tm):
    t, d = x2d.shape
    tm = min(tm, rows_per_batch)
    per = rows_per_batch // tm
    row = lambda i: (i, 0)
    const = lambda i: (0, 0)
    bat = lambda i: (i // per, 0, 0)
    gcol = COL["gates"] // d
    return pl.pallas_call(
        _merge_kernel,
        out_shape=(jax.ShapeDtypeStruct((t, d), F32), jax.ShapeDtypeStruct((t, d), BF16),
                   jax.ShapeDtypeStruct((t, LANES), F32)),
        grid=(t // tm,),
        in_specs=[
            pl.BlockSpec((1, tm, HG_W), lambda i: (0, i, 0)),
            pl.BlockSpec((1, tm, HG_W), lambda i: (1, i, 0)),
            pl.BlockSpec((tm, HG_W), lambda i: (i, COL["hg"] // HG_W)),
            pl.BlockSpec((tm, ATT_W), row),
            pl.BlockSpec((tm, ATT_W), row),
            pl.BlockSpec((tm, d), lambda i: (i, gcol)),
            pl.BlockSpec((tm, d), lambda i: (i, gcol + 1)),
            pl.BlockSpec((tm, d), lambda i: (i, gcol + 2)),
            pl.BlockSpec((tm, d), row),
            pl.BlockSpec((HG_W, d), const),
            pl.BlockSpec((ATT_W, d), const),
            pl.BlockSpec((ATT_W, d), const),
            pl.BlockSpec((d, d), const),
            pl.BlockSpec((1, LANES), const),
            pl.BlockSpec((1, 1, d), bat),
            pl.BlockSpec((1, d), const),
            pl.BlockSpec((1, 1, d), bat),
            pl.BlockSpec((1, 1, d), bat),
            pl.BlockSpec((3 * d, LANES), const),
        ],
        out_specs=(pl.BlockSpec((tm, d), row), pl.BlockSpec((tm, d), row), pl.BlockSpec((tm, LANES), row)),
        compiler_params=_cparams(("parallel",)),
    )(o_fb, o_fb, p2d, b2d, c2d, p2d, p2d, p2d, x2d, wa, wb, wc, wo, hgn.reshape(1, LANES),
      gate1, n2.reshape(1, d), sc2, sh2, w_route)


_STRICT_LOWER = np.tril(np.ones((ROUTE_ROWS, ROUTE_ROWS), np.float32), -1)
_NEG_BIG = -3.0e38


def _lane_argmax(x, lane):
    top = jnp.max(x, axis=1, keepdims=True)
    idx = jnp.min(jnp.where(x == top, lane, LANES), axis=1, keepdims=True)
    return top, idx


def _route_kernel(lg_ref, tri_ref, o_ref, cnt_ref, run_scr):
    @pl.when(pl.program_id(0) == 0)
    def _():
        run_scr[...] = jnp.zeros_like(run_scr)

    lg = lg_ref[...]
    lane = lax.broadcasted_iota(jnp.int32, lg.shape, 1)
    is_grp = lane < N_GROUPS
    gtop, gidx = _lane_argmax(jnp.where(is_grp, lg, _NEG_BIG), lane)
    grp_w = 1.0 / jnp.sum(jnp.where(is_grp, jnp.exp(lg - gtop), 0.0), axis=1, keepdims=True)
    lo = N_GROUPS + EXP_PER_GROUP * gidx
    x1 = jnp.where((lane >= lo) & (lane < lo + EXP_PER_GROUP), lg, _NEG_BIG)
    t1, i1 = _lane_argmax(x1, lane)
    t2, i2 = _lane_argmax(jnp.where(lane == i1, _NEG_BIG, x1), lane)
    r = jnp.exp(t2 - t1)
    w1 = grp_w / (1.0 + r)
    w2 = w1 * r
    e1 = i1 - N_GROUPS
    e2 = i2 - N_GROUPS
    hot1 = lane == e1
    hot2 = lane == e2
    hot = jnp.where(hot1 | hot2, 1.0, 0.0)
    before = run_scr[...] + jnp.dot(tri_ref[...], hot.astype(BF16), preferred_element_type=F32)
    rank1 = jnp.sum(jnp.where(hot1, before, 0.0), axis=1, keepdims=True)
    rank2 = jnp.sum(jnp.where(hot2, before, 0.0), axis=1, keepdims=True)
    run_scr[...] = run_scr[...] + jnp.sum(hot, axis=0, keepdims=True)
    cnt_ref[...] = jnp.broadcast_to(run_scr[...], cnt_ref.shape)
    out = jnp.where(lane == 0, e1.astype(F32), jnp.where(lane == 1, e2.astype(F32), 0.0))
    out = jnp.where(lane == 2, w1, jnp.where(lane == 3, w2, out))
    o_ref[...] = jnp.where(lane == 4, rank1, jnp.where(lane == 5, rank2, out))


def route(logits):
    t = logits.shape[0]
    return pl.pallas_call(
        _route_kernel,
        out_shape=(jax.ShapeDtypeStruct((t, LANES), F32), jax.ShapeDtypeStruct((8, LANES), F32)),
        grid=(t // ROUTE_ROWS,),
        in_specs=[pl.BlockSpec((ROUTE_ROWS, LANES), lambda i: (i, 0)),
                  pl.BlockSpec((ROUTE_ROWS, ROUTE_ROWS), lambda i: (0, 0))],
        out_specs=(pl.BlockSpec((ROUTE_ROWS, LANES), lambda i: (i, 0)),
                   pl.BlockSpec((8, LANES), lambda i: (0, 0))),
        scratch_shapes=[pltpu.VMEM((1, LANES), F32)],
        compiler_params=_cparams(("arbitrary",)),
    )(logits, jnp.asarray(_STRICT_LOWER, BF16))


def _expert_kernel(be_ref, x_ref, wg_ref, wu_ref, wd_ref, *rest):
    o_ref, wg_s, wu_s, wd_s = rest[-4:]
    i = pl.program_id(0)

    @pl.when((i == 0) | (be_ref[i] != be_ref[jnp.maximum(i - 1, 0)]))
    def _():
        wg_s[...] = wg_ref[0, 0].astype(BF16)
        wu_s[...] = wu_ref[0, 0].astype(BF16)
        wd_s[...] = wd_ref[0, 0].astype(BF16)

    x = x_ref[...]
    gte = jnp.dot(x, wg_s[...], preferred_element_type=F32)
    up = jnp.dot(x, wu_s[...], preferred_element_type=F32)
    hid = (gte * _sigmoid(gte) * up).astype(BF16)
    o_ref[...] = jnp.dot(hid, wd_s[...], preferred_element_type=F32).astype(o_ref.dtype)


def expert_ffn(rows, block_expert, wg, wu, wd, layer, out_prev, block_off, n_total):
    n, d = rows.shape
    de = wg.shape[3]
    in_specs = [
        pl.BlockSpec((MOE_ROWS, d), lambda i, be: (i, 0)),
        pl.BlockSpec((1, 1, d, de), lambda i, be: (layer, be[i], 0, 0)),
        pl.BlockSpec((1, 1, d, de), lambda i, be: (layer, be[i], 0, 0)),
        pl.BlockSpec((1, 1, de, d), lambda i, be: (layer, be[i], 0, 0)),
    ]
    args = [block_expert, rows, wg, wu, wd]
    aliases = {}
    if out_prev is not None:
        in_specs.append(pl.BlockSpec(memory_space=pl.ANY))
        args.append(out_prev)
        aliases = {5: 0}
    return pl.pallas_call(
        _expert_kernel,
        out_shape=jax.ShapeDtypeStruct((n_total, d), BF16),
        grid_spec=pltpu.PrefetchScalarGridSpec(
            num_scalar_prefetch=1,
            grid=(n // MOE_ROWS,),
            in_specs=in_specs,
            out_specs=pl.BlockSpec((MOE_ROWS, d), lambda i, be: (i + block_off, 0)),
            scratch_shapes=[pltpu.VMEM((d, de), BF16), pltpu.VMEM((d, de), BF16), pltpu.VMEM((de, d), BF16)],
        ),
        input_output_aliases=aliases,
        compiler_params=_cparams(("arbitrary",)),
    )(*args)


MOE_PARTS = 4


def experts_overlapped(h_all, src_tok, block_expert, wg, wu, wd, layer):
    n_blocks = block_expert.shape[0]
    n_total = n_blocks * MOE_ROWS
    bounds = [n_blocks * k // MOE_PARTS for k in range(MOE_PARTS + 1)]
    out = None
    for b0, b1 in zip(bounds[:-1], bounds[1:]):
        rows = jnp.take(h_all, src_tok[b0 * MOE_ROWS:b1 * MOE_ROWS], axis=0, mode="clip")
        out = expert_ffn(rows, block_expert[b0:b1], wg, wu, wd, layer, out, b0, n_total)
    return out


def _combine_kernel(x_ref, r1_ref, r2_ref, info_ref, gate_ref, *rest, final):
    o_ref = rest[-1]
    info = info_ref[...]
    y = info[:, 2:3] * r1_ref[...].astype(F32) + info[:, 3:4] * r2_ref[...].astype(F32)
    x = x_ref[...] + gate_ref[0] * y
    if final:
        ms = jnp.mean(x * x, axis=-1, keepdims=True)
        x = x * lax.rsqrt(ms + EPS) * rest[0][...]
    o_ref[...] = x


def combine(x2d, r1, r2, info, row_off, gate, rows_per_batch, final_g=None):
    t, d = x2d.shape
    tm = min(ROUTE_ROWS, rows_per_batch)
    per = rows_per_batch // tm
    off = row_off // tm
    row = lambda i: (i, 0)
    shifted = lambda i: (i + off, 0)
    specs = [pl.BlockSpec((tm, d), row), pl.BlockSpec((tm, d), shifted), pl.BlockSpec((tm, d), shifted),
             pl.BlockSpec((tm, LANES), shifted), pl.BlockSpec((1, 1, d), lambda i: (i // per, 0, 0))]
    args = [x2d, r1, r2, info, gate]
    if final_g is not None:
        specs.append(pl.BlockSpec((1, d), lambda i: (0, 0)))
        args.append(final_g.reshape(1, d))
    return pl.pallas_call(
        functools.partial(_combine_kernel, final=final_g is not None),
        out_shape=jax.ShapeDtypeStruct((t, d), F32), grid=(t // tm,),
        in_specs=specs, out_specs=pl.BlockSpec((tm, d), row),
        compiler_params=_cparams(("parallel",)),
    )(*args)


def _dispatch(info, counts_row):
    n_tok = info.shape[0]
    expert = info[:, 0:TOP_K].astype(jnp.int32)
    rank = info[:, 4:4 + TOP_K].astype(jnp.int32)
    counts = counts_row[0, :N_EXPERTS].astype(jnp.int32)
    padded = (counts + MOE_ROWS - 1) // MOE_ROWS * MOE_ROWS
    pad_end = jnp.cumsum(padded)
    pad_start = pad_end - padded
    hot = expert[:, :, None] == jnp.arange(N_EXPERTS, dtype=jnp.int32)
    dest = jnp.sum(jnp.where(hot, pad_start, 0), axis=-1) + rank
    n_blocks = -(-(n_tok * TOP_K + N_EXPERTS * (MOE_ROWS - 1)) // MOE_ROWS)
    token = jnp.broadcast_to(jnp.arange(n_tok, dtype=jnp.int32)[:, None], dest.shape)
    src_tok = jnp.zeros((n_blocks * MOE_ROWS,), jnp.int32).at[dest.reshape(-1)].set(token.reshape(-1))
    block_start = jnp.arange(n_blocks, dtype=jnp.int32)[:, None] * MOE_ROWS
    block_expert = jnp.minimum(jnp.sum((pad_end[None, :] <= block_start).astype(jnp.int32), axis=1),
                               N_EXPERTS - 1)
    return src_tok, dest, block_expert


def _rope_tables(seq):
    n_rows = seq // GRID_W
    row = jnp.repeat(jnp.arange(n_rows), GRID_W).astype(F32)
    col = jnp.tile(jnp.arange(GRID_W), n_rows).astype(F32)
    axis_pairs = HEAD_DIM // 4
    inv = ROPE_THETA ** (-jnp.arange(axis_pairs, dtype=F32) / axis_pairs)
    ang = jnp.concatenate([row[:, None] * inv, col[:, None] * inv], axis=-1)
    cos, sin = jnp.cos(ang), jnp.sin(ang)
    cos128 = jnp.concatenate([cos, cos, cos, cos], axis=-1)
    sins128 = jnp.concatenate([-sin, sin, -sin, sin], axis=-1)
    return cos128, sins128


def kernel(x, c, ctx, c_ctx, w_mod, b_mod, norm1_g, norm2_g, w_in, hgrn_lb_logits, hgrn_out_norm_g,
           attn_q_norm_g, attn_k_norm_g, swa_sink, w_branch_a, w_branch_b, w_branch_c, w_out,
           w_group, w_router, w_exp_gate, w_exp_up, w_exp_down, final_norm_g):
    bsz, seq, d = x.shape
    n_ctx = ctx.shape[1]
    depth = w_mod.shape[0]
    cos, sins = _rope_tables(seq)
    lb_p = jax.nn.softmax(hgrn_lb_logits.astype(F32), axis=0)
    lower_bounds = jnp.cumsum(lb_p, axis=0) - lb_p[0]
    no_sink = jnp.zeros((N_HEADS,), F32)
    ones128 = jnp.ones((1, LANES), F32)

    xl = x.reshape(bsz * seq, d)
    xc = ctx.reshape(bsz * n_ctx, d)
    for layer in range(depth):
        ctx_out = layer < depth - 1
        mod_l = jax.nn.silu(c) @ w_mod[layer] + b_mod[layer]
        mod_c = jnp.broadcast_to(jax.nn.silu(c_ctx) @ w_mod[layer] + b_mod[layer], (bsz, 6 * d))
        ml = [m.reshape(bsz, 1, d) for m in jnp.split(mod_l, 6, axis=-1)]
        mc = [m.reshape(bsz, 1, d) for m in jnp.split(mod_c, 6, axis=-1)]

        w_in_b = w_in[layer].astype(BF16)
        p_l = inproj(xl, seq, norm1_g[layer], ml[1], ml[0], w_in_b, 1024, 1024)
        p_c = inproj(xc, n_ctx, norm1_g[layer], mc[1], mc[0], w_in_b, 256, 1024)
        p_l3 = p_l.reshape(bsz, seq, D_IN)
        p_c3 = p_c.reshape(bsz, n_ctx, D_IN)

        s0 = jnp.zeros((bsz, 2, HG_HEADS, HG_KDIM, HG_KDIM), F32)
        o_c, s_c = hgrn_scan(p_c3, lower_bounds[layer], s0, 256)
        o_l, _ = hgrn_scan(p_l3, lower_bounds[layer], s_c, 512)

        gq = jnp.tile(attn_q_norm_g[layer], 2).reshape(1, LANES)
        gk = jnp.tile(attn_k_norm_g[layer], 2).reshape(1, LANES)
        kt_l, v_l, kn_l = prep_kv(p_l3, COL["ak"], COL["av"], gk, cos, sins, True, True, 512)
        kt_c, v_c, kn_c = prep_kv(p_c3, COL["ak"], COL["av"], gk, cos, sins, True, False, 256)
        kmax = _key_norm_max([kn_l, kn_c])
        q_l, u_l = prep_q(p_l3, COL["aq"], gq, cos, sins, kmax, True, True, 512)
        kt_all = jnp.concatenate([kt_c, kt_l], axis=3)
        v_all = jnp.concatenate([v_c, v_l], axis=2)
        b_l = attention(q_l, kt_all, v_all, jnp.max(u_l), no_sink, 256, 1280)

        skt_l, sv_l, skn_l = prep_kv(p_l3, COL["sk"], COL["sv"], ones128, cos, sins, False, True, 512)
        skt_c, sv_c, skn_c = prep_kv(p_c3, COL["sk"], COL["sv"], ones128, cos, sins, False, False, 256)
        skmax = _key_norm_max([skn_l, skn_c])
        sq_l, su_l = prep_q(p_l3, COL["sq"], ones128, cos, sins, skmax, False, True, 512)
        zk = jnp.zeros((bsz, N_KV, LANES, WINDOW), BF16)
        zv = jnp.zeros((bsz, N_KV, WINDOW, LANES), BF16)
        skt_all = jnp.concatenate([skt_c, zk, skt_l, zk], axis=3)
        sv_all = jnp.concatenate([sv_c, zv, sv_l, zv], axis=2)
        c_l = attention(sq_l, skt_all, sv_all, jnp.max(su_l), swa_sink[layer], 256, 256,
                        use_sink=True, band=n_ctx)

        wa = w_branch_a[layer].astype(BF16)
        wb = w_branch_b[layer].astype(BF16)
        wc = w_branch_c[layer].astype(BF16)
        wo = w_out[layer].astype(BF16)
        w_r = jnp.concatenate(
            [w_group[layer], w_router[layer], jnp.zeros((d, LANES - N_GROUPS - N_EXPERTS), F32)], axis=1)
        w_r_hi = w_r.astype(BF16)
        w_r_lo = (w_r - w_r_hi.astype(F32)).astype(BF16)
        w_route = jnp.concatenate([w_r_hi, w_r_hi, w_r_lo], axis=0)
        xl, h_l, lg_l = merge(o_l.reshape(2, bsz * seq, HG_W), p_l, b_l.reshape(-1, ATT_W),
                              c_l.reshape(-1, ATT_W), xl, seq, wa, wb, wc, wo, hgrn_out_norm_g[layer],
                              ml[2], norm2_g[layer], ml[4], ml[3], w_route, 256)
        if ctx_out:
            q_c, u_c = prep_q(p_c3, COL["aq"], gq, cos, sins, kmax, True, False, 256)
            b_c = attention(q_c, kt_c, v_c, jnp.max(u_c), no_sink, 256, 256)
            sq_c, su_c = prep_q(p_c3, COL["sq"], ones128, cos, sins, skmax, False, False, 256)
            c_c = attention(sq_c, skt_c, sv_c, jnp.max(su_c), swa_sink[layer], 256, 256, use_sink=True)
            xc, h_c, lg_c = merge(o_c.reshape(2, bsz * n_ctx, HG_W), p_c, b_c.reshape(-1, ATT_W),
                                  c_c.reshape(-1, ATT_W), xc, n_ctx, wa, wb, wc, wo, hgrn_out_norm_g[layer],
                                  mc[2], norm2_g[layer], mc[4], mc[3], w_route, 256)
            h_all = jnp.concatenate([h_c, h_l], axis=0)
            lg_all = jnp.concatenate([lg_c, lg_l], axis=0)
        else:
            h_all, lg_all = h_l, lg_l

        info, counts_row = route(lg_all)
        src_tok, dest, block_expert = _dispatch(info, counts_row)
        out_rows = experts_overlapped(h_all, src_tok, block_expert, w_exp_gate, w_exp_up, w_exp_down, layer)
        r1 = jnp.take(out_rows, dest[:, 0], axis=0, mode="clip")
        r2 = jnp.take(out_rows, dest[:, 1], axis=0, mode="clip")
        n_c = bsz * n_ctx if ctx_out else 0
        if ctx_out:
            xc = combine(xc, r1, r2, info, 0, mc[5], n_ctx)
        xl = combine(xl, r1, r2, info, n_c, ml[5], seq, final_g=final_norm_g if layer == depth - 1 else None)
    return xl.reshape(bsz, seq, d)
```

```python
import functools
import math

import numpy as np
import jax
import jax.numpy as jnp
from jax import lax
from jax.experimental import pallas as pl
from jax.experimental.pallas import tpu as pltpu

F32 = jnp.float32
BF16 = jnp.bfloat16

EPS = 1e-6
MASK_VALUE = -1e30
TINY = 1e-30
GRID_W = 64
ROPE_THETA = 10000.0

HG_HEADS = 4
HG_KDIM = 128
HG_W = HG_HEADS * HG_KDIM
HEAD_DIM = 64
N_HEADS = 8
N_KV = 2
GROUP = N_HEADS // N_KV
ATT_W = N_HEADS * HEAD_DIM
KV_W = N_KV * HEAD_DIM
WINDOW = 128
ATTN_SCALE = HEAD_DIM ** -0.5
N_GROUPS = 4
EXP_PER_GROUP = 8
N_EXPERTS = N_GROUPS * EXP_PER_GROUP
TOP_K = 2

LANES = 128
HG_CHUNK = 128
HG_LEVELS = int(math.log2(HG_CHUNK))
HG_TOTAL_ROWS = 16
MOE_ROWS = 256
ROUTE_ROWS = 512
VMEM_LIMIT = 56 * 1024 * 1024

LOG2E = 1.4426950408889634
SHIFT_HEADROOM = 57.0
SAFE_BOUND = 90.0


def _cparams(sem):
    return pltpu.CompilerParams(dimension_semantics=sem, vmem_limit_bytes=VMEM_LIMIT)


_SIZES = (HG_W, HG_W, HG_W, HG_W, HG_W, ATT_W, KV_W, KV_W, ATT_W, KV_W, KV_W, 3 * 1024)
_NAMES = ("hq", "ff", "fb", "hi", "hg", "aq", "ak", "av", "sq", "sk", "sv", "gates")
COL = {n: int(sum(_SIZES[:i])) for i, n in enumerate(_NAMES)}
D_IN = int(sum(_SIZES))


def _inproj_kernel(x_ref, g_ref, sc_ref, sh_ref, w_ref, o_ref, h_scr):
    @pl.when(pl.program_id(1) == 0)
    def _():
        x = x_ref[...]
        ms = jnp.mean(x * x, axis=-1, keepdims=True)
        y = x * lax.rsqrt(ms + EPS) * g_ref[...]
        h_scr[...] = (y * (1.0 + sc_ref[0]) + sh_ref[0]).astype(BF16)

    o_ref[...] = jnp.dot(h_scr[...], w_ref[...], preferred_element_type=F32).astype(o_ref.dtype)


def inproj(x2d, rows_per_batch, g, scale, shift, w_bf16, tm, tn):
    t, d = x2d.shape
    n = w_bf16.shape[1]
    tm = min(tm, rows_per_batch)
    per = rows_per_batch // tm
    return pl.pallas_call(
        _inproj_kernel,
        out_shape=jax.ShapeDtypeStruct((t, n), BF16),
        grid=(t // tm, n // tn),
        in_specs=[
            pl.BlockSpec((tm, d), lambda i, j: (i, 0)),
            pl.BlockSpec((1, d), lambda i, j: (0, 0)),
            pl.BlockSpec((1, 1, d), lambda i, j: (i // per, 0, 0)),
            pl.BlockSpec((1, 1, d), lambda i, j: (i // per, 0, 0)),
            pl.BlockSpec((d, tn), lambda i, j: (0, j)),
        ],
        out_specs=pl.BlockSpec((tm, tn), lambda i, j: (i, j)),
        scratch_shapes=[pltpu.VMEM((tm, d), BF16)],
        compiler_params=_cparams(("parallel", "arbitrary")),
    )(x2d, g.reshape(1, d), scale, shift, w_bf16)


def _hgrn_exponent_matrices():
    c, nl = HG_CHUNK, HG_LEVELS
    rows = (nl + 2) * c + HG_TOTAL_ROWS
    out = np.zeros((2, rows, c), np.float32)
    for d in range(2):
        pos = np.arange(c) if d == 0 else c - 1 - np.arange(c)
        for l in range(nl):
            m = 1 << l
            for t in range(c):
                p = pos[t]
                mid = (p // (2 * m)) * 2 * m + m
                if p >= mid:
                    sel = (pos >= mid) & (pos <= p)
                else:
                    sel = (pos > p) & (pos <= mid - 1)
                out[d, l * c + t, sel] = 1.0
        for t in range(c):
            out[d, nl * c + t, pos <= pos[t]] = 1.0
            out[d, (nl + 1) * c + t, pos > pos[t]] = 1.0
        out[d, (nl + 2) * c:, :] = 1.0
    return np.concatenate([out, out], axis=2)


_HG_EXP_MATS = _hgrn_exponent_matrices()


def _sigmoid(x):
    return 1.0 / (1.0 + jnp.exp(-x))


def _sigmoid_t(x):
    return 0.5 * jnp.tanh(0.5 * x) + 0.5


def _hgrn_kernel(q_ref, f_ref, i_ref, lb_ref, mat_ref, s0_ref, o_ref, st_ref, *, n_chunks):
    c, nl = HG_CHUNK, HG_LEVELS
    d = pl.program_id(1)

    @pl.when(pl.program_id(2) == 0)
    def _():
        st_ref[...] = s0_ref[...]

    row = lax.broadcasted_iota(jnp.int32, (c, HG_W), 0)
    pos = row + d * (c - 1 - 2 * row)
    r_i = lax.broadcasted_iota(jnp.int32, (c, c), 0)
    c_i = lax.broadcasted_iota(jnp.int32, (c, c), 1)
    xor_rc = r_i ^ c_i
    top_bit = sum((xor_rc >= (1 << j)).astype(jnp.int32) for j in range(1, nl))
    pair_level = jnp.where((r_i - c_i) * (1 - 2 * d) > 0, top_bit, -1)

    def chunk(ci, carry):
        cc = ci + d * (n_chunks - 1 - 2 * ci)
        r0 = pl.multiple_of(cc * c, c)
        qraw = q_ref[0, pl.ds(r0, c), :].astype(F32)
        fz = f_ref[0, pl.ds(r0, c), :].astype(F32)
        vb = i_ref[0, pl.ds(r0, c), :]
        v = vb.astype(F32)
        lb = lb_ref[...]
        sig = _sigmoid(fz)
        logf = jnp.log(jnp.maximum(lb + (1.0 - lb) * sig, TINY))
        key = (1.0 - lb) * (1.0 - sig)
        qh = qraw * _sigmoid(qraw) * (HG_KDIM ** -0.5)
        logf2 = logf * LOG2E
        hi = logf2.astype(BF16)
        lo = (logf2 - hi.astype(F32)).astype(BF16)
        hilo = jnp.concatenate([hi, lo], axis=0)

        dec_all = jnp.exp2(jnp.dot(mat_ref[0], hilo, preferred_element_type=F32))

        def decay(r_lo, r_hi):
            return dec_all[r_lo:r_hi]

        a = [jnp.zeros((c, c), F32)] * HG_HEADS
        for l in range(nl):
            x = (jnp.where(((pos >> l) & 1) == 1, qh, key) * decay(l * c, (l + 1) * c)).astype(BF16)
            for h in range(HG_HEADS):
                xh = x[:, h * LANES:(h + 1) * LANES]
                prod = lax.dot_general(xh, xh, (((1,), (1,)), ((), ())), preferred_element_type=F32)
                a[h] = jnp.where(pair_level == l, prod, a[h])
        q_in = (qh * decay(nl * c, (nl + 1) * c)).astype(BF16)
        k_out = (key * decay((nl + 1) * c, (nl + 2) * c)).astype(BF16)
        total = decay((nl + 2) * c, (nl + 2) * c + HG_TOTAL_ROWS)[0:1]
        qk = qh * key
        for h in range(HG_HEADS):
            lanes = slice(h * LANES, (h + 1) * LANES)
            st = st_ref[0, 0, h]
            o = (jnp.dot(a[h].astype(BF16), vb[:, lanes], preferred_element_type=F32)
                 + jnp.sum(qk[:, lanes], axis=1, keepdims=True) * v[:, lanes]
                 + lax.dot_general(q_in[:, lanes], st.astype(BF16), (((1,), (1,)), ((), ())),
                                   preferred_element_type=F32))
            o_ref[0, 0, pl.ds(r0, c), lanes] = o.astype(o_ref.dtype)
            st_ref[0, 0, h] = total[:, lanes] * st + lax.dot_general(
                vb[:, lanes], k_out[:, lanes], (((0,), (0,)), ((), ())), preferred_element_type=F32)
        return carry

    lax.fori_loop(0, n_chunks, chunk, 0)


def hgrn_scan(p3, lb, s0, tb):
    b, l, _ = p3.shape
    tb = min(tb, l)
    nb = l // tb
    blk = lambda d, n: n + d * (nb - 1 - 2 * n)
    wblk = HG_W
    kern = functools.partial(_hgrn_kernel, n_chunks=tb // HG_CHUNK)
    mats = jnp.asarray(_HG_EXP_MATS, BF16)
    return pl.pallas_call(
        kern,
        out_shape=(jax.ShapeDtypeStruct((2, b, l, HG_W), BF16),
                   jax.ShapeDtypeStruct((b, 2, HG_HEADS, HG_KDIM, HG_KDIM), F32)),
        grid=(b, 2, nb),
        in_specs=[
            pl.BlockSpec((1, tb, wblk), lambda bi, d, n: (bi, blk(d, n), COL["hq"] // wblk)),
            pl.BlockSpec((1, tb, wblk), lambda bi, d, n: (bi, blk(d, n), COL["ff"] // wblk + d)),
            pl.BlockSpec((1, tb, wblk), lambda bi, d, n: (bi, blk(d, n), COL["hi"] // wblk)),
            pl.BlockSpec((1, HG_W), lambda bi, d, n: (0, 0)),
            pl.BlockSpec((1,) + _HG_EXP_MATS.shape[1:], lambda bi, d, n: (d, 0, 0)),
            pl.BlockSpec((1, 1, HG_HEADS, HG_KDIM, HG_KDIM), lambda bi, d, n: (bi, d, 0, 0, 0)),
        ],
        out_specs=(
            pl.BlockSpec((1, 1, tb, HG_W), lambda bi, d, n: (d, bi, blk(d, n), 0)),
            pl.BlockSpec((1, 1, HG_HEADS, HG_KDIM, HG_KDIM), lambda bi, d, n: (bi, d, 0, 0, 0)),
        ),
        compiler_params=_cparams(("parallel", "arbitrary", "arbitrary")),
    )(p3, p3, p3, lb.reshape(1, HG_W), mats, s0)


_GROUP_ONES = np.kron(np.eye(LANES // HEAD_DIM, dtype=np.float32), np.ones((HEAD_DIM, HEAD_DIM), np.float32))
_GROUP_ONES2 = np.concatenate([_GROUP_ONES, _GROUP_ONES], axis=0)


def _group_ssq(x, ones2):
    sq = x * x
    hi = sq.astype(BF16)
    lo = (sq - hi.astype(F32)).astype(BF16)
    return jnp.dot(jnp.concatenate([hi, lo], axis=1), ones2, preferred_element_type=F32)


def _prep_tile(x, g, cos, sins, ones2, norm, rope):
    if norm:
        x = x * lax.rsqrt(_group_ssq(x, ones2) * (1.0 / HEAD_DIM) + EPS) * g
    if rope:
        lane = lax.broadcasted_iota(jnp.int32, x.shape, 1)
        first = (lane % HEAD_DIM) < (HEAD_DIM // 2)
        other = jnp.where(first, pltpu.roll(x, LANES - HEAD_DIM // 2, 1), pltpu.roll(x, HEAD_DIM // 2, 1))
        x = x * cos + other * sins
    return x


def _prep_kv_kernel(k_ref, v_ref, g_ref, cos_ref, sin_ref, ones_ref, kt_ref, vo_ref, kn_ref, *, norm, rope):
    k = _prep_tile(k_ref[0].astype(F32), g_ref[...], cos_ref[...], sin_ref[...], ones_ref[...], norm, rope)
    kb = k.astype(BF16).astype(F32)
    ssq = _group_ssq(kb, ones_ref[...])
    kn_ref[0, 0] = jnp.broadcast_to(jnp.max(ssq, axis=0, keepdims=True), (8, LANES))
    kt = kb.T
    row = lax.broadcasted_iota(jnp.int32, kt.shape, 0)
    v = v_ref[0]
    low = lax.broadcasted_iota(jnp.int32, v.shape, 1) < HEAD_DIM
    one = jnp.ones((), v.dtype)
    for g in range(N_KV):
        ktg = kt if g == 0 else jnp.concatenate([kt[HEAD_DIM:], kt[:HEAD_DIM]], axis=0)
        kt_ref[0, g] = jnp.where(row < HEAD_DIM, ktg, jnp.where(row == HEAD_DIM, 1.0, 0.0)).astype(BF16)
        vg = v if g == 0 else jnp.concatenate([v[:, HEAD_DIM:], v[:, :HEAD_DIM]], axis=1)
        vo_ref[0, g] = jnp.where(low, vg, one).astype(BF16)


def _prep_q_kernel(kmax_ref, x0_ref, x1_ref, x2_ref, x3_ref, g_ref, cos_ref, sin_ref, ones_ref, o_ref, u_ref,
                   *, norm, rope):
    b = pl.program_id(0)
    lane = lax.broadcasted_iota(jnp.int32, (x0_ref.shape[1], LANES), 1)
    umax = None
    for tile, x_ref in enumerate((x0_ref, x1_ref, x2_ref, x3_ref)):
        kv = (2 * tile) // GROUP
        x = x_ref[0].astype(F32)
        y = _prep_tile(x, g_ref[...], cos_ref[...], sin_ref[...], ones_ref[...], norm, rope) * (ATTN_SCALE * LOG2E)
        yb = y.astype(BF16).astype(F32)
        u2 = jnp.sqrt(_group_ssq(yb, ones_ref[...])) * kmax_ref[b * N_KV + kv]
        tmax = jnp.max(u2, axis=0, keepdims=True)
        umax = tmax if umax is None else jnp.maximum(umax, tmax)
        neg_shift = SHIFT_HEADROOM - u2
        y_sw = pltpu.roll(yb, HEAD_DIM, 1)
        ns_sw = pltpu.roll(neg_shift, HEAD_DIM, 1)
        for half in range(2):
            h = 2 * tile + half
            data = yb if half == 0 else y_sw
            ns = ns_sw if half == 0 else neg_shift
            o_ref[0, :, h * LANES:(h + 1) * LANES] = jnp.where(
                lane < HEAD_DIM, data, jnp.where(lane == HEAD_DIM, ns, 0.0)).astype(BF16)
    u_ref[0, 0] = jnp.broadcast_to(umax, (8, LANES))


def prep_kv(p3, col_k, col_v, g128, cos, sins, norm, rope, tq):
    b, l, _ = p3.shape
    tq = min(tq, l)
    ones2 = jnp.asarray(_GROUP_ONES2, BF16)
    return pl.pallas_call(
        functools.partial(_prep_kv_kernel, norm=norm, rope=rope),
        out_shape=(jax.ShapeDtypeStruct((b, N_KV, LANES, l), BF16),
                   jax.ShapeDtypeStruct((b, N_KV, l, LANES), BF16),
                   jax.ShapeDtypeStruct((b, l // tq, 8, LANES), F32)),
        grid=(b, l // tq),
        in_specs=[
            pl.BlockSpec((1, tq, KV_W), lambda bi, i: (bi, i, col_k // KV_W)),
            pl.BlockSpec((1, tq, KV_W), lambda bi, i: (bi, i, col_v // KV_W)),
            pl.BlockSpec((1, LANES), lambda bi, i: (0, 0)),
            pl.BlockSpec((tq, LANES), lambda bi, i: (i, 0)),
            pl.BlockSpec((tq, LANES), lambda bi, i: (i, 0)),
            pl.BlockSpec((2 * LANES, LANES), lambda bi, i: (0, 0)),
        ],
        out_specs=(pl.BlockSpec((1, N_KV, LANES, tq), lambda bi, i: (bi, 0, 0, i)),
                   pl.BlockSpec((1, N_KV, tq, LANES), lambda bi, i: (bi, 0, i, 0)),
                   pl.BlockSpec((1, 1, 8, LANES), lambda bi, i: (bi, i, 0, 0))),
        compiler_params=_cparams(("parallel", "parallel")),
    )(p3, p3, g128, cos, sins, ones2)


def prep_q(p3, col, g128, cos, sins, kmax, norm, rope, tq):
    b, l, _ = p3.shape
    tq = min(tq, l)
    ones2 = jnp.asarray(_GROUP_ONES2, BF16)
    return pl.pallas_call(
        functools.partial(_prep_q_kernel, norm=norm, rope=rope),
        out_shape=(jax.ShapeDtypeStruct((b, l, N_HEADS * LANES), BF16),
                   jax.ShapeDtypeStruct((b, l // tq, 8, LANES), F32)),
        grid_spec=pltpu.PrefetchScalarGridSpec(
            num_scalar_prefetch=1,
            grid=(b, l // tq),
            in_specs=[
                pl.BlockSpec((1, tq, LANES), lambda bi, i, s, t=t: (bi, i, col // LANES + t))
                for t in range(ATT_W // LANES)
            ] + [
                pl.BlockSpec((1, LANES), lambda bi, i, s: (0, 0)),
                pl.BlockSpec((tq, LANES), lambda bi, i, s: (i, 0)),
                pl.BlockSpec((tq, LANES), lambda bi, i, s: (i, 0)),
                pl.BlockSpec((2 * LANES, LANES), lambda bi, i, s: (0, 0)),
            ],
            out_specs=(pl.BlockSpec((1, tq, N_HEADS * LANES), lambda bi, i, s: (bi, i, 0)),
                       pl.BlockSpec((1, 1, 8, LANES), lambda bi, i, s: (bi, i, 0, 0))),
        ),
        compiler_params=_cparams(("parallel", "parallel")),
    )(kmax, p3, p3, p3, p3, g128, cos, sins, ones2)


def _key_norm_max(kn_list):
    kn = functools.reduce(jnp.maximum, [jnp.max(k, axis=(1, 2)) for k in kn_list])
    return jnp.sqrt(kn[:, ::HEAD_DIM]).reshape(-1)


def _attn_kernel(sink_ref, q_ref, kt_ref, v_ref, o_ref, *, tk, online, use_sink, band, seq):
    tq = q_ref.shape[1]
    rows = GROUP * tq
    lane = lax.broadcasted_iota(jnp.int32, (tq, LANES), 1)
    low = lane < HEAD_DIM
    if band is not None:
        width = tq + 2 * WINDOW
        start = pl.multiple_of(pl.program_id(1) * tq, tq)
        r = lax.broadcasted_iota(jnp.int32, (rows, width), 0) % tq
        kp = lax.broadcasted_iota(jnp.int32, (rows, width), 1)
        key_pos = start + kp - WINDOW
        mask = (jnp.abs(kp - WINDOW - r) <= WINDOW) & (key_pos >= 0) & (key_pos < seq)
    for g in range(N_KV):
        tiles = [q_ref[0, :, (g * GROUP + hh) * LANES:(g * GROUP + hh + 1) * LANES] for hh in range(GROUP)]
        qg = jnp.concatenate(tiles, axis=0)

        def scores(kt, qg=qg):
            return jnp.dot(qg, kt, preferred_element_type=F32)

        def update(carry, s, v, msk=None):
            if msk is not None:
                s = jnp.where(msk, s, MASK_VALUE)
            if online:
                m, acc = carry
                m_new = jnp.maximum(m, jnp.max(s, axis=1, keepdims=True))
                p = jnp.exp2(s - m_new).astype(BF16)
                return m_new, jnp.exp2(m - m_new) * acc + jnp.dot(p, v, preferred_element_type=F32)
            return carry + jnp.dot(jnp.exp2(s).astype(BF16), v, preferred_element_type=F32)

        acc0 = jnp.zeros((rows, LANES), F32)
        carry = (jnp.full((rows, 1), MASK_VALUE, F32), acc0) if online else acc0
        if band is None:
            def chunk(c, k0, g=g, scores=scores, update=update):
                return update(c, scores(kt_ref[0, g, :, pl.ds(k0, tk)]), v_ref[0, g, pl.ds(k0, tk), :])

            nk = kt_ref.shape[3] // tk
            for j in range(nk % 2):
                carry = chunk(carry, j * tk)

            def body(j, c, chunk=chunk, first=(nk % 2) * tk):
                k0 = pl.multiple_of(first + j * (2 * tk), tk)
                return chunk(chunk(c, k0), pl.multiple_of(k0 + tk, tk))

            carry = lax.fori_loop(0, nk // 2, body, carry)
        else:
            carry = update(carry, scores(kt_ref[0, g, :, 0:band]), v_ref[0, g, 0:band, :])
            b0 = pl.multiple_of(band + start, LANES)
            carry = update(carry, scores(kt_ref[0, g, :, pl.ds(b0, width)]), v_ref[0, g, pl.ds(b0, width), :], mask)
        acc = carry[1] if online else carry
        if use_sink:
            e = jnp.concatenate([t[:, HEAD_DIM:HEAD_DIM + 1].astype(F32) + sink_ref[g * GROUP + hh] * LOG2E
                                 for hh, t in enumerate(tiles)], axis=0)
            if online:
                e = e - carry[0]
            lane_r = lax.broadcasted_iota(jnp.int32, (rows, LANES), 1)
            acc = acc + jnp.where(lane_r >= HEAD_DIM, jnp.exp2(e), 0.0)
        o = acc * pltpu.roll(1.0 / acc, HEAD_DIM, 1)
        for pair in range(GROUP // 2):
            a = o[(2 * pair) * tq:(2 * pair + 1) * tq]
            b = o[(2 * pair + 1) * tq:(2 * pair + 2) * tq]
            t0 = (g * GROUP // 2 + pair) * LANES
            o_ref[0, :, t0:t0 + LANES] = jnp.where(low, a, pltpu.roll(b, HEAD_DIM, 1)).astype(o_ref.dtype)


def _attention(q, kt, v, sink, tq, tk, online, use_sink, band):
    b, l, _ = q.shape
    sk = kt.shape[3]
    tq = min(tq, l)
    tk = min(tk, sk)
    kern = functools.partial(_attn_kernel, tk=tk, online=online, use_sink=use_sink, band=band, seq=l)
    return pl.pallas_call(
        kern,
        out_shape=jax.ShapeDtypeStruct((b, l, ATT_W), BF16),
        grid_spec=pltpu.PrefetchScalarGridSpec(
            num_scalar_prefetch=1,
            grid=(b, l // tq),
            in_specs=[
                pl.BlockSpec((1, tq, N_HEADS * LANES), lambda bi, i, s: (bi, i, 0)),
                pl.BlockSpec((1, N_KV, LANES, sk), lambda bi, i, s: (bi, 0, 0, 0)),
                pl.BlockSpec((1, N_KV, sk, LANES), lambda bi, i, s: (bi, 0, 0, 0)),
            ],
            out_specs=pl.BlockSpec((1, tq, ATT_W), lambda bi, i, s: (bi, i, 0)),
        ),
        compiler_params=_cparams(("parallel", "arbitrary")),
    )(sink, q, kt, v)


def attention(q, kt, v, ubound, sink, tq, tk, use_sink=False, band=None):
    return lax.cond(ubound <= SAFE_BOUND,
                    lambda: _attention(q, kt, v, sink, tq, tk, False, use_sink, band),
                    lambda: _attention(q, kt, v, sink, tq, tk, True, use_sink, band))


def _merge_kernel(of_ref, ob_ref, hg_ref, b_ref, c_ref, ga_ref, gb_ref, gc_ref, x_ref,
                  wa_ref, wb_ref, wc_ref, wo_ref, hgn_ref, gate1_ref, n2_ref, sc2_ref, sh2_ref, wr_ref,
                  xo_ref, h2_ref, lg_ref):
    o = of_ref[0].astype(F32) + ob_ref[0].astype(F32)
    tiles = []
    for h in range(HG_HEADS):
        t = o[:, h * LANES:(h + 1) * LANES]
        ms = jnp.mean(t * t, axis=-1, keepdims=True)
        tiles.append(t * lax.rsqrt(ms + EPS) * hgn_ref[...])
    hg = hg_ref[...].astype(F32)
    a = (jnp.concatenate(tiles, axis=1) * (hg * _sigmoid_t(hg))).astype(BF16)
    merged = (_sigmoid_t(ga_ref[...].astype(F32)) * jnp.dot(a, wa_ref[...], preferred_element_type=F32)
              + _sigmoid_t(gb_ref[...].astype(F32)) * jnp.dot(b_ref[...], wb_ref[...], preferred_element_type=F32)
              + _sigmoid_t(gc_ref[...].astype(F32)) * jnp.dot(c_ref[...], wc_ref[...], preferred_element_type=F32))
    y = jnp.dot(merged.astype(BF16), wo_ref[...], preferred_element_type=F32)
    x = x_ref[...] + gate1_ref[0] * y
    xo_ref[...] = x
    ms = jnp.mean(x * x, axis=-1, keepdims=True)
    h2 = (x * lax.rsqrt(ms + EPS) * n2_ref[...]) * (1.0 + sc2_ref[0]) + sh2_ref[0]
    hi = h2.astype(BF16)
    lo = (h2 - hi.astype(F32)).astype(BF16)
    h2_ref[...] = hi
    lg_ref[...] = jnp.dot(jnp.concatenate([hi, lo, hi], axis=1), wr_ref[...], preferred_element_type=F32)


def merge(o_fb, p2d, b2d, c2d, x2d, rows_per_batch, wa, wb, wc, wo, hgn, gate1, n2, sc2, sh2, w_route, tm):
    t, d = x2d.shape
    tm = min(tm, rows_per_batch)
    per = rows_per_batch // tm
    row = lambda i: (i, 0)
    const = lambda i: (0, 0)
    bat = lambda i: (i // per, 0, 0)
    gcol = COL["gates"] // d
    return pl.pallas_call(
        _merge_kernel,
        out_shape=(jax.ShapeDtypeStruct((t, d), F32), jax.ShapeDtypeStruct((t, d), BF16),
                   jax.ShapeDtypeStruct((t, LANES), F32)),
        grid=(t // tm,),
        in_specs=[
            pl.BlockSpec((1, tm, HG_W), lambda i: (0, i, 0)),
            pl.BlockSpec((1, tm, HG_W), lambda i: (1, i, 0)),
            pl.BlockSpec((tm, HG_W), lambda i: (i, COL["hg"] // HG_W)),
            pl.BlockSpec((tm, ATT_W), row),
            pl.BlockSpec((tm, ATT_W), row),
            pl.BlockSpec((tm, d), lambda i: (i, gcol)),
            pl.BlockSpec((tm, d), lambda i: (i, gcol + 1)),
            pl.BlockSpec((tm, d), lambda i: (i, gcol + 2)),
            pl.BlockSpec((tm, d), row),
            pl.BlockSpec((HG_W, d), const),
            pl.BlockSpec((ATT_W, d), const),
            pl.BlockSpec((ATT_W, d), const),
            pl.BlockSpec((d, d), const),
            pl.BlockSpec((1, LANES), const),
            pl.BlockSpec((1, 1, d), bat),
            pl.BlockSpec((1, d), const),
            pl.BlockSpec((1, 1, d), bat),
            pl.BlockSpec((1, 1, d), bat),
            pl.BlockSpec((3 * d, LANES), const),
        ],
        out_specs=(pl.BlockSpec((tm, d), row), pl.BlockSpec((tm, d), row), pl.BlockSpec((tm, LANES), row)),
        compiler_params=_cparams(("parallel",)),
    )(o_fb, o_fb, p2d, b2d, c2d, p2d, p2d, p2d, x2d, wa, wb, wc, wo, hgn.reshape(1, LANES),
      gate1, n2.reshape(1, d), sc2, sh2, w_route)


_STRICT_LOWER = np.tril(np.ones((ROUTE_ROWS, ROUTE_ROWS), np.float32), -1)
_NEG_BIG = -3.0e38


def _lane_argmax(x, lane):
    top = jnp.max(x, axis=1, keepdims=True)
    idx = jnp.min(jnp.where(x == top, lane, LANES), axis=1, keepdims=True)
    return top, idx


def _route_kernel(lg_ref, tri_ref, o_ref, cnt_ref, run_scr):
    @pl.when(pl.program_id(0) == 0)
    def _():
        run_scr[...] = jnp.zeros_like(run_scr)

    lg = lg_ref[...]
    lane = lax.broadcasted_iota(jnp.int32, lg.shape, 1)
    is_grp = lane < N_GROUPS
    gtop, gidx = _lane_argmax(jnp.where(is_grp, lg, _NEG_BIG), lane)
    grp_w = 1.0 / jnp.sum(jnp.where(is_grp, jnp.exp(lg - gtop), 0.0), axis=1, keepdims=True)
    lo = N_GROUPS + EXP_PER_GROUP * gidx
    x1 = jnp.where((lane >= lo) & (lane < lo + EXP_PER_GROUP), lg, _NEG_BIG)
    t1, i1 = _lane_argmax(x1, lane)
    t2, i2 = _lane_argmax(jnp.where(lane == i1, _NEG_BIG, x1), lane)
    r = jnp.exp(t2 - t1)
    w1 = grp_w / (1.0 + r)
    w2 = w1 * r
    e1 = i1 - N_GROUPS
    e2 = i2 - N_GROUPS
    hot1 = lane == e1
    hot2 = lane == e2
    hot = jnp.where(hot1 | hot2, 1.0, 0.0)
    before = run_scr[...] + jnp.dot(tri_ref[...], hot.astype(BF16), preferred_element_type=F32)
    rank1 = jnp.sum(jnp.where(hot1, before, 0.0), axis=1, keepdims=True)
    rank2 = jnp.sum(jnp.where(hot2, before, 0.0), axis=1, keepdims=True)
    run_scr[...] = run_scr[...] + jnp.sum(hot, axis=0, keepdims=True)
    cnt_ref[...] = jnp.broadcast_to(run_scr[...], cnt_ref.shape)
    out = jnp.where(lane == 0, e1.astype(F32), jnp.where(lane == 1, e2.astype(F32), 0.0))
    out = jnp.where(lane == 2, w1, jnp.where(lane == 3, w2, out))
    o_ref[...] = jnp.where(lane == 4, rank1, jnp.where(lane == 5, rank2, out))


def route(logits):
    t = logits.shape[0]
    return pl.pallas_call(
        _route_kernel,
        out_shape=(jax.ShapeDtypeStruct((t, LANES), F32), jax.ShapeDtypeStruct((8, LANES), F32)),
        grid=(t // ROUTE_ROWS,),
        in_specs=[pl.BlockSpec((ROUTE_ROWS, LANES), lambda i: (i, 0)),
                  pl.BlockSpec((ROUTE_ROWS, ROUTE_ROWS), lambda i: (0, 0))],
        out_specs=(pl.BlockSpec((ROUTE_ROWS, LANES), lambda i: (i, 0)),
                   pl.BlockSpec((8, LANES), lambda i: (0, 0))),
        scratch_shapes=[pltpu.VMEM((1, LANES), F32)],
        compiler_params=_cparams(("arbitrary",)),
    )(logits, jnp.asarray(_STRICT_LOWER, BF16))


def _expert_kernel(be_ref, x_ref, wg_ref, wu_ref, wd_ref, *rest):
    o_ref, wg_s, wu_s, wd_s = rest[-4:]
    i = pl.program_id(0)

    @pl.when((i == 0) | (be_ref[i] != be_ref[jnp.maximum(i - 1, 0)]))
    def _():
        wg_s[...] = wg_ref[0, 0].astype(BF16)
        wu_s[...] = wu_ref[0, 0].astype(BF16)
        wd_s[...] = wd_ref[0, 0].astype(BF16)

    x = x_ref[...]
    gte = jnp.dot(x, wg_s[...], preferred_element_type=F32)
    up = jnp.dot(x, wu_s[...], preferred_element_type=F32)
    hid = (gte * _sigmoid(gte) * up).astype(BF16)
    o_ref[...] = jnp.dot(hid, wd_s[...], preferred_element_type=F32).astype(o_ref.dtype)


def expert_ffn(rows, block_expert, wg, wu, wd, layer, out_prev, block_off, n_total):
    n, d = rows.shape
    de = wg.shape[3]
    in_specs = [
        pl.BlockSpec((MOE_ROWS, d), lambda i, be: (i, 0)),
        pl.BlockSpec((1, 1, d, de), lambda i, be: (layer, be[i], 0, 0)),
        pl.BlockSpec((1, 1, d, de), lambda i, be: (layer, be[i], 0, 0)),
        pl.BlockSpec((1, 1, de, d), lambda i, be: (layer, be[i], 0, 0)),
    ]
    args = [block_expert, rows, wg, wu, wd]
    aliases = {}
    if out_prev is not None:
        in_specs.append(pl.BlockSpec(memory_space=pl.ANY))
        args.append(out_prev)
        aliases = {5: 0}
    return pl.pallas_call(
        _expert_kernel,
        out_shape=jax.ShapeDtypeStruct((n_total, d), BF16),
        grid_spec=pltpu.PrefetchScalarGridSpec(
            num_scalar_prefetch=1,
            grid=(n // MOE_ROWS,),
            in_specs=in_specs,
            out_specs=pl.BlockSpec((MOE_ROWS, d), lambda i, be: (i + block_off, 0)),
            scratch_shapes=[pltpu.VMEM((d, de), BF16), pltpu.VMEM((d, de), BF16), pltpu.VMEM((de, d), BF16)],
        ),
        input_output_aliases=aliases,
        compiler_params=_cparams(("arbitrary",)),
    )(*args)


MOE_PARTS = 4


def experts_overlapped(h_all, src_tok, block_expert, wg, wu, wd, layer):
    n_blocks = block_expert.shape[0]
    n_total = n_blocks * MOE_ROWS
    bounds = [n_blocks * k // MOE_PARTS for k in range(MOE_PARTS + 1)]
    out = None
    for b0, b1 in zip(bounds[:-1], bounds[1:]):
        rows = jnp.take(h_all, src_tok[b0 * MOE_ROWS:b1 * MOE_ROWS], axis=0, mode="clip")
        out = expert_ffn(rows, block_expert[b0:b1], wg, wu, wd, layer, out, b0, n_total)
    return out


def _combine_kernel(x_ref, r1_ref, r2_ref, info_ref, gate_ref, *rest, final):
    o_ref = rest[-1]
    info = info_ref[...]
    y = info[:, 2:3] * r1_ref[...].astype(F32) + info[:, 3:4] * r2_ref[...].astype(F32)
    x = x_ref[...] + gate_ref[0] * y
    if final:
        ms = jnp.mean(x * x, axis=-1, keepdims=True)
        x = x * lax.rsqrt(ms + EPS) * rest[0][...]
    o_ref[...] = x


def combine(x2d, r1, r2, info, row_off, gate, rows_per_batch, final_g=None):
    t, d = x2d.shape
    tm = min(ROUTE_ROWS, rows_per_batch)
    per = rows_per_batch // tm
    off = row_off // tm
    row = lambda i: (i, 0)
    shifted = lambda i: (i + off, 0)
    specs = [pl.BlockSpec((tm, d), row), pl.BlockSpec((tm, d), shifted), pl.BlockSpec((tm, d), shifted),
             pl.BlockSpec((tm, LANES), shifted), pl.BlockSpec((1, 1, d), lambda i: (i // per, 0, 0))]
    args = [x2d, r1, r2, info, gate]
    if final_g is not None:
        specs.append(pl.BlockSpec((1, d), lambda i: (0, 0)))
        args.append(final_g.reshape(1, d))
    return pl.pallas_call(
        functools.partial(_combine_kernel, final=final_g is not None),
        out_shape=jax.ShapeDtypeStruct((t, d), F32), grid=(t // tm,),
        in_specs=specs, out_specs=pl.BlockSpec((tm, d), row),
        compiler_params=_cparams(("parallel",)),
    )(*args)


def _dispatch(info, counts_row):
    n_tok = info.shape[0]
    expert = info[:, 0:TOP_K].astype(jnp.int32)
    rank = info[:, 4:4 + TOP_K].astype(jnp.int32)
    counts = counts_row[0, :N_EXPERTS].astype(jnp.int32)
    padded = (counts + MOE_ROWS - 1) // MOE_ROWS * MOE_ROWS
    pad_end = jnp.cumsum(padded)
    pad_start = pad_end - padded
    hot = expert[:, :, None] == jnp.arange(N_EXPERTS, dtype=jnp.int32)
    dest = jnp.sum(jnp.where(hot, pad_start, 0), axis=-1) + rank
    n_blocks = -(-(n_tok * TOP_K + N_EXPERTS * (MOE_ROWS - 1)) // MOE_ROWS)
    token = jnp.broadcast_to(jnp.arange(n_tok, dtype=jnp.int32)[:, None], dest.shape)
    src_tok = jnp.zeros((n_blocks * MOE_ROWS,), jnp.int32).at[dest.reshape(-1)].set(token.reshape(-1))
    block_start = jnp.arange(n_blocks, dtype=jnp.int32)[:, None] * MOE_ROWS
    block_expert = jnp.minimum(jnp.sum((pad_end[None, :] <= block_start).astype(jnp.int32), axis=1),
                               N_EXPERTS - 1)
    return src_tok, dest, block_expert


def _rope_tables(seq):
    n_rows = seq // GRID_W
    row = jnp.repeat(jnp.arange(n_rows), GRID_W).astype(F32)
    col = jnp.tile(jnp.arange(GRID_W), n_rows).astype(F32)
    axis_pairs = HEAD_DIM // 4
    inv = ROPE_THETA ** (-jnp.arange(axis_pairs, dtype=F32) / axis_pairs)
    ang = jnp.concatenate([row[:, None] * inv, col[:, None] * inv], axis=-1)
    cos, sin = jnp.cos(ang), jnp.sin(ang)
    cos128 = jnp.concatenate([cos, cos, cos, cos], axis=-1)
    sins128 = jnp.concatenate([-sin, sin, -sin, sin], axis=-1)
    return cos128, sins128


def kernel(x, c, ctx, c_ctx, w_mod, b_mod, norm1_g, norm2_g, w_in, hgrn_lb_logits, hgrn_out_norm_g,
           attn_q_norm_g, attn_k_norm_g, swa_sink, w_branch_a, w_branch_b, w_branch_c, w_out,
           w_group, w_router, w_exp_gate, w_exp_up, w_exp_down, final_norm_g):
    bsz, seq, d = x.shape
    n_ctx = ctx.shape[1]
    depth = w_mod.shape[0]
    cos, sins = _rope_tables(seq)
    lb_p = jax.nn.softmax(hgrn_lb_logits.astype(F32), axis=0)
    lower_bounds = jnp.cumsum(lb_p, axis=0) - lb_p[0]
    no_sink = jnp.zeros((N_HEADS,), F32)
    ones128 = jnp.ones((1, LANES), F32)

    xl = x.reshape(bsz * seq, d)
    xc = ctx.reshape(bsz * n_ctx, d)
    for layer in range(depth):
        ctx_out = layer < depth - 1
        mod_l = jax.nn.silu(c) @ w_mod[layer] + b_mod[layer]
        mod_c = jnp.broadcast_to(jax.nn.silu(c_ctx) @ w_mod[layer] + b_mod[layer], (bsz, 6 * d))
        ml = [m.reshape(bsz, 1, d) for m in jnp.split(mod_l, 6, axis=-1)]
        mc = [m.reshape(bsz, 1, d) for m in jnp.split(mod_c, 6, axis=-1)]

        w_in_b = w_in[layer].astype(BF16)
        p_l = inproj(xl, seq, norm1_g[layer], ml[1], ml[0], w_in_b, 1024, 3584)
        p_c = inproj(xc, n_ctx, norm1_g[layer], mc[1], mc[0], w_in_b, 256, 1024)
        p_l3 = p_l.reshape(bsz, seq, D_IN)
        p_c3 = p_c.reshape(bsz, n_ctx, D_IN)

        s0 = jnp.zeros((bsz, 2, HG_HEADS, HG_KDIM, HG_KDIM), F32)
        o_c, s_c = hgrn_scan(p_c3, lower_bounds[layer], s0, 256)
        o_l, _ = hgrn_scan(p_l3, lower_bounds[layer], s_c, 512)

        gq = jnp.tile(attn_q_norm_g[layer], 2).reshape(1, LANES)
        gk = jnp.tile(attn_k_norm_g[layer], 2).reshape(1, LANES)
        kt_l, v_l, kn_l = prep_kv(p_l3, COL["ak"], COL["av"], gk, cos, sins, True, True, 512)
        kt_c, v_c, kn_c = prep_kv(p_c3, COL["ak"], COL["av"], gk, cos, sins, True, False, 256)
        kmax = _key_norm_max([kn_l, kn_c])
        q_l, u_l = prep_q(p_l3, COL["aq"], gq, cos, sins, kmax, True, True, 512)
        kt_all = jnp.concatenate([kt_c, kt_l], axis=3)
        v_all = jnp.concatenate([v_c, v_l], axis=2)
        b_l = attention(q_l, kt_all, v_all, jnp.max(u_l), no_sink, 512, 640)

        skt_l, sv_l, skn_l = prep_kv(p_l3, COL["sk"], COL["sv"], ones128, cos, sins, False, True, 512)
        skt_c, sv_c, skn_c = prep_kv(p_c3, COL["sk"], COL["sv"], ones128, cos, sins, False, False, 256)
        skmax = _key_norm_max([skn_l, skn_c])
        sq_l, su_l = prep_q(p_l3, COL["sq"], ones128, cos, sins, skmax, False, True, 512)
        zk = jnp.zeros((bsz, N_KV, LANES, WINDOW), BF16)
        zv = jnp.zeros((bsz, N_KV, WINDOW, LANES), BF16)
        skt_all = jnp.concatenate([skt_c, zk, skt_l, zk], axis=3)
        sv_all = jnp.concatenate([sv_c, zv, sv_l, zv], axis=2)
        c_l = attention(sq_l, skt_all, sv_all, jnp.max(su_l), swa_sink[layer], 256, 256,
                        use_sink=True, band=n_ctx)

        wa = w_branch_a[layer].astype(BF16)
        wb = w_branch_b[layer].astype(BF16)
        wc = w_branch_c[layer].astype(BF16)
        wo = w_out[layer].astype(BF16)
        w_r = jnp.concatenate(
            [w_group[layer], w_router[layer], jnp.zeros((d, LANES - N_GROUPS - N_EXPERTS), F32)], axis=1)
        w_r_hi = w_r.astype(BF16)
        w_r_lo = (w_r - w_r_hi.astype(F32)).astype(BF16)
        w_route = jnp.concatenate([w_r_hi, w_r_hi, w_r_lo], axis=0)
        xl, h_l, lg_l = merge(o_l.reshape(2, bsz * seq, HG_W), p_l, b_l.reshape(-1, ATT_W),
                              c_l.reshape(-1, ATT_W), xl, seq, wa, wb, wc, wo, hgrn_out_norm_g[layer],
                              ml[2], norm2_g[layer], ml[4], ml[3], w_route, 256)
        if ctx_out:
            q_c, u_c = prep_q(p_c3, COL["aq"], gq, cos, sins, kmax, True, False, 256)
            b_c = attention(q_c, kt_c, v_c, jnp.max(u_c), no_sink, 256, 256)
            sq_c, su_c = prep_q(p_c3, COL["sq"], ones128, cos, sins, skmax, False, False, 256)
            c_c = attention(sq_c, skt_c, sv_c, jnp.max(su_c), swa_sink[layer], 256, 256, use_sink=True)
            xc, h_c, lg_c = merge(o_c.reshape(2, bsz * n_ctx, HG_W), p_c, b_c.reshape(-1, ATT_W),
                                  c_c.reshape(-1, ATT_W), xc, n_ctx, wa, wb, wc, wo, hgrn_out_norm_g[layer],
                                  mc[2], norm2_g[layer], mc[4], mc[3], w_route, 256)
            h_all = jnp.concatenate([h_c, h_l], axis=0)
            lg_all = jnp.concatenate([lg_c, lg_l], axis=0)
        else:
            h_all, lg_all = h_l, lg_l

        info, counts_row = route(lg_all)
        src_tok, dest, block_expert = _dispatch(info, counts_row)
        out_rows = experts_overlapped(h_all, src_tok, block_expert, w_exp_gate, w_exp_up, w_exp_down, layer)
        r1 = jnp.take(out_rows, dest[:, 0], axis=0, mode="clip")
        r2 = jnp.take(out_rows, dest[:, 1], axis=0, mode="clip")
        n_c = bsz * n_ctx if ctx_out else 0
        if ctx_out:
            xc = combine(xc, r1, r2, info, 0, mc[5], n_ctx)
        xl = combine(xl, r1, r2, info, n_c, ml[5], seq, final_g=final_norm_g if layer == depth - 1 else None)
    return xl.reshape(bsz, seq, d)
```

```python
import functools
import math

import numpy as np
import jax
import jax.numpy as jnp
from jax import lax
from jax.experimental import pallas as pl
from jax.experimental.pallas import tpu as pltpu

F32 = jnp.float32
BF16 = jnp.bfloat16

EPS = 1e-6
MASK_VALUE = -1e30
TINY = 1e-30
GRID_W = 64
ROPE_THETA = 10000.0

HG_HEADS = 4
HG_KDIM = 128
HG_W = HG_HEADS * HG_KDIM
HEAD_DIM = 64
N_HEADS = 8
N_KV = 2
GROUP = N_HEADS // N_KV
ATT_W = N_HEADS * HEAD_DIM
KV_W = N_KV * HEAD_DIM
WINDOW = 128
ATTN_SCALE = HEAD_DIM ** -0.5
N_GROUPS = 4
EXP_PER_GROUP = 8
N_EXPERTS = N_GROUPS * EXP_PER_GROUP
TOP_K = 2

LANES = 128
HG_CHUNK = 128
HG_LEVELS = int(math.log2(HG_CHUNK))
HG_TOTAL_ROWS = 16
MOE_ROWS = 256
ROUTE_ROWS = 512
VMEM_LIMIT = 56 * 1024 * 1024

LOG2E = 1.4426950408889634
SHIFT_HEADROOM = 57.0
SAFE_BOUND = 90.0


def _cparams(sem):
    return pltpu.CompilerParams(dimension_semantics=sem, vmem_limit_bytes=VMEM_LIMIT)


_SIZES = (HG_W, HG_W, HG_W, HG_W, HG_W, ATT_W, KV_W, KV_W, ATT_W, KV_W, KV_W, 3 * 1024)
_NAMES = ("hq", "ff", "fb", "hi", "hg", "aq", "ak", "av", "sq", "sk", "sv", "gates")
COL = {n: int(sum(_SIZES[:i])) for i, n in enumerate(_NAMES)}
D_IN = int(sum(_SIZES))


def _inproj_kernel(x_ref, g_ref, sc_ref, sh_ref, w_ref, o_ref, h_scr):
    @pl.when(pl.program_id(1) == 0)
    def _():
        x = x_ref[...]
        ms = jnp.mean(x * x, axis=-1, keepdims=True)
        y = x * lax.rsqrt(ms + EPS) * g_ref[...]
        h_scr[...] = (y * (1.0 + sc_ref[0]) + sh_ref[0]).astype(BF16)

    o_ref[...] = jnp.dot(h_scr[...], w_ref[...], preferred_element_type=F32).astype(o_ref.dtype)


def inproj(x2d, rows_per_batch, g, scale, shift, w_bf16, tm, tn):
    t, d = x2d.shape
    n = w_bf16.shape[1]
    tm = min(tm, rows_per_batch)
    per = rows_per_batch // tm
    return pl.pallas_call(
        _inproj_kernel,
        out_shape=jax.ShapeDtypeStruct((t, n), BF16),
        grid=(t // tm, n // tn),
        in_specs=[
            pl.BlockSpec((tm, d), lambda i, j: (i, 0)),
            pl.BlockSpec((1, d), lambda i, j: (0, 0)),
            pl.BlockSpec((1, 1, d), lambda i, j: (i // per, 0, 0)),
            pl.BlockSpec((1, 1, d), lambda i, j: (i // per, 0, 0)),
            pl.BlockSpec((d, tn), lambda i, j: (0, j)),
        ],
        out_specs=pl.BlockSpec((tm, tn), lambda i, j: (i, j)),
        scratch_shapes=[pltpu.VMEM((tm, d), BF16)],
        compiler_params=_cparams(("parallel", "arbitrary")),
    )(x2d, g.reshape(1, d), scale, shift, w_bf16)


def _hgrn_exponent_matrices():
    c, nl = HG_CHUNK, HG_LEVELS
    rows = (nl + 2) * c + HG_TOTAL_ROWS
    out = np.zeros((2, rows, c), np.float32)
    for d in range(2):
        pos = np.arange(c) if d == 0 else c - 1 - np.arange(c)
        for l in range(nl):
            m = 1 << l
            for t in range(c):
                p = pos[t]
                mid = (p // (2 * m)) * 2 * m + m
                if p >= mid:
                    sel = (pos >= mid) & (pos <= p)
                else:
                    sel = (pos > p) & (pos <= mid - 1)
                out[d, l * c + t, sel] = 1.0
        for t in range(c):
            out[d, nl * c + t, pos <= pos[t]] = 1.0
            out[d, (nl + 1) * c + t, pos > pos[t]] = 1.0
        out[d, (nl + 2) * c:, :] = 1.0
    return np.concatenate([out, out], axis=2)


_HG_EXP_MATS = _hgrn_exponent_matrices()


def _sigmoid(x):
    return 1.0 / (1.0 + jnp.exp(-x))


def _hgrn_kernel(q_ref, f_ref, i_ref, lb_ref, mat_ref, s0_ref, o_ref, st_ref, *, n_chunks):
    c, nl = HG_CHUNK, HG_LEVELS
    d = pl.program_id(1)

    @pl.when(pl.program_id(2) == 0)
    def _():
        st_ref[...] = s0_ref[...]

    row = lax.broadcasted_iota(jnp.int32, (c, HG_W), 0)
    pos = row + d * (c - 1 - 2 * row)
    r_i = lax.broadcasted_iota(jnp.int32, (c, c), 0)
    c_i = lax.broadcasted_iota(jnp.int32, (c, c), 1)
    xor_rc = r_i ^ c_i
    top_bit = sum((xor_rc >= (1 << j)).astype(jnp.int32) for j in range(1, nl))
    pair_level = jnp.where((r_i - c_i) * (1 - 2 * d) > 0, top_bit, -1)

    def chunk(ci, carry):
        cc = ci + d * (n_chunks - 1 - 2 * ci)
        r0 = pl.multiple_of(cc * c, c)
        qraw = q_ref[0, pl.ds(r0, c), :].astype(F32)
        fz = f_ref[0, pl.ds(r0, c), :].astype(F32)
        vb = i_ref[0, pl.ds(r0, c), :]
        v = vb.astype(F32)
        lb = lb_ref[...]
        sig = _sigmoid(fz)
        logf = jnp.log(jnp.maximum(lb + (1.0 - lb) * sig, TINY))
        key = (1.0 - lb) * (1.0 - sig)
        qh = qraw * _sigmoid(qraw) * (HG_KDIM ** -0.5)
        logf2 = logf * LOG2E
        hi = logf2.astype(BF16)
        lo = (logf2 - hi.astype(F32)).astype(BF16)
        hilo = jnp.concatenate([hi, lo], axis=0)

        dec_all = jnp.exp2(jnp.dot(mat_ref[0], hilo, preferred_element_type=F32))

        def decay(r_lo, r_hi):
            return dec_all[r_lo:r_hi]

        a = [jnp.zeros((c, c), F32)] * HG_HEADS
        for l in range(nl):
            x = (jnp.where(((pos >> l) & 1) == 1, qh, key) * decay(l * c, (l + 1) * c)).astype(BF16)
            for h in range(HG_HEADS):
                xh = x[:, h * LANES:(h + 1) * LANES]
                prod = lax.dot_general(xh, xh, (((1,), (1,)), ((), ())), preferred_element_type=F32)
                a[h] = jnp.where(pair_level == l, prod, a[h])
        q_in = (qh * decay(nl * c, (nl + 1) * c)).astype(BF16)
        k_out = (key * decay((nl + 1) * c, (nl + 2) * c)).astype(BF16)
        total = decay((nl + 2) * c, (nl + 2) * c + HG_TOTAL_ROWS)[0:1]
        qk = qh * key
        for h in range(HG_HEADS):
            lanes = slice(h * LANES, (h + 1) * LANES)
            st = st_ref[0, 0, h]
            o = (jnp.dot(a[h].astype(BF16), vb[:, lanes], preferred_element_type=F32)
                 + jnp.sum(qk[:, lanes], axis=1, keepdims=True) * v[:, lanes]
                 + lax.dot_general(q_in[:, lanes], st.astype(BF16), (((1,), (1,)), ((), ())),
                                   preferred_element_type=F32))
            o_ref[0, 0, pl.ds(r0, c), lanes] = o.astype(o_ref.dtype)
            st_ref[0, 0, h] = total[:, lanes] * st + lax.dot_general(
                vb[:, lanes], k_out[:, lanes], (((0,), (0,)), ((), ())), preferred_element_type=F32)
        return carry

    lax.fori_loop(0, n_chunks, chunk, 0)


def hgrn_scan(p3, lb, s0, tb):
    b, l, _ = p3.shape
    tb = min(tb, l)
    nb = l // tb
    blk = lambda d, n: n + d * (nb - 1 - 2 * n)
    wblk = HG_W
    kern = functools.partial(_hgrn_kernel, n_chunks=tb // HG_CHUNK)
    mats = jnp.asarray(_HG_EXP_MATS, BF16)
    return pl.pallas_call(
        kern,
        out_shape=(jax.ShapeDtypeStruct((2, b, l, HG_W), BF16),
                   jax.ShapeDtypeStruct((b, 2, HG_HEADS, HG_KDIM, HG_KDIM), F32)),
        grid=(b, 2, nb),
        in_specs=[
            pl.BlockSpec((1, tb, wblk), lambda bi, d, n: (bi, blk(d, n), COL["hq"] // wblk)),
            pl.BlockSpec((1, tb, wblk), lambda bi, d, n: (bi, blk(d, n), COL["ff"] // wblk + d)),
            pl.BlockSpec((1, tb, wblk), lambda bi, d, n: (bi, blk(d, n), COL["hi"] // wblk)),
            pl.BlockSpec((1, HG_W), lambda bi, d, n: (0, 0)),
            pl.BlockSpec((1,) + _HG_EXP_MATS.shape[1:], lambda bi, d, n: (d, 0, 0)),
            pl.BlockSpec((1, 1, HG_HEADS, HG_KDIM, HG_KDIM), lambda bi, d, n: (bi, d, 0, 0, 0)),
        ],
        out_specs=(
            pl.BlockSpec((1, 1, tb, HG_W), lambda bi, d, n: (d, bi, blk(d, n), 0)),
            pl.BlockSpec((1, 1, HG_HEADS, HG_KDIM, HG_KDIM), lambda bi, d, n: (bi, d, 0, 0, 0)),
        ),
        compiler_params=_cparams(("parallel", "arbitrary", "arbitrary")),
    )(p3, p3, p3, lb.reshape(1, HG_W), mats, s0)


_GROUP_ONES = np.kron(np.eye(LANES // HEAD_DIM, dtype=np.float32), np.ones((HEAD_DIM, HEAD_DIM), np.float32))
_GROUP_ONES2 = np.concatenate([_GROUP_ONES, _GROUP_ONES], axis=0)


def _group_ssq(x, ones2):
    sq = x * x
    hi = sq.astype(BF16)
    lo = (sq - hi.astype(F32)).astype(BF16)
    return jnp.dot(jnp.concatenate([hi, lo], axis=1), ones2, preferred_element_type=F32)


def _prep_tile(x, g, cos, sins, ones2, norm, rope):
    if norm:
        x = x * lax.rsqrt(_group_ssq(x, ones2) * (1.0 / HEAD_DIM) + EPS) * g
    if rope:
        lane = lax.broadcasted_iota(jnp.int32, x.shape, 1)
        first = (lane % HEAD_DIM) < (HEAD_DIM // 2)
        other = jnp.where(first, pltpu.roll(x, LANES - HEAD_DIM // 2, 1), pltpu.roll(x, HEAD_DIM // 2, 1))
        x = x * cos + other * sins
    return x


def _prep_kv_kernel(k_ref, v_ref, g_ref, cos_ref, sin_ref, ones_ref, kt_ref, vo_ref, kn_ref, *, norm, rope):
    k = _prep_tile(k_ref[0].astype(F32), g_ref[...], cos_ref[...], sin_ref[...], ones_ref[...], norm, rope)
    kb = k.astype(BF16).astype(F32)
    ssq = _group_ssq(kb, ones_ref[...])
    kn_ref[0, 0] = jnp.broadcast_to(jnp.max(ssq, axis=0, keepdims=True), (8, LANES))
    kt = kb.T
    row = lax.broadcasted_iota(jnp.int32, kt.shape, 0)
    v = v_ref[0]
    low = lax.broadcasted_iota(jnp.int32, v.shape, 1) < HEAD_DIM
    one = jnp.ones((), v.dtype)
    for g in range(N_KV):
        ktg = kt if g == 0 else jnp.concatenate([kt[HEAD_DIM:], kt[:HEAD_DIM]], axis=0)
        kt_ref[0, g] = jnp.where(row < HEAD_DIM, ktg, jnp.where(row == HEAD_DIM, 1.0, 0.0)).astype(BF16)
        vg = v if g == 0 else jnp.concatenate([v[:, HEAD_DIM:], v[:, :HEAD_DIM]], axis=1)
        vo_ref[0, g] = jnp.where(low, vg, one).astype(BF16)


def _prep_q_kernel(kmax_ref, x0_ref, x1_ref, x2_ref, x3_ref, g_ref, cos_ref, sin_ref, ones_ref, o_ref, u_ref,
                   *, norm, rope):
    b = pl.program_id(0)
    lane = lax.broadcasted_iota(jnp.int32, (x0_ref.shape[1], LANES), 1)
    umax = None
    for tile, x_ref in enumerate((x0_ref, x1_ref, x2_ref, x3_ref)):
        kv = (2 * tile) // GROUP
        x = x_ref[0].astype(F32)
        y = _prep_tile(x, g_ref[...], cos_ref[...], sin_ref[...], ones_ref[...], norm, rope) * (ATTN_SCALE * LOG2E)
        yb = y.astype(BF16).astype(F32)
        u2 = jnp.sqrt(_group_ssq(yb, ones_ref[...])) * kmax_ref[b * N_KV + kv]
        tmax = jnp.max(u2, axis=0, keepdims=True)
        umax = tmax if umax is None else jnp.maximum(umax, tmax)
        neg_shift = SHIFT_HEADROOM - u2
        y_sw = pltpu.roll(yb, HEAD_DIM, 1)
        ns_sw = pltpu.roll(neg_shift, HEAD_DIM, 1)
        for half in range(2):
            h = 2 * tile + half
            data = yb if half == 0 else y_sw
            ns = ns_sw if half == 0 else neg_shift
            o_ref[0, :, h * LANES:(h + 1) * LANES] = jnp.where(
                lane < HEAD_DIM, data, jnp.where(lane == HEAD_DIM, ns, 0.0)).astype(BF16)
    u_ref[0, 0] = jnp.broadcast_to(umax, (8, LANES))


def prep_kv(p3, col_k, col_v, g128, cos, sins, norm, rope, tq):
    b, l, _ = p3.shape
    tq = min(tq, l)
    ones2 = jnp.asarray(_GROUP_ONES2, BF16)
    return pl.pallas_call(
        functools.partial(_prep_kv_kernel, norm=norm, rope=rope),
        out_shape=(jax.ShapeDtypeStruct((b, N_KV, LANES, l), BF16),
                   jax.ShapeDtypeStruct((b, N_KV, l, LANES), BF16),
                   jax.ShapeDtypeStruct((b, l // tq, 8, LANES), F32)),
        grid=(b, l // tq),
        in_specs=[
            pl.BlockSpec((1, tq, KV_W), lambda bi, i: (bi, i, col_k // KV_W)),
            pl.BlockSpec((1, tq, KV_W), lambda bi, i: (bi, i, col_v // KV_W)),
            pl.BlockSpec((1, LANES), lambda bi, i: (0, 0)),
            pl.BlockSpec((tq, LANES), lambda bi, i: (i, 0)),
            pl.BlockSpec((tq, LANES), lambda bi, i: (i, 0)),
            pl.BlockSpec((2 * LANES, LANES), lambda bi, i: (0, 0)),
        ],
        out_specs=(pl.BlockSpec((1, N_KV, LANES, tq), lambda bi, i: (bi, 0, 0, i)),
                   pl.BlockSpec((1, N_KV, tq, LANES), lambda bi, i: (bi, 0, i, 0)),
                   pl.BlockSpec((1, 1, 8, LANES), lambda bi, i: (bi, i, 0, 0))),
        compiler_params=_cparams(("parallel", "parallel")),
    )(p3, p3, g128, cos, sins, ones2)


def prep_q(p3, col, g128, cos, sins, kmax, norm, rope, tq):
    b, l, _ = p3.shape
    tq = min(tq, l)
    ones2 = jnp.asarray(_GROUP_ONES2, BF16)
    return pl.pallas_call(
        functools.partial(_prep_q_kernel, norm=norm, rope=rope),
        out_shape=(jax.ShapeDtypeStruct((b, l, N_HEADS * LANES), BF16),
                   jax.ShapeDtypeStruct((b, l // tq, 8, LANES), F32)),
        grid_spec=pltpu.PrefetchScalarGridSpec(
            num_scalar_prefetch=1,
            grid=(b, l // tq),
            in_specs=[
                pl.BlockSpec((1, tq, LANES), lambda bi, i, s, t=t: (bi, i, col // LANES + t))
                for t in range(ATT_W // LANES)
            ] + [
                pl.BlockSpec((1, LANES), lambda bi, i, s: (0, 0)),
                pl.BlockSpec((tq, LANES), lambda bi, i, s: (i, 0)),
                pl.BlockSpec((tq, LANES), lambda bi, i, s: (i, 0)),
                pl.BlockSpec((2 * LANES, LANES), lambda bi, i, s: (0, 0)),
            ],
            out_specs=(pl.BlockSpec((1, tq, N_HEADS * LANES), lambda bi, i, s: (bi, i, 0)),
                       pl.BlockSpec((1, 1, 8, LANES), lambda bi, i, s: (bi, i, 0, 0))),
        ),
        compiler_params=_cparams(("parallel", "parallel")),
    )(kmax, p3, p3, p3, p3, g128, cos, sins, ones2)


def _key_norm_max(kn_list):
    kn = functools.reduce(jnp.maximum, [jnp.max(k, axis=(1, 2)) for k in kn_list])
    return jnp.sqrt(kn[:, ::HEAD_DIM]).reshape(-1)


def _attn_kernel(sink_ref, q_ref, kt_ref, v_ref, o_ref, *, tk, online, use_sink, band, seq):
    tq = q_ref.shape[1]
    rows = GROUP * tq
    lane = lax.broadcasted_iota(jnp.int32, (tq, LANES), 1)
    low = lane < HEAD_DIM
    if band is not None:
        width = tq + 2 * WINDOW
        start = pl.multiple_of(pl.program_id(1) * tq, tq)
        r = lax.broadcasted_iota(jnp.int32, (rows, width), 0) % tq
        kp = lax.broadcasted_iota(jnp.int32, (rows, width), 1)
        key_pos = start + kp - WINDOW
        mask = (jnp.abs(kp - WINDOW - r) <= WINDOW) & (key_pos >= 0) & (key_pos < seq)
    for g in range(N_KV):
        tiles = [q_ref[0, :, (g * GROUP + hh) * LANES:(g * GROUP + hh + 1) * LANES] for hh in range(GROUP)]
        qg = jnp.concatenate(tiles, axis=0)

        def scores(kt, qg=qg):
            return jnp.dot(qg, kt, preferred_element_type=F32)

        def update(carry, s, v, msk=None):
            if msk is not None:
                s = jnp.where(msk, s, MASK_VALUE)
            if online:
                m, acc = carry
                m_new = jnp.maximum(m, jnp.max(s, axis=1, keepdims=True))
                p = jnp.exp2(s - m_new).astype(BF16)
                return m_new, jnp.exp2(m - m_new) * acc + jnp.dot(p, v, preferred_element_type=F32)
            return carry + jnp.dot(jnp.exp2(s).astype(BF16), v, preferred_element_type=F32)

        acc0 = jnp.zeros((rows, LANES), F32)
        carry = (jnp.full((rows, 1), MASK_VALUE, F32), acc0) if online else acc0
        if band is None:
            def chunk(c, k0, g=g, scores=scores, update=update):
                return update(c, scores(kt_ref[0, g, :, pl.ds(k0, tk)]), v_ref[0, g, pl.ds(k0, tk), :])

            nk = kt_ref.shape[3] // tk
            for j in range(nk % 2):
                carry = chunk(carry, j * tk)

            def body(j, c, chunk=chunk, first=(nk % 2) * tk):
                k0 = pl.multiple_of(first + j * (2 * tk), tk)
                return chunk(chunk(c, k0), pl.multiple_of(k0 + tk, tk))

            carry = lax.fori_loop(0, nk // 2, body, carry)
        else:
            carry = update(carry, scores(kt_ref[0, g, :, 0:band]), v_ref[0, g, 0:band, :])
            b0 = pl.multiple_of(band + start, LANES)
            carry = update(carry, scores(kt_ref[0, g, :, pl.ds(b0, width)]), v_ref[0, g, pl.ds(b0, width), :], mask)
        acc = carry[1] if online else carry
        if use_sink:
            e = jnp.concatenate([t[:, HEAD_DIM:HEAD_DIM + 1].astype(F32) + sink_ref[g * GROUP + hh] * LOG2E
                                 for hh, t in enumerate(tiles)], axis=0)
            if online:
                e = e - carry[0]
            lane_r = lax.broadcasted_iota(jnp.int32, (rows, LANES), 1)
            acc = acc + jnp.where(lane_r >= HEAD_DIM, jnp.exp2(e), 0.0)
        o = acc * pltpu.roll(1.0 / acc, HEAD_DIM, 1)
        for pair in range(GROUP // 2):
            a = o[(2 * pair) * tq:(2 * pair + 1) * tq]
            b = o[(2 * pair + 1) * tq:(2 * pair + 2) * tq]
            t0 = (g * GROUP // 2 + pair) * LANES
            o_ref[0, :, t0:t0 + LANES] = jnp.where(low, a, pltpu.roll(b, HEAD_DIM, 1)).astype(o_ref.dtype)


def _attention(q, kt, v, sink, tq, tk, online, use_sink, band):
    b, l, _ = q.shape
    sk = kt.shape[3]
    tq = min(tq, l)
    tk = min(tk, sk)
    kern = functools.partial(_attn_kernel, tk=tk, online=online, use_sink=use_sink, band=band, seq=l)
    return pl.pallas_call(
        kern,
        out_shape=jax.ShapeDtypeStruct((b, l, ATT_W), BF16),
        grid_spec=pltpu.PrefetchScalarGridSpec(
            num_scalar_prefetch=1,
            grid=(b, l // tq),
            in_specs=[
                pl.BlockSpec((1, tq, N_HEADS * LANES), lambda bi, i, s: (bi, i, 0)),
                pl.BlockSpec((1, N_KV, LANES, sk), lambda bi, i, s: (bi, 0, 0, 0), pipeline_mode=pl.Buffered(1)),
                pl.BlockSpec((1, N_KV, sk, LANES), lambda bi, i, s: (bi, 0, 0, 0), pipeline_mode=pl.Buffered(1)),
            ],
            out_specs=pl.BlockSpec((1, tq, ATT_W), lambda bi, i, s: (bi, i, 0)),
        ),
        compiler_params=_cparams(("parallel", "arbitrary")),
    )(sink, q, kt, v)


def attention(q, kt, v, ubound, sink, tq, tk, use_sink=False, band=None):
    return lax.cond(ubound <= SAFE_BOUND,
                    lambda: _attention(q, kt, v, sink, tq, tk, False, use_sink, band),
                    lambda: _attention(q, kt, v, sink, tq, tk, True, use_sink, band))


def _merge_kernel(of_ref, ob_ref, hg_ref, b_ref, c_ref, ga_ref, gb_ref, gc_ref, x_ref,
                  wa_ref, wb_ref, wc_ref, wo_ref, hgn_ref, gate1_ref, n2_ref, sc2_ref, sh2_ref, wr_ref,
                  xo_ref, h2_ref, lg_ref):
    o = of_ref[0].astype(F32) + ob_ref[0].astype(F32)
    tiles = []
    for h in range(HG_HEADS):
        t = o[:, h * LANES:(h + 1) * LANES]
        ms = jnp.mean(t * t, axis=-1, keepdims=True)
        tiles.append(t * lax.rsqrt(ms + EPS) * hgn_ref[...])
    hg = hg_ref[...].astype(F32)
    a = (jnp.concatenate(tiles, axis=1) * (hg * _sigmoid(hg))).astype(BF16)
    merged = (_sigmoid(ga_ref[...].astype(F32)) * jnp.dot(a, wa_ref[...], preferred_element_type=F32)
              + _sigmoid(gb_ref[...].astype(F32)) * jnp.dot(b_ref[...], wb_ref[...], preferred_element_type=F32)
              + _sigmoid(gc_ref[...].astype(F32)) * jnp.dot(c_ref[...], wc_ref[...], preferred_element_type=F32))
    y = jnp.dot(merged.astype(BF16), wo_ref[...], preferred_element_type=F32)
    x = x_ref[...] + gate1_ref[0] * y
    xo_ref[...] = x
    ms = jnp.mean(x * x, axis=-1, keepdims=True)
    h2 = (x * lax.rsqrt(ms + EPS) * n2_ref[...]) * (1.0 + sc2_ref[0]) + sh2_ref[0]
    hi = h2.astype(BF16)
    lo = (h2 - hi.astype(F32)).astype(BF16)
    h2_ref[...] = hi
    lg_ref[...] = jnp.dot(jnp.concatenate([hi, lo, hi], axis=1), wr_ref[...], preferred_element_type=F32)


def merge(o_fb, p2d, b2d, c2d, x2d, rows_per_batch, wa, wb, wc, wo, hgn, gate1, n2, sc2, sh2, w_route, tm):
    t, d = x2d.shape
    tm = min(tm, rows_per_batch)
    per = rows_per_batch // tm
    row = lambda i: (i, 0)
    const = lambda i: (0, 0)
    bat = lambda i: (i // per, 0, 0)
    gcol = COL["gates"] // d
    return pl.pallas_call(
        _merge_kernel,
        out_shape=(jax.ShapeDtypeStruct((t, d), F32), jax.ShapeDtypeStruct((t, d), BF16),
                   jax.ShapeDtypeStruct((t, LANES), F32)),
        grid=(t // tm,),
        in_specs=[
            pl.BlockSpec((1, tm, HG_W), lambda i: (0, i, 0)),
            pl.BlockSpec((1, tm, HG_W), lambda i: (1, i, 0)),
            pl.BlockSpec((tm, HG_W), lambda i: (i, COL["hg"] // HG_W)),
            pl.BlockSpec((tm, ATT_W), row),
            pl.BlockSpec((tm, ATT_W), row),
            pl.BlockSpec((tm, d), lambda i: (i, gcol)),
            pl.BlockSpec((tm, d), lambda i: (i, gcol + 1)),
            pl.BlockSpec((tm, d), lambda i: (i, gcol + 2)),
            pl.BlockSpec((tm, d), row),
            pl.BlockSpec((HG_W, d), const),
            pl.BlockSpec((ATT_W, d), const),
            pl.BlockSpec((ATT_W, d), const),
            pl.BlockSpec((d, d), const),
            pl.BlockSpec((1, LANES), const),
            pl.BlockSpec((1, 1, d), bat),
            pl.BlockSpec((1, d), const),
            pl.BlockSpec((1, 1, d), bat),
            pl.BlockSpec((1, 1, d), bat),
            pl.BlockSpec((3 * d, LANES), const),
        ],
        out_specs=(pl.BlockSpec((tm, d), row), pl.BlockSpec((tm, d), row), pl.BlockSpec((tm, LANES), row)),
        compiler_params=_cparams(("parallel",)),
    )(o_fb, o_fb, p2d, b2d, c2d, p2d, p2d, p2d, x2d, wa, wb, wc, wo, hgn.reshape(1, LANES),
      gate1, n2.reshape(1, d), sc2, sh2, w_route)


_STRICT_LOWER = np.tril(np.ones((ROUTE_ROWS, ROUTE_ROWS), np.float32), -1)
_NEG_BIG = -3.0e38


def _lane_argmax(x, lane):
    top = jnp.max(x, axis=1, keepdims=True)
    idx = jnp.min(jnp.where(x == top, lane, LANES), axis=1, keepdims=True)
    return top, idx


def _route_kernel(lg_ref, tri_ref, o_ref, cnt_ref, run_scr):
    @pl.when(pl.program_id(0) == 0)
    def _():
        run_scr[...] = jnp.zeros_like(run_scr)

    lg = lg_ref[...]
    lane = lax.broadcasted_iota(jnp.int32, lg.shape, 1)
    is_grp = lane < N_GROUPS
    gtop, gidx = _lane_argmax(jnp.where(is_grp, lg, _NEG_BIG), lane)
    grp_w = 1.0 / jnp.sum(jnp.where(is_grp, jnp.exp(lg - gtop), 0.0), axis=1, keepdims=True)
    lo = N_GROUPS + EXP_PER_GROUP * gidx
    x1 = jnp.where((lane >= lo) & (lane < lo + EXP_PER_GROUP), lg, _NEG_BIG)
    t1, i1 = _lane_argmax(x1, lane)
    t2, i2 = _lane_argmax(jnp.where(lane == i1, _NEG_BIG, x1), lane)
    r = jnp.exp(t2 - t1)
    w1 = grp_w / (1.0 + r)
    w2 = w1 * r
    e1 = i1 - N_GROUPS
    e2 = i2 - N_GROUPS
    hot1 = lane == e1
    hot2 = lane == e2
    hot = jnp.where(hot1 | hot2, 1.0, 0.0)
    before = run_scr[...] + jnp.dot(tri_ref[...], hot.astype(BF16), preferred_element_type=F32)
    rank1 = jnp.sum(jnp.where(hot1, before, 0.0), axis=1, keepdims=True)
    rank2 = jnp.sum(jnp.where(hot2, before, 0.0), axis=1, keepdims=True)
    run_scr[...] = run_scr[...] + jnp.sum(hot, axis=0, keepdims=True)
    cnt_ref[...] = jnp.broadcast_to(run_scr[...], cnt_ref.shape)
    out = jnp.where(lane == 0, e1.astype(F32), jnp.where(lane == 1, e2.astype(F32), 0.0))
    out = jnp.where(lane == 2, w1, jnp.where(lane == 3, w2, out))
    o_ref[...] = jnp.where(lane == 4, rank1, jnp.where(lane == 5, rank2, out))


def route(logits):
    t = logits.shape[0]
    return pl.pallas_call(
        _route_kernel,
        out_shape=(jax.ShapeDtypeStruct((t, LANES), F32), jax.ShapeDtypeStruct((8, LANES), F32)),
        grid=(t // ROUTE_ROWS,),
        in_specs=[pl.BlockSpec((ROUTE_ROWS, LANES), lambda i: (i, 0)),
                  pl.BlockSpec((ROUTE_ROWS, ROUTE_ROWS), lambda i: (0, 0))],
        out_specs=(pl.BlockSpec((ROUTE_ROWS, LANES), lambda i: (i, 0)),
                   pl.BlockSpec((8, LANES), lambda i: (0, 0))),
        scratch_shapes=[pltpu.VMEM((1, LANES), F32)],
        compiler_params=_cparams(("arbitrary",)),
    )(logits, jnp.asarray(_STRICT_LOWER, BF16))


def _expert_kernel(be_ref, x_ref, wg_ref, wu_ref, wd_ref, *rest):
    o_ref, wg_s, wu_s, wd_s = rest[-4:]
    i = pl.program_id(0)

    @pl.when((i == 0) | (be_ref[i] != be_ref[jnp.maximum(i - 1, 0)]))
    def _():
        wg_s[...] = wg_ref[0, 0].astype(BF16)
        wu_s[...] = wu_ref[0, 0].astype(BF16)
        wd_s[...] = wd_ref[0, 0].astype(BF16)

    x = x_ref[...]
    gte = jnp.dot(x, wg_s[...], preferred_element_type=F32)
    up = jnp.dot(x, wu_s[...], preferred_element_type=F32)
    hid = (gte * _sigmoid(gte) * up).astype(BF16)
    o_ref[...] = jnp.dot(hid, wd_s[...], preferred_element_type=F32).astype(o_ref.dtype)


def expert_ffn(rows, block_expert, wg, wu, wd, layer, out_prev, block_off, n_total):
    n, d = rows.shape
    de = wg.shape[3]
    in_specs = [
        pl.BlockSpec((MOE_ROWS, d), lambda i, be: (i, 0)),
        pl.BlockSpec((1, 1, d, de), lambda i, be: (layer, be[i], 0, 0)),
        pl.BlockSpec((1, 1, d, de), lambda i, be: (layer, be[i], 0, 0)),
        pl.BlockSpec((1, 1, de, d), lambda i, be: (layer, be[i], 0, 0)),
    ]
    args = [block_expert, rows, wg, wu, wd]
    aliases = {}
    if out_prev is not None:
        in_specs.append(pl.BlockSpec(memory_space=pl.ANY))
        args.append(out_prev)
        aliases = {5: 0}
    return pl.pallas_call(
        _expert_kernel,
        out_shape=jax.ShapeDtypeStruct((n_total, d), BF16),
        grid_spec=pltpu.PrefetchScalarGridSpec(
            num_scalar_prefetch=1,
            grid=(n // MOE_ROWS,),
            in_specs=in_specs,
            out_specs=pl.BlockSpec((MOE_ROWS, d), lambda i, be: (i + block_off, 0)),
            scratch_shapes=[pltpu.VMEM((d, de), BF16), pltpu.VMEM((d, de), BF16), pltpu.VMEM((de, d), BF16)],
        ),
        input_output_aliases=aliases,
        compiler_params=_cparams(("arbitrary",)),
    )(*args)


MOE_PARTS = 4


def experts_overlapped(h_all, src_tok, block_expert, wg, wu, wd, layer):
    n_blocks = block_expert.shape[0]
    n_total = n_blocks * MOE_ROWS
    bounds = [n_blocks * k // MOE_PARTS for k in range(MOE_PARTS + 1)]
    out = None
    for b0, b1 in zip(bounds[:-1], bounds[1:]):
        rows = jnp.take(h_all, src_tok[b0 * MOE_ROWS:b1 * MOE_ROWS], axis=0, mode="clip")
        out = expert_ffn(rows, block_expert[b0:b1], wg, wu, wd, layer, out, b0, n_total)
    return out


def _combine_kernel(x_ref, r1_ref, r2_ref, info_ref, gate_ref, *rest, final):
    o_ref = rest[-1]
    info = info_ref[...]
    y = info[:, 2:3] * r1_ref[...].astype(F32) + info[:, 3:4] * r2_ref[...].astype(F32)
    x = x_ref[...] + gate_ref[0] * y
    if final:
        ms = jnp.mean(x * x, axis=-1, keepdims=True)
        x = x * lax.rsqrt(ms + EPS) * rest[0][...]
    o_ref[...] = x


def combine(x2d, r1, r2, info, row_off, gate, rows_per_batch, final_g=None):
    t, d = x2d.shape
    tm = min(ROUTE_ROWS, rows_per_batch)
    per = rows_per_batch // tm
    off = row_off // tm
    row = lambda i: (i, 0)
    shifted = lambda i: (i + off, 0)
    specs = [pl.BlockSpec((tm, d), row), pl.BlockSpec((tm, d), shifted), pl.BlockSpec((tm, d), shifted),
             pl.BlockSpec((tm, LANES), shifted), pl.BlockSpec((1, 1, d), lambda i: (i // per, 0, 0))]
    args = [x2d, r1, r2, info, gate]
    if final_g is not None:
        specs.append(pl.BlockSpec((1, d), lambda i: (0, 0)))
        args.append(final_g.reshape(1, d))
    return pl.pallas_call(
        functools.partial(_combine_kernel, final=final_g is not None),
        out_shape=jax.ShapeDtypeStruct((t, d), F32), grid=(t // tm,),
        in_specs=specs, out_specs=pl.BlockSpec((tm, d), row),
        compiler_params=_cparams(("parallel",)),
    )(*args)


def _dispatch(info, counts_row):
    n_tok = info.shape[0]
    expert = info[:, 0:TOP_K].astype(jnp.int32)
    rank = info[:, 4:4 + TOP_K].astype(jnp.int32)
    counts = counts_row[0, :N_EXPERTS].astype(jnp.int32)
    padded = (counts + MOE_ROWS - 1) // MOE_ROWS * MOE_ROWS
    pad_end = jnp.cumsum(padded)
    pad_start = pad_end - padded
    hot = expert[:, :, None] == jnp.arange(N_EXPERTS, dtype=jnp.int32)
    dest = jnp.sum(jnp.where(hot, pad_start, 0), axis=-1) + rank
    n_blocks = -(-(n_tok * TOP_K + N_EXPERTS * (MOE_ROWS - 1)) // MOE_ROWS)
    token = jnp.broadcast_to(jnp.arange(n_tok, dtype=jnp.int32)[:, None], dest.shape)
    src_tok = jnp.zeros((n_blocks * MOE_ROWS,), jnp.int32).at[dest.reshape(-1)].set(token.reshape(-1))
    block_start = jnp.arange(n_blocks, dtype=jnp.int32)[:, None] * MOE_ROWS
    block_expert = jnp.minimum(jnp.sum((pad_end[None, :] <= block_start).astype(jnp.int32), axis=1),
                               N_EXPERTS - 1)
    return src_tok, dest, block_expert


def _rope_tables(seq):
    n_rows = seq // GRID_W
    row = jnp.repeat(jnp.arange(n_rows), GRID_W).astype(F32)
    col = jnp.tile(jnp.arange(GRID_W), n_rows).astype(F32)
    axis_pairs = HEAD_DIM // 4
    inv = ROPE_THETA ** (-jnp.arange(axis_pairs, dtype=F32) / axis_pairs)
    ang = jnp.concatenate([row[:, None] * inv, col[:, None] * inv], axis=-1)
    cos, sin = jnp.cos(ang), jnp.sin(ang)
    cos128 = jnp.concatenate([cos, cos, cos, cos], axis=-1)
    sins128 = jnp.concatenate([-sin, sin, -sin, sin], axis=-1)
    return cos128, sins128


def kernel(x, c, ctx, c_ctx, w_mod, b_mod, norm1_g, norm2_g, w_in, hgrn_lb_logits, hgrn_out_norm_g,
           attn_q_norm_g, attn_k_norm_g, swa_sink, w_branch_a, w_branch_b, w_branch_c, w_out,
           w_group, w_router, w_exp_gate, w_exp_up, w_exp_down, final_norm_g):
    bsz, seq, d = x.shape
    n_ctx = ctx.shape[1]
    depth = w_mod.shape[0]
    cos, sins = _rope_tables(seq)
    lb_p = jax.nn.softmax(hgrn_lb_logits.astype(F32), axis=0)
    lower_bounds = jnp.cumsum(lb_p, axis=0) - lb_p[0]
    no_sink = jnp.zeros((N_HEADS,), F32)
    ones128 = jnp.ones((1, LANES), F32)

    xl = x.reshape(bsz * seq, d)
    xc = ctx.reshape(bsz * n_ctx, d)
    for layer in range(depth):
        ctx_out = layer < depth - 1
        mod_l = jax.nn.silu(c) @ w_mod[layer] + b_mod[layer]
        mod_c = jnp.broadcast_to(jax.nn.silu(c_ctx) @ w_mod[layer] + b_mod[layer], (bsz, 6 * d))
        ml = [m.reshape(bsz, 1, d) for m in jnp.split(mod_l, 6, axis=-1)]
        mc = [m.reshape(bsz, 1, d) for m in jnp.split(mod_c, 6, axis=-1)]

        w_in_b = w_in[layer].astype(BF16)
        p_l = inproj(xl, seq, norm1_g[layer], ml[1], ml[0], w_in_b, 1024, 3584)
        p_c = inproj(xc, n_ctx, norm1_g[layer], mc[1], mc[0], w_in_b, 256, 1024)
        p_l3 = p_l.reshape(bsz, seq, D_IN)
        p_c3 = p_c.reshape(bsz, n_ctx, D_IN)

        s0 = jnp.zeros((bsz, 2, HG_HEADS, HG_KDIM, HG_KDIM), F32)
        o_c, s_c = hgrn_scan(p_c3, lower_bounds[layer], s0, 256)
        o_l, _ = hgrn_scan(p_l3, lower_bounds[layer], s_c, 512)

        gq = jnp.tile(attn_q_norm_g[layer], 2).reshape(1, LANES)
        gk = jnp.tile(attn_k_norm_g[layer], 2).reshape(1, LANES)
        kt_l, v_l, kn_l = prep_kv(p_l3, COL["ak"], COL["av"], gk, cos, sins, True, True, 512)
        kt_c, v_c, kn_c = prep_kv(p_c3, COL["ak"], COL["av"], gk, cos, sins, True, False, 256)
        kmax = _key_norm_max([kn_l, kn_c])
        q_l, u_l = prep_q(p_l3, COL["aq"], gq, cos, sins, kmax, True, True, 512)
        kt_all = jnp.concatenate([kt_c, kt_l], axis=3)
        v_all = jnp.concatenate([v_c, v_l], axis=2)
        b_l = attention(q_l, kt_all, v_all, jnp.max(u_l), no_sink, 512, 1280)

        skt_l, sv_l, skn_l = prep_kv(p_l3, COL["sk"], COL["sv"], ones128, cos, sins, False, True, 512)
        skt_c, sv_c, skn_c = prep_kv(p_c3, COL["sk"], COL["sv"], ones128, cos, sins, False, False, 256)
        skmax = _key_norm_max([skn_l, skn_c])
        sq_l, su_l = prep_q(p_l3, COL["sq"], ones128, cos, sins, skmax, False, True, 512)
        zk = jnp.zeros((bsz, N_KV, LANES, WINDOW), BF16)
        zv = jnp.zeros((bsz, N_KV, WINDOW, LANES), BF16)
        skt_all = jnp.concatenate([skt_c, zk, skt_l, zk], axis=3)
        sv_all = jnp.concatenate([sv_c, zv, sv_l, zv], axis=2)
        c_l = attention(sq_l, skt_all, sv_all, jnp.max(su_l), swa_sink[layer], 256, 256,
                        use_sink=True, band=n_ctx)

        wa = w_branch_a[layer].astype(BF16)
        wb = w_branch_b[layer].astype(BF16)
        wc = w_branch_c[layer].astype(BF16)
        wo = w_out[layer].astype(BF16)
        w_r = jnp.concatenate(
            [w_group[layer], w_router[layer], jnp.zeros((d, LANES - N_GROUPS - N_EXPERTS), F32)], axis=1)
        w_r_hi = w_r.astype(BF16)
        w_r_lo = (w_r - w_r_hi.astype(F32)).astype(BF16)
        w_route = jnp.concatenate([w_r_hi, w_r_hi, w_r_lo], axis=0)
        xl, h_l, lg_l = merge(o_l.reshape(2, bsz * seq, HG_W), p_l, b_l.reshape(-1, ATT_W),
                              c_l.reshape(-1, ATT_W), xl, seq, wa, wb, wc, wo, hgrn_out_norm_g[layer],
                              ml[2], norm2_g[layer], ml[4], ml[3], w_route, 256)
        if ctx_out:
            q_c, u_c = prep_q(p_c3, COL["aq"], gq, cos, sins, kmax, True, False, 256)
            b_c = attention(q_c, kt_c, v_c, jnp.max(u_c), no_sink, 256, 256)
            sq_c, su_c = prep_q(p_c3, COL["sq"], ones128, cos, sins, skmax, False, False, 256)
            c_c = attention(sq_c, skt_c, sv_c, jnp.max(su_c), swa_sink[layer], 256, 256, use_sink=True)
            xc, h_c, lg_c = merge(o_c.reshape(2, bsz * n_ctx, HG_W), p_c, b_c.reshape(-1, ATT_W),
                                  c_c.reshape(-1, ATT_W), xc, n_ctx, wa, wb, wc, wo, hgrn_out_norm_g[layer],
                                  mc[2], norm2_g[layer], mc[4], mc[3], w_route, 256)
            h_all = jnp.concatenate([h_c, h_l], axis=0)
            lg_all = jnp.concatenate([lg_c, lg_l], axis=0)
        else:
            h_all, lg_all = h_l, lg_l

        info, counts_row = route(lg_all)
        src_tok, dest, block_expert = _dispatch(info, counts_row)
        out_rows = experts_overlapped(h_all, src_tok, block_expert, w_exp_gate, w_exp_up, w_exp_down, layer)
        r1 = jnp.take(out_rows, dest[:, 0], axis=0, mode="clip")
        r2 = jnp.take(out_rows, dest[:, 1], axis=0, mode="clip")
        n_c = bsz * n_ctx if ctx_out else 0
        if ctx_out:
            xc = combine(xc, r1, r2, info, 0, mc[5], n_ctx)
        xl = combine(xl, r1, r2, info, n_c, ml[5], seq, final_g=final_norm_g if layer == depth - 1 else None)
    return xl.reshape(bsz, seq, d)
```

```python
import functools
import math

import numpy as np
import jax
import jax.numpy as jnp
from jax import lax
from jax.experimental import pallas as pl
from jax.experimental.pallas import tpu as pltpu

F32 = jnp.float32
BF16 = jnp.bfloat16

EPS = 1e-6
MASK_VALUE = -1e30
TINY = 1e-30
GRID_W = 64
ROPE_THETA = 10000.0

HG_HEADS = 4
HG_KDIM = 128
HG_W = HG_HEADS * HG_KDIM
HEAD_DIM = 64
N_HEADS = 8
N_KV = 2
GROUP = N_HEADS // N_KV
ATT_W = N_HEADS * HEAD_DIM
KV_W = N_KV * HEAD_DIM
WINDOW = 128
ATTN_SCALE = HEAD_DIM ** -0.5
N_GROUPS = 4
EXP_PER_GROUP = 8
N_EXPERTS = N_GROUPS * EXP_PER_GROUP
TOP_K = 2

LANES = 128
HG_CHUNK = 128
HG_LEVELS = int(math.log2(HG_CHUNK))
HG_TOTAL_ROWS = 16
MOE_ROWS = 256
ROUTE_ROWS = 512
VMEM_LIMIT = 56 * 1024 * 1024

LOG2E = 1.4426950408889634
SHIFT_HEADROOM = 57.0
SAFE_BOUND = 90.0


def _cparams(sem):
    return pltpu.CompilerParams(dimension_semantics=sem, vmem_limit_bytes=VMEM_LIMIT)


_SIZES = (HG_W, HG_W, HG_W, HG_W, HG_W, ATT_W, KV_W, KV_W, ATT_W, KV_W, KV_W, 3 * 1024)
_NAMES = ("hq", "ff", "fb", "hi", "hg", "aq", "ak", "av", "sq", "sk", "sv", "gates")
COL = {n: int(sum(_SIZES[:i])) for i, n in enumerate(_NAMES)}
D_IN = int(sum(_SIZES))


def _inproj_kernel(x_ref, g_ref, sc_ref, sh_ref, w_ref, o_ref, h_scr):
    @pl.when(pl.program_id(1) == 0)
    def _():
        x = x_ref[...]
        ms = jnp.mean(x * x, axis=-1, keepdims=True)
        y = x * lax.rsqrt(ms + EPS) * g_ref[...]
        h_scr[...] = (y * (1.0 + sc_ref[0]) + sh_ref[0]).astype(BF16)

    o_ref[...] = jnp.dot(h_scr[...], w_ref[...], preferred_element_type=F32).astype(o_ref.dtype)


def inproj(x2d, rows_per_batch, g, scale, shift, w_bf16, tm, tn):
    t, d = x2d.shape
    n = w_bf16.shape[1]
    tm = min(tm, rows_per_batch)
    per = rows_per_batch // tm
    return pl.pallas_call(
        _inproj_kernel,
        out_shape=jax.ShapeDtypeStruct((t, n), BF16),
        grid=(t // tm, n // tn),
        in_specs=[
            pl.BlockSpec((tm, d), lambda i, j: (i, 0)),
            pl.BlockSpec((1, d), lambda i, j: (0, 0)),
            pl.BlockSpec((1, 1, d), lambda i, j: (i // per, 0, 0)),
            pl.BlockSpec((1, 1, d), lambda i, j: (i // per, 0, 0)),
            pl.BlockSpec((d, tn), lambda i, j: (0, j)),
        ],
        out_specs=pl.BlockSpec((tm, tn), lambda i, j: (i, j)),
        scratch_shapes=[pltpu.VMEM((tm, d), BF16)],
        compiler_params=_cparams(("parallel", "arbitrary")),
    )(x2d, g.reshape(1, d), scale, shift, w_bf16)


def _hgrn_exponent_matrices():
    c, nl = HG_CHUNK, HG_LEVELS
    rows = (nl + 2) * c + HG_TOTAL_ROWS
    out = np.zeros((2, rows, c), np.float32)
    for d in range(2):
        pos = np.arange(c) if d == 0 else c - 1 - np.arange(c)
        for l in range(nl):
            m = 1 << l
            for t in range(c):
                p = pos[t]
                mid = (p // (2 * m)) * 2 * m + m
                if p >= mid:
                    sel = (pos >= mid) & (pos <= p)
                else:
                    sel = (pos > p) & (pos <= mid - 1)
                out[d, l * c + t, sel] = 1.0
        for t in range(c):
            out[d, nl * c + t, pos <= pos[t]] = 1.0
            out[d, (nl + 1) * c + t, pos > pos[t]] = 1.0
        out[d, (nl + 2) * c:, :] = 1.0
    return np.concatenate([out, out], axis=2)


_HG_EXP_MATS = _hgrn_exponent_matrices()


def _sigmoid(x):
    return 1.0 / (1.0 + jnp.exp(-x))


def _hgrn_kernel(q_ref, f_ref, i_ref, lb_ref, mat_ref, s0_ref, o_ref, st_ref, *, n_chunks):
    c, nl = HG_CHUNK, HG_LEVELS
    d = pl.program_id(1)

    @pl.when(pl.program_id(2) == 0)
    def _():
        st_ref[...] = s0_ref[...]

    row = lax.broadcasted_iota(jnp.int32, (c, HG_W), 0)
    pos = row + d * (c - 1 - 2 * row)
    r_i = lax.broadcasted_iota(jnp.int32, (c, c), 0)
    c_i = lax.broadcasted_iota(jnp.int32, (c, c), 1)
    xor_rc = r_i ^ c_i
    top_bit = sum((xor_rc >= (1 << j)).astype(jnp.int32) for j in range(1, nl))
    pair_level = jnp.where((r_i - c_i) * (1 - 2 * d) > 0, top_bit, -1)

    def chunk(ci, carry):
        cc = ci + d * (n_chunks - 1 - 2 * ci)
        r0 = pl.multiple_of(cc * c, c)
        qraw = q_ref[0, pl.ds(r0, c), :].astype(F32)
        fz = f_ref[0, pl.ds(r0, c), :].astype(F32)
        vb = i_ref[0, pl.ds(r0, c), :]
        v = vb.astype(F32)
        lb = lb_ref[...]
        sig = _sigmoid(fz)
        logf = jnp.log(jnp.maximum(lb + (1.0 - lb) * sig, TINY))
        key = (1.0 - lb) * (1.0 - sig)
        qh = qraw * _sigmoid(qraw) * (HG_KDIM ** -0.5)
        logf2 = logf * LOG2E
        hi = logf2.astype(BF16)
        lo = (logf2 - hi.astype(F32)).astype(BF16)
        hilo = jnp.concatenate([hi, lo], axis=0)

        dec_all = jnp.exp2(jnp.dot(mat_ref[0], hilo, preferred_element_type=F32))

        def decay(r_lo, r_hi):
            return dec_all[r_lo:r_hi]

        a = [jnp.zeros((c, c), F32)] * HG_HEADS
        for l in range(nl):
            x = (jnp.where(((pos >> l) & 1) == 1, qh, key) * decay(l * c, (l + 1) * c)).astype(BF16)
            for h in range(HG_HEADS):
                xh = x[:, h * LANES:(h + 1) * LANES]
                prod = lax.dot_general(xh, xh, (((1,), (1,)), ((), ())), preferred_element_type=F32)
                a[h] = jnp.where(pair_level == l, prod, a[h])
        q_in = (qh * decay(nl * c, (nl + 1) * c)).astype(BF16)
        k_out = (key * decay((nl + 1) * c, (nl + 2) * c)).astype(BF16)
        total = decay((nl + 2) * c, (nl + 2) * c + HG_TOTAL_ROWS)[0:1]
        qk = qh * key
        for h in range(HG_HEADS):
            lanes = slice(h * LANES, (h + 1) * LANES)
            st = st_ref[0, 0, h]
            o = (jnp.dot(a[h].astype(BF16), vb[:, lanes], preferred_element_type=F32)
                 + jnp.sum(qk[:, lanes], axis=1, keepdims=True) * v[:, lanes]
                 + lax.dot_general(q_in[:, lanes], st.astype(BF16), (((1,), (1,)), ((), ())),
                                   preferred_element_type=F32))
            o_ref[0, 0, pl.ds(r0, c), lanes] = o.astype(o_ref.dtype)
            st_ref[0, 0, h] = total[:, lanes] * st + lax.dot_general(
                vb[:, lanes], k_out[:, lanes], (((0,), (0,)), ((), ())), preferred_element_type=F32)
        return carry

    lax.fori_loop(0, n_chunks, chunk, 0, unroll=2 if n_chunks % 2 == 0 else 1)


def hgrn_scan(p3, lb, s0, tb):
    b, l, _ = p3.shape
    tb = min(tb, l)
    nb = l // tb
    blk = lambda d, n: n + d * (nb - 1 - 2 * n)
    wblk = HG_W
    kern = functools.partial(_hgrn_kernel, n_chunks=tb // HG_CHUNK)
    mats = jnp.asarray(_HG_EXP_MATS, BF16)
    return pl.pallas_call(
        kern,
        out_shape=(jax.ShapeDtypeStruct((2, b, l, HG_W), BF16),
                   jax.ShapeDtypeStruct((b, 2, HG_HEADS, HG_KDIM, HG_KDIM), F32)),
        grid=(b, 2, nb),
        in_specs=[
            pl.BlockSpec((1, tb, wblk), lambda bi, d, n: (bi, blk(d, n), COL["hq"] // wblk)),
            pl.BlockSpec((1, tb, wblk), lambda bi, d, n: (bi, blk(d, n), COL["ff"] // wblk + d)),
            pl.BlockSpec((1, tb, wblk), lambda bi, d, n: (bi, blk(d, n), COL["hi"] // wblk)),
            pl.BlockSpec((1, HG_W), lambda bi, d, n: (0, 0)),
            pl.BlockSpec((1,) + _HG_EXP_MATS.shape[1:], lambda bi, d, n: (d, 0, 0)),
            pl.BlockSpec((1, 1, HG_HEADS, HG_KDIM, HG_KDIM), lambda bi, d, n: (bi, d, 0, 0, 0)),
        ],
        out_specs=(
            pl.BlockSpec((1, 1, tb, HG_W), lambda bi, d, n: (d, bi, blk(d, n), 0)),
            pl.BlockSpec((1, 1, HG_HEADS, HG_KDIM, HG_KDIM), lambda bi, d, n: (bi, d, 0, 0, 0)),
        ),
        compiler_params=_cparams(("parallel", "arbitrary", "arbitrary")),
    )(p3, p3, p3, lb.reshape(1, HG_W), mats, s0)


_GROUP_ONES = np.kron(np.eye(LANES // HEAD_DIM, dtype=np.float32), np.ones((HEAD_DIM, HEAD_DIM), np.float32))
_GROUP_ONES2 = np.concatenate([_GROUP_ONES, _GROUP_ONES], axis=0)


def _group_ssq(x, ones2):
    sq = x * x
    hi = sq.astype(BF16)
    lo = (sq - hi.astype(F32)).astype(BF16)
    return jnp.dot(jnp.concatenate([hi, lo], axis=1), ones2, preferred_element_type=F32)


def _prep_tile(x, g, cos, sins, ones2, norm, rope):
    if norm:
        x = x * lax.rsqrt(_group_ssq(x, ones2) * (1.0 / HEAD_DIM) + EPS) * g
    if rope:
        lane = lax.broadcasted_iota(jnp.int32, x.shape, 1)
        first = (lane % HEAD_DIM) < (HEAD_DIM // 2)
        other = jnp.where(first, pltpu.roll(x, LANES - HEAD_DIM // 2, 1), pltpu.roll(x, HEAD_DIM // 2, 1))
        x = x * cos + other * sins
    return x


def _prep_kv_kernel(k_ref, v_ref, g_ref, cos_ref, sin_ref, ones_ref, kt_ref, vo_ref, kn_ref, *, norm, rope):
    k = _prep_tile(k_ref[0].astype(F32), g_ref[...], cos_ref[...], sin_ref[...], ones_ref[...], norm, rope)
    kb = k.astype(BF16).astype(F32)
    ssq = _group_ssq(kb, ones_ref[...])
    kn_ref[0, 0] = jnp.broadcast_to(jnp.max(ssq, axis=0, keepdims=True), (8, LANES))
    kt = kb.T
    row = lax.broadcasted_iota(jnp.int32, kt.shape, 0)
    v = v_ref[0]
    low = lax.broadcasted_iota(jnp.int32, v.shape, 1) < HEAD_DIM
    one = jnp.ones((), v.dtype)
    for g in range(N_KV):
        ktg = kt if g == 0 else jnp.concatenate([kt[HEAD_DIM:], kt[:HEAD_DIM]], axis=0)
        kt_ref[0, g] = jnp.where(row < HEAD_DIM, ktg, jnp.where(row == HEAD_DIM, 1.0, 0.0)).astype(BF16)
        vg = v if g == 0 else jnp.concatenate([v[:, HEAD_DIM:], v[:, :HEAD_DIM]], axis=1)
        vo_ref[0, g] = jnp.where(low, vg, one).astype(BF16)


def _prep_q_kernel(kmax_ref, x0_ref, x1_ref, x2_ref, x3_ref, g_ref, cos_ref, sin_ref, ones_ref, o_ref, u_ref,
                   *, norm, rope):
    b = pl.program_id(0)
    lane = lax.broadcasted_iota(jnp.int32, (x0_ref.shape[1], LANES), 1)
    umax = None
    for tile, x_ref in enumerate((x0_ref, x1_ref, x2_ref, x3_ref)):
        kv = (2 * tile) // GROUP
        x = x_ref[0].astype(F32)
        y = _prep_tile(x, g_ref[...], cos_ref[...], sin_ref[...], ones_ref[...], norm, rope) * (ATTN_SCALE * LOG2E)
        yb = y.astype(BF16).astype(F32)
        u2 = jnp.sqrt(_group_ssq(yb, ones_ref[...])) * kmax_ref[b * N_KV + kv]
        tmax = jnp.max(u2, axis=0, keepdims=True)
        umax = tmax if umax is None else jnp.maximum(umax, tmax)
        neg_shift = SHIFT_HEADROOM - u2
        y_sw = pltpu.roll(yb, HEAD_DIM, 1)
        ns_sw = pltpu.roll(neg_shift, HEAD_DIM, 1)
        for half in range(2):
            h = 2 * tile + half
            data = yb if half == 0 else y_sw
            ns = ns_sw if half == 0 else neg_shift
            o_ref[0, :, h * LANES:(h + 1) * LANES] = jnp.where(
                lane < HEAD_DIM, data, jnp.where(lane == HEAD_DIM, ns, 0.0)).astype(BF16)
    u_ref[0, 0] = jnp.broadcast_to(umax, (8, LANES))


def prep_kv(p3, col_k, col_v, g128, cos, sins, norm, rope, tq):
    b, l, _ = p3.shape
    tq = min(tq, l)
    ones2 = jnp.asarray(_GROUP_ONES2, BF16)
    return pl.pallas_call(
        functools.partial(_prep_kv_kernel, norm=norm, rope=rope),
        out_shape=(jax.ShapeDtypeStruct((b, N_KV, LANES, l), BF16),
                   jax.ShapeDtypeStruct((b, N_KV, l, LANES), BF16),
                   jax.ShapeDtypeStruct((b, l // tq, 8, LANES), F32)),
        grid=(b, l // tq),
        in_specs=[
            pl.BlockSpec((1, tq, KV_W), lambda bi, i: (bi, i, col_k // KV_W)),
            pl.BlockSpec((1, tq, KV_W), lambda bi, i: (bi, i, col_v // KV_W)),
            pl.BlockSpec((1, LANES), lambda bi, i: (0, 0)),
            pl.BlockSpec((tq, LANES), lambda bi, i: (i, 0)),
            pl.BlockSpec((tq, LANES), lambda bi, i: (i, 0)),
            pl.BlockSpec((2 * LANES, LANES), lambda bi, i: (0, 0)),
        ],
        out_specs=(pl.BlockSpec((1, N_KV, LANES, tq), lambda bi, i: (bi, 0, 0, i)),
                   pl.BlockSpec((1, N_KV, tq, LANES), lambda bi, i: (bi, 0, i, 0)),
                   pl.BlockSpec((1, 1, 8, LANES), lambda bi, i: (bi, i, 0, 0))),
        compiler_params=_cparams(("parallel", "parallel")),
    )(p3, p3, g128, cos, sins, ones2)


def prep_q(p3, col, g128, cos, sins, kmax, norm, rope, tq):
    b, l, _ = p3.shape
    tq = min(tq, l)
    ones2 = jnp.asarray(_GROUP_ONES2, BF16)
    return pl.pallas_call(
        functools.partial(_prep_q_kernel, norm=norm, rope=rope),
        out_shape=(jax.ShapeDtypeStruct((b, l, N_HEADS * LANES), BF16),
                   jax.ShapeDtypeStruct((b, l // tq, 8, LANES), F32)),
        grid_spec=pltpu.PrefetchScalarGridSpec(
            num_scalar_prefetch=1,
            grid=(b, l // tq),
            in_specs=[
                pl.BlockSpec((1, tq, LANES), lambda bi, i, s, t=t: (bi, i, col // LANES + t))
                for t in range(ATT_W // LANES)
            ] + [
                pl.BlockSpec((1, LANES), lambda bi, i, s: (0, 0)),
                pl.BlockSpec((tq, LANES), lambda bi, i, s: (i, 0)),
                pl.BlockSpec((tq, LANES), lambda bi, i, s: (i, 0)),
                pl.BlockSpec((2 * LANES, LANES), lambda bi, i, s: (0, 0)),
            ],
            out_specs=(pl.BlockSpec((1, tq, N_HEADS * LANES), lambda bi, i, s: (bi, i, 0)),
                       pl.BlockSpec((1, 1, 8, LANES), lambda bi, i, s: (bi, i, 0, 0))),
        ),
        compiler_params=_cparams(("parallel", "parallel")),
    )(kmax, p3, p3, p3, p3, g128, cos, sins, ones2)


def _key_norm_max(kn_list):
    kn = functools.reduce(jnp.maximum, [jnp.max(k, axis=(1, 2)) for k in kn_list])
    return jnp.sqrt(kn[:, ::HEAD_DIM]).reshape(-1)


def _attn_kernel(sink_ref, q_ref, kt_ref, v_ref, o_ref, *, tk, online, use_sink, band, seq):
    tq = q_ref.shape[1]
    rows = GROUP * tq
    lane = lax.broadcasted_iota(jnp.int32, (tq, LANES), 1)
    low = lane < HEAD_DIM
    if band is not None:
        width = tq + 2 * WINDOW
        start = pl.multiple_of(pl.program_id(1) * tq, tq)
        r = lax.broadcasted_iota(jnp.int32, (rows, width), 0) % tq
        kp = lax.broadcasted_iota(jnp.int32, (rows, width), 1)
        key_pos = start + kp - WINDOW
        mask = (jnp.abs(kp - WINDOW - r) <= WINDOW) & (key_pos >= 0) & (key_pos < seq)
    for g in range(N_KV):
        tiles = [q_ref[0, :, (g * GROUP + hh) * LANES:(g * GROUP + hh + 1) * LANES] for hh in range(GROUP)]
        qg = jnp.concatenate(tiles, axis=0)

        def scores(kt, qg=qg):
            return jnp.dot(qg, kt, preferred_element_type=F32)

        def update(carry, s, v, msk=None):
            if msk is not None:
                s = jnp.where(msk, s, MASK_VALUE)
            if online:
                m, acc = carry
                m_new = jnp.maximum(m, jnp.max(s, axis=1, keepdims=True))
                p = jnp.exp2(s - m_new).astype(BF16)
                return m_new, jnp.exp2(m - m_new) * acc + jnp.dot(p, v, preferred_element_type=F32)
            return carry + jnp.dot(jnp.exp2(s).astype(BF16), v, preferred_element_type=F32)

        acc0 = jnp.zeros((rows, LANES), F32)
        carry = (jnp.full((rows, 1), MASK_VALUE, F32), acc0) if online else acc0
        if band is None:
            def chunk(c, k0, g=g, scores=scores, update=update):
                return update(c, scores(kt_ref[0, g, :, pl.ds(k0, tk)]), v_ref[0, g, pl.ds(k0, tk), :])

            nk = kt_ref.shape[3] // tk
            for j in range(nk % 2):
                carry = chunk(carry, j * tk)

            def body(j, c, chunk=chunk, first=(nk % 2) * tk):
                k0 = pl.multiple_of(first + j * (2 * tk), tk)
                return chunk(chunk(c, k0), pl.multiple_of(k0 + tk, tk))

            carry = lax.fori_loop(0, nk // 2, body, carry)
        else:
            carry = update(carry, scores(kt_ref[0, g, :, 0:band]), v_ref[0, g, 0:band, :])
            b0 = pl.multiple_of(band + start, LANES)
            carry = update(carry, scores(kt_ref[0, g, :, pl.ds(b0, width)]), v_ref[0, g, pl.ds(b0, width), :], mask)
        acc = carry[1] if online else carry
        if use_sink:
            e = jnp.concatenate([t[:, HEAD_DIM:HEAD_DIM + 1].astype(F32) + sink_ref[g * GROUP + hh] * LOG2E
                                 for hh, t in enumerate(tiles)], axis=0)
            if online:
                e = e - carry[0]
            lane_r = lax.broadcasted_iota(jnp.int32, (rows, LANES), 1)
            acc = acc + jnp.where(lane_r >= HEAD_DIM, jnp.exp2(e), 0.0)
        o = acc * pltpu.roll(1.0 / acc, HEAD_DIM, 1)
        for pair in range(GROUP // 2):
            a = o[(2 * pair) * tq:(2 * pair + 1) * tq]
            b = o[(2 * pair + 1) * tq:(2 * pair + 2) * tq]
            t0 = (g * GROUP // 2 + pair) * LANES
            o_ref[0, :, t0:t0 + LANES] = jnp.where(low, a, pltpu.roll(b, HEAD_DIM, 1)).astype(o_ref.dtype)


def _attention(q, kt, v, sink, tq, tk, online, use_sink, band):
    b, l, _ = q.shape
    sk = kt.shape[3]
    tq = min(tq, l)
    tk = min(tk, sk)
    kern = functools.partial(_attn_kernel, tk=tk, online=online, use_sink=use_sink, band=band, seq=l)
    return pl.pallas_call(
        kern,
        out_shape=jax.ShapeDtypeStruct((b, l, ATT_W), BF16),
        grid_spec=pltpu.PrefetchScalarGridSpec(
            num_scalar_prefetch=1,
            grid=(b, l // tq),
            in_specs=[
                pl.BlockSpec((1, tq, N_HEADS * LANES), lambda bi, i, s: (bi, i, 0)),
                pl.BlockSpec((1, N_KV, LANES, sk), lambda bi, i, s: (bi, 0, 0, 0), pipeline_mode=pl.Buffered(1)),
                pl.BlockSpec((1, N_KV, sk, LANES), lambda bi, i, s: (bi, 0, 0, 0), pipeline_mode=pl.Buffered(1)),
            ],
            out_specs=pl.BlockSpec((1, tq, ATT_W), lambda bi, i, s: (bi, i, 0)),
        ),
        compiler_params=_cparams(("parallel", "arbitrary")),
    )(sink, q, kt, v)


def attention(q, kt, v, ubound, sink, tq, tk, use_sink=False, band=None):
    return lax.cond(ubound <= SAFE_BOUND,
                    lambda: _attention(q, kt, v, sink, tq, tk, False, use_sink, band),
                    lambda: _attention(q, kt, v, sink, tq, tk, True, use_sink, band))


def _merge_kernel(of_ref, ob_ref, hg_ref, b_ref, c_ref, ga_ref, gb_ref, gc_ref, x_ref,
                  wa_ref, wb_ref, wc_ref, wo_ref, hgn_ref, gate1_ref, n2_ref, sc2_ref, sh2_ref, wr_ref,
                  xo_ref, h2_ref, lg_ref):
    o = of_ref[0].astype(F32) + ob_ref[0].astype(F32)
    tiles = []
    for h in range(HG_HEADS):
        t = o[:, h * LANES:(h + 1) * LANES]
        ms = jnp.mean(t * t, axis=-1, keepdims=True)
        tiles.append(t * lax.rsqrt(ms + EPS) * hgn_ref[...])
    hg = hg_ref[...].astype(F32)
    a = (jnp.concatenate(tiles, axis=1) * (hg * _sigmoid(hg))).astype(BF16)
    merged = (_sigmoid(ga_ref[...].astype(F32)) * jnp.dot(a, wa_ref[...], preferred_element_type=F32)
              + _sigmoid(gb_ref[...].astype(F32)) * jnp.dot(b_ref[...], wb_ref[...], preferred_element_type=F32)
              + _sigmoid(gc_ref[...].astype(F32)) * jnp.dot(c_ref[...], wc_ref[...], preferred_element_type=F32))
    y = jnp.dot(merged.astype(BF16), wo_ref[...], preferred_element_type=F32)
    x = x_ref[...] + gate1_ref[0] * y
    xo_ref[...] = x
    ms = jnp.mean(x * x, axis=-1, keepdims=True)
    h2 = (x * lax.rsqrt(ms + EPS) * n2_ref[...]) * (1.0 + sc2_ref[0]) + sh2_ref[0]
    hi = h2.astype(BF16)
    lo = (h2 - hi.astype(F32)).astype(BF16)
    h2_ref[...] = hi
    lg_ref[...] = jnp.dot(jnp.concatenate([hi, lo, hi], axis=1), wr_ref[...], preferred_element_type=F32)


def merge(o_fb, p2d, b2d, c2d, x2d, rows_per_batch, wa, wb, wc, wo, hgn, gate1, n2, sc2, sh2, w_route, tm):
    t, d = x2d.shape
    tm = min(tm, rows_per_batch)
    per = rows_per_batch // tm
    row = lambda i: (i, 0)
    const = lambda i: (0, 0)
    bat = lambda i: (i // per, 0, 0)
    gcol = COL["gates"] // d
    return pl.pallas_call(
        _merge_kernel,
        out_shape=(jax.ShapeDtypeStruct((t, d), F32), jax.ShapeDtypeStruct((t, d), BF16),
                   jax.ShapeDtypeStruct((t, LANES), F32)),
        grid=(t // tm,),
        in_specs=[
            pl.BlockSpec((1, tm, HG_W), lambda i: (0, i, 0)),
            pl.BlockSpec((1, tm, HG_W), lambda i: (1, i, 0)),
            pl.BlockSpec((tm, HG_W), lambda i: (i, COL["hg"] // HG_W)),
            pl.BlockSpec((tm, ATT_W), row),
            pl.BlockSpec((tm, ATT_W), row),
            pl.BlockSpec((tm, d), lambda i: (i, gcol)),
            pl.BlockSpec((tm, d), lambda i: (i, gcol + 1)),
            pl.BlockSpec((tm, d), lambda i: (i, gcol + 2)),
            pl.BlockSpec((tm, d), row),
            pl.BlockSpec((HG_W, d), const),
            pl.BlockSpec((ATT_W, d), const),
            pl.BlockSpec((ATT_W, d), const),
            pl.BlockSpec((d, d), const),
            pl.BlockSpec((1, LANES), const),
            pl.BlockSpec((1, 1, d), bat),
            pl.BlockSpec((1, d), const),
            pl.BlockSpec((1, 1, d), bat),
            pl.BlockSpec((1, 1, d), bat),
            pl.BlockSpec((3 * d, LANES), const),
        ],
        out_specs=(pl.BlockSpec((tm, d), row), pl.BlockSpec((tm, d), row), pl.BlockSpec((tm, LANES), row)),
        compiler_params=_cparams(("parallel",)),
    )(o_fb, o_fb, p2d, b2d, c2d, p2d, p2d, p2d, x2d, wa, wb, wc, wo, hgn.reshape(1, LANES),
      gate1, n2.reshape(1, d), sc2, sh2, w_route)


_STRICT_LOWER = np.tril(np.ones((ROUTE_ROWS, ROUTE_ROWS), np.float32), -1)
_NEG_BIG = -3.0e38


def _lane_argmax(x, lane):
    top = jnp.max(x, axis=1, keepdims=True)
    idx = jnp.min(jnp.where(x == top, lane, LANES), axis=1, keepdims=True)
    return top, idx


def _route_kernel(lg_ref, tri_ref, o_ref, cnt_ref, run_scr):
    @pl.when(pl.program_id(0) == 0)
    def _():
        run_scr[...] = jnp.zeros_like(run_scr)

    lg = lg_ref[...]
    lane = lax.broadcasted_iota(jnp.int32, lg.shape, 1)
    is_grp = lane < N_GROUPS
    gtop, gidx = _lane_argmax(jnp.where(is_grp, lg, _NEG_BIG), lane)
    grp_w = 1.0 / jnp.sum(jnp.where(is_grp, jnp.exp(lg - gtop), 0.0), axis=1, keepdims=True)
    lo = N_GROUPS + EXP_PER_GROUP * gidx
    x1 = jnp.where((lane >= lo) & (lane < lo + EXP_PER_GROUP), lg, _NEG_BIG)
    t1, i1 = _lane_argmax(x1, lane)
    t2, i2 = _lane_argmax(jnp.where(lane == i1, _NEG_BIG, x1), lane)
    r = jnp.exp(t2 - t1)
    w1 = grp_w / (1.0 + r)
    w2 = w1 * r
    e1 = i1 - N_GROUPS
    e2 = i2 - N_GROUPS
    hot1 = lane == e1
    hot2 = lane == e2
    hot = jnp.where(hot1 | hot2, 1.0, 0.0)
    before = run_scr[...] + jnp.dot(tri_ref[...], hot.astype(BF16), preferred_element_type=F32)
    rank1 = jnp.sum(jnp.where(hot1, before, 0.0), axis=1, keepdims=True)
    rank2 = jnp.sum(jnp.where(hot2, before, 0.0), axis=1, keepdims=True)
    run_scr[...] = run_scr[...] + jnp.sum(hot, axis=0, keepdims=True)
    cnt_ref[...] = jnp.broadcast_to(run_scr[...], cnt_ref.shape)
    out = jnp.where(lane == 0, e1.astype(F32), jnp.where(lane == 1, e2.astype(F32), 0.0))
    out = jnp.where(lane == 2, w1, jnp.where(lane == 3, w2, out))
    o_ref[...] = jnp.where(lane == 4, rank1, jnp.where(lane == 5, rank2, out))


def route(logits):
    t = logits.shape[0]
    return pl.pallas_call(
        _route_kernel,
        out_shape=(jax.ShapeDtypeStruct((t, LANES), F32), jax.ShapeDtypeStruct((8, LANES), F32)),
        grid=(t // ROUTE_ROWS,),
        in_specs=[pl.BlockSpec((ROUTE_ROWS, LANES), lambda i: (i, 0)),
                  pl.BlockSpec((ROUTE_ROWS, ROUTE_ROWS), lambda i: (0, 0))],
        out_specs=(pl.BlockSpec((ROUTE_ROWS, LANES), lambda i: (i, 0)),
                   pl.BlockSpec((8, LANES), lambda i: (0, 0))),
        scratch_shapes=[pltpu.VMEM((1, LANES), F32)],
        compiler_params=_cparams(("arbitrary",)),
    )(logits, jnp.asarray(_STRICT_LOWER, BF16))


def _expert_kernel(be_ref, x_ref, wg_ref, wu_ref, wd_ref, *rest):
    o_ref, wg_s, wu_s, wd_s = rest[-4:]
    i = pl.program_id(0)

    @pl.when((i == 0) | (be_ref[i] != be_ref[jnp.maximum(i - 1, 0)]))
    def _():
        wg_s[...] = wg_ref[0, 0].astype(BF16)
        wu_s[...] = wu_ref[0, 0].astype(BF16)
        wd_s[...] = wd_ref[0, 0].astype(BF16)

    x = x_ref[...]
    gte = jnp.dot(x, wg_s[...], preferred_element_type=F32)
    up = jnp.dot(x, wu_s[...], preferred_element_type=F32)
    hid = (gte * _sigmoid(gte) * up).astype(BF16)
    o_ref[...] = jnp.dot(hid, wd_s[...], preferred_element_type=F32).astype(o_ref.dtype)


def expert_ffn(rows, block_expert, wg, wu, wd, layer, out_prev, block_off, n_total):
    n, d = rows.shape
    de = wg.shape[3]
    in_specs = [
        pl.BlockSpec((MOE_ROWS, d), lambda i, be: (i, 0)),
        pl.BlockSpec((1, 1, d, de), lambda i, be: (layer, be[i], 0, 0)),
        pl.BlockSpec((1, 1, d, de), lambda i, be: (layer, be[i], 0, 0)),
        pl.BlockSpec((1, 1, de, d), lambda i, be: (layer, be[i], 0, 0)),
    ]
    args = [block_expert, rows, wg, wu, wd]
    aliases = {}
    if out_prev is not None:
        in_specs.append(pl.BlockSpec(memory_space=pl.ANY))
        args.append(out_prev)
        aliases = {5: 0}
    return pl.pallas_call(
        _expert_kernel,
        out_shape=jax.ShapeDtypeStruct((n_total, d), BF16),
        grid_spec=pltpu.PrefetchScalarGridSpec(
            num_scalar_prefetch=1,
            grid=(n // MOE_ROWS,),
            in_specs=in_specs,
            out_specs=pl.BlockSpec((MOE_ROWS, d), lambda i, be: (i + block_off, 0)),
            scratch_shapes=[pltpu.VMEM((d, de), BF16), pltpu.VMEM((d, de), BF16), pltpu.VMEM((de, d), BF16)],
        ),
        input_output_aliases=aliases,
        compiler_params=_cparams(("arbitrary",)),
    )(*args)


MOE_PARTS = 4


def experts_overlapped(h_all, src_tok, block_expert, wg, wu, wd, layer):
    n_blocks = block_expert.shape[0]
    n_total = n_blocks * MOE_ROWS
    bounds = [n_blocks * k // MOE_PARTS for k in range(MOE_PARTS + 1)]
    out = None
    for b0, b1 in zip(bounds[:-1], bounds[1:]):
        rows = jnp.take(h_all, src_tok[b0 * MOE_ROWS:b1 * MOE_ROWS], axis=0, mode="clip")
        out = expert_ffn(rows, block_expert[b0:b1], wg, wu, wd, layer, out, b0, n_total)
    return out


def _combine_kernel(x_ref, r1_ref, r2_ref, info_ref, gate_ref, *rest, final):
    o_ref = rest[-1]
    info = info_ref[...]
    y = info[:, 2:3] * r1_ref[...].astype(F32) + info[:, 3:4] * r2_ref[...].astype(F32)
    x = x_ref[...] + gate_ref[0] * y
    if final:
        ms = jnp.mean(x * x, axis=-1, keepdims=True)
        x = x * lax.rsqrt(ms + EPS) * rest[0][...]
    o_ref[...] = x


def combine(x2d, r1, r2, info, row_off, gate, rows_per_batch, final_g=None):
    t, d = x2d.shape
    tm = min(ROUTE_ROWS, rows_per_batch)
    per = rows_per_batch // tm
    off = row_off // tm
    row = lambda i: (i, 0)
    shifted = lambda i: (i + off, 0)
    specs = [pl.BlockSpec((tm, d), row), pl.BlockSpec((tm, d), shifted), pl.BlockSpec((tm, d), shifted),
             pl.BlockSpec((tm, LANES), shifted), pl.BlockSpec((1, 1, d), lambda i: (i // per, 0, 0))]
    args = [x2d, r1, r2, info, gate]
    if final_g is not None:
        specs.append(pl.BlockSpec((1, d), lambda i: (0, 0)))
        args.append(final_g.reshape(1, d))
    return pl.pallas_call(
        functools.partial(_combine_kernel, final=final_g is not None),
        out_shape=jax.ShapeDtypeStruct((t, d), F32), grid=(t // tm,),
        in_specs=specs, out_specs=pl.BlockSpec((tm, d), row),
        compiler_params=_cparams(("parallel",)),
    )(*args)


def _dispatch(info, counts_row):
    n_tok = info.shape[0]
    expert = info[:, 0:TOP_K].astype(jnp.int32)
    rank = info[:, 4:4 + TOP_K].astype(jnp.int32)
    counts = counts_row[0, :N_EXPERTS].astype(jnp.int32)
    padded = (counts + MOE_ROWS - 1) // MOE_ROWS * MOE_ROWS
    pad_end = jnp.cumsum(padded)
    pad_start = pad_end - padded
    hot = expert[:, :, None] == jnp.arange(N_EXPERTS, dtype=jnp.int32)
    dest = jnp.sum(jnp.where(hot, pad_start, 0), axis=-1) + rank
    n_blocks = -(-(n_tok * TOP_K + N_EXPERTS * (MOE_ROWS - 1)) // MOE_ROWS)
    token = jnp.broadcast_to(jnp.arange(n_tok, dtype=jnp.int32)[:, None], dest.shape)
    src_tok = jnp.zeros((n_blocks * MOE_ROWS,), jnp.int32).at[dest.reshape(-1)].set(
        token.reshape(-1), unique_indices=True, mode="promise_in_bounds")
    block_start = jnp.arange(n_blocks, dtype=jnp.int32)[:, None] * MOE_ROWS
    block_expert = jnp.minimum(jnp.sum((pad_end[None, :] <= block_start).astype(jnp.int32), axis=1),
                               N_EXPERTS - 1)
    return src_tok, dest, block_expert


def _rope_tables(seq):
    n_rows = seq // GRID_W
    row = jnp.repeat(jnp.arange(n_rows), GRID_W).astype(F32)
    col = jnp.tile(jnp.arange(GRID_W), n_rows).astype(F32)
    axis_pairs = HEAD_DIM // 4
    inv = ROPE_THETA ** (-jnp.arange(axis_pairs, dtype=F32) / axis_pairs)
    ang = jnp.concatenate([row[:, None] * inv, col[:, None] * inv], axis=-1)
    cos, sin = jnp.cos(ang), jnp.sin(ang)
    cos128 = jnp.concatenate([cos, cos, cos, cos], axis=-1)
    sins128 = jnp.concatenate([-sin, sin, -sin, sin], axis=-1)
    return cos128, sins128


def kernel(x, c, ctx, c_ctx, w_mod, b_mod, norm1_g, norm2_g, w_in, hgrn_lb_logits, hgrn_out_norm_g,
           attn_q_norm_g, attn_k_norm_g, swa_sink, w_branch_a, w_branch_b, w_branch_c, w_out,
           w_group, w_router, w_exp_gate, w_exp_up, w_exp_down, final_norm_g):
    bsz, seq, d = x.shape
    n_ctx = ctx.shape[1]
    depth = w_mod.shape[0]
    cos, sins = _rope_tables(seq)
    lb_p = jax.nn.softmax(hgrn_lb_logits.astype(F32), axis=0)
    lower_bounds = jnp.cumsum(lb_p, axis=0) - lb_p[0]
    no_sink = jnp.zeros((N_HEADS,), F32)
    ones128 = jnp.ones((1, LANES), F32)

    xl = x.reshape(bsz * seq, d)
    xc = ctx.reshape(bsz * n_ctx, d)
    for layer in range(depth):
        ctx_out = layer < depth - 1
        mod_l = jax.nn.silu(c) @ w_mod[layer] + b_mod[layer]
        mod_c = jnp.broadcast_to(jax.nn.silu(c_ctx) @ w_mod[layer] + b_mod[layer], (bsz, 6 * d))
        ml = [m.reshape(bsz, 1, d) for m in jnp.split(mod_l, 6, axis=-1)]
        mc = [m.reshape(bsz, 1, d) for m in jnp.split(mod_c, 6, axis=-1)]

        w_in_b = w_in[layer].astype(BF16)
        p_l = inproj(xl, seq, norm1_g[layer], ml[1], ml[0], w_in_b, 1024, 3584)
        p_c = inproj(xc, n_ctx, norm1_g[layer], mc[1], mc[0], w_in_b, 256, 1024)
        p_l3 = p_l.reshape(bsz, seq, D_IN)
        p_c3 = p_c.reshape(bsz, n_ctx, D_IN)

        s0 = jnp.zeros((bsz, 2, HG_HEADS, HG_KDIM, HG_KDIM), F32)
        o_c, s_c = hgrn_scan(p_c3, lower_bounds[layer], s0, 256)
        o_l, _ = hgrn_scan(p_l3, lower_bounds[layer], s_c, 512)

        gq = jnp.tile(attn_q_norm_g[layer], 2).reshape(1, LANES)
        gk = jnp.tile(attn_k_norm_g[layer], 2).reshape(1, LANES)
        kt_l, v_l, kn_l = prep_kv(p_l3, COL["ak"], COL["av"], gk, cos, sins, True, True, 512)
        kt_c, v_c, kn_c = prep_kv(p_c3, COL["ak"], COL["av"], gk, cos, sins, True, False, 256)
        kmax = _key_norm_max([kn_l, kn_c])
        q_l, u_l = prep_q(p_l3, COL["aq"], gq, cos, sins, kmax, True, True, 512)
        kt_all = jnp.concatenate([kt_c, kt_l], axis=3)
        v_all = jnp.concatenate([v_c, v_l], axis=2)
        b_l = attention(q_l, kt_all, v_all, jnp.max(u_l), no_sink, 512, 1280)

        skt_l, sv_l, skn_l = prep_kv(p_l3, COL["sk"], COL["sv"], ones128, cos, sins, False, True, 512)
        skt_c, sv_c, skn_c = prep_kv(p_c3, COL["sk"], COL["sv"], ones128, cos, sins, False, False, 256)
        skmax = _key_norm_max([skn_l, skn_c])
        sq_l, su_l = prep_q(p_l3, COL["sq"], ones128, cos, sins, skmax, False, True, 512)
        zk = jnp.zeros((bsz, N_KV, LANES, WINDOW), BF16)
        zv = jnp.zeros((bsz, N_KV, WINDOW, LANES), BF16)
        skt_all = jnp.concatenate([skt_c, zk, skt_l, zk], axis=3)
        sv_all = jnp.concatenate([sv_c, zv, sv_l, zv], axis=2)
        c_l = attention(sq_l, skt_all, sv_all, jnp.max(su_l), swa_sink[layer], 256, 256,
                        use_sink=True, band=n_ctx)

        wa = w_branch_a[layer].astype(BF16)
        wb = w_branch_b[layer].astype(BF16)
        wc = w_branch_c[layer].astype(BF16)
        wo = w_out[layer].astype(BF16)
        w_r = jnp.concatenate(
            [w_group[layer], w_router[layer], jnp.zeros((d, LANES - N_GROUPS - N_EXPERTS), F32)], axis=1)
        w_r_hi = w_r.astype(BF16)
        w_r_lo = (w_r - w_r_hi.astype(F32)).astype(BF16)
        w_route = jnp.concatenate([w_r_hi, w_r_hi, w_r_lo], axis=0)
        xl, h_l, lg_l = merge(o_l.reshape(2, bsz * seq, HG_W), p_l, b_l.reshape(-1, ATT_W),
                              c_l.reshape(-1, ATT_W), xl, seq, wa, wb, wc, wo, hgrn_out_norm_g[layer],
                              ml[2], norm2_g[layer], ml[4], ml[3], w_route, 256)
        if ctx_out:
            q_c, u_c = prep_q(p_c3, COL["aq"], gq, cos, sins, kmax, True, False, 256)
            b_c = attention(q_c, kt_c, v_c, jnp.max(u_c), no_sink, 256, 256)
            sq_c, su_c = prep_q(p_c3, COL["sq"], ones128, cos, sins, skmax, False, False, 256)
            c_c = attention(sq_c, skt_c, sv_c, jnp.max(su_c), swa_sink[layer], 256, 256, use_sink=True)
            xc, h_c, lg_c = merge(o_c.reshape(2, bsz * n_ctx, HG_W), p_c, b_c.reshape(-1, ATT_W),
                                  c_c.reshape(-1, ATT_W), xc, n_ctx, wa, wb, wc, wo, hgrn_out_norm_g[layer],
                                  mc[2], norm2_g[layer], mc[4], mc[3], w_route, 256)
            h_all = jnp.concatenate([h_c, h_l], axis=0)
            lg_all = jnp.concatenate([lg_c, lg_l], axis=0)
        else:
            h_all, lg_all = h_l, lg_l

        info, counts_row = route(lg_all)
        src_tok, dest, block_expert = _dispatch(info, counts_row)
        out_rows = experts_overlapped(h_all, src_tok, block_expert, w_exp_gate, w_exp_up, w_exp_down, layer)
        r1 = jnp.take(out_rows, dest[:, 0], axis=0, mode="clip")
        r2 = jnp.take(out_rows, dest[:, 1], axis=0, mode="clip")
        n_c = bsz * n_ctx if ctx_out else 0
        if ctx_out:
            xc = combine(xc, r1, r2, info, 0, mc[5], n_ctx)
        xl = combine(xl, r1, r2, info, n_c, ml[5], seq, final_g=final_norm_g if layer == depth - 1 else None)
    return xl.reshape(bsz, seq, d)
```

```python
import functools
import math

import numpy as np
import jax
import jax.numpy as jnp
from jax import lax
from jax.experimental import pallas as pl
from jax.experimental.pallas import tpu as pltpu

F32 = jnp.float32
BF16 = jnp.bfloat16

EPS = 1e-6
MASK_VALUE = -1e30
TINY = 1e-30
GRID_W = 64
ROPE_THETA = 10000.0

HG_HEADS = 4
HG_KDIM = 128
HG_W = HG_HEADS * HG_KDIM
HEAD_DIM = 64
N_HEADS = 8
N_KV = 2
GROUP = N_HEADS // N_KV
ATT_W = N_HEADS * HEAD_DIM
KV_W = N_KV * HEAD_DIM
WINDOW = 128
ATTN_SCALE = HEAD_DIM ** -0.5
N_GROUPS = 4
EXP_PER_GROUP = 8
N_EXPERTS = N_GROUPS * EXP_PER_GROUP
TOP_K = 2

LANES = 128
HG_CHUNK = 128
HG_LEVELS = int(math.log2(HG_CHUNK))
HG_TOTAL_ROWS = 16
MOE_ROWS = 256
ROUTE_ROWS = 512
VMEM_LIMIT = 56 * 1024 * 1024

LOG2E = 1.4426950408889634
SHIFT_HEADROOM = 57.0
SAFE_BOUND = 90.0


def _cparams(sem):
    return pltpu.CompilerParams(dimension_semantics=sem, vmem_limit_bytes=VMEM_LIMIT)


_SIZES = (HG_W, HG_W, HG_W, HG_W, HG_W, ATT_W, KV_W, KV_W, ATT_W, KV_W, KV_W, 3 * 1024)
_NAMES = ("hq", "ff", "fb", "hi", "hg", "aq", "ak", "av", "sq", "sk", "sv", "gates")
COL = {n: int(sum(_SIZES[:i])) for i, n in enumerate(_NAMES)}
D_IN = int(sum(_SIZES))


def _inproj_kernel(x_ref, g_ref, sc_ref, sh_ref, w_ref, o_ref, h_scr):
    @pl.when(pl.program_id(1) == 0)
    def _():
        x = x_ref[...]
        ms = jnp.mean(x * x, axis=-1, keepdims=True)
        y = x * lax.rsqrt(ms + EPS) * g_ref[...]
        h_scr[...] = (y * (1.0 + sc_ref[0]) + sh_ref[0]).astype(BF16)

    o_ref[...] = jnp.dot(h_scr[...], w_ref[...], preferred_element_type=F32).astype(o_ref.dtype)


def inproj(x2d, rows_per_batch, g, scale, shift, w_bf16, tm, tn):
    t, d = x2d.shape
    n = w_bf16.shape[1]
    tm = min(tm, rows_per_batch)
    per = rows_per_batch // tm
    return pl.pallas_call(
        _inproj_kernel,
        out_shape=jax.ShapeDtypeStruct((t, n), BF16),
        grid=(t // tm, n // tn),
        in_specs=[
            pl.BlockSpec((tm, d), lambda i, j: (i, 0)),
            pl.BlockSpec((1, d), lambda i, j: (0, 0)),
            pl.BlockSpec((1, 1, d), lambda i, j: (i // per, 0, 0)),
            pl.BlockSpec((1, 1, d), lambda i, j: (i // per, 0, 0)),
            pl.BlockSpec((d, tn), lambda i, j: (0, j)),
        ],
        out_specs=pl.BlockSpec((tm, tn), lambda i, j: (i, j)),
        scratch_shapes=[pltpu.VMEM((tm, d), BF16)],
        compiler_params=_cparams(("parallel", "arbitrary")),
    )(x2d, g.reshape(1, d), scale, shift, w_bf16)


def _hgrn_exponent_matrices():
    c, nl = HG_CHUNK, HG_LEVELS
    rows = (nl + 2) * c + HG_TOTAL_ROWS
    out = np.zeros((2, rows, c), np.float32)
    for d in range(2):
        pos = np.arange(c) if d == 0 else c - 1 - np.arange(c)
        for l in range(nl):
            m = 1 << l
            for t in range(c):
                p = pos[t]
                mid = (p // (2 * m)) * 2 * m + m
                if p >= mid:
                    sel = (pos >= mid) & (pos <= p)
                else:
                    sel = (pos > p) & (pos <= mid - 1)
                out[d, l * c + t, sel] = 1.0
        for t in range(c):
            out[d, nl * c + t, pos <= pos[t]] = 1.0
            out[d, (nl + 1) * c + t, pos > pos[t]] = 1.0
        out[d, (nl + 2) * c:, :] = 1.0
    return np.concatenate([out, out], axis=2)


_HG_EXP_MATS = _hgrn_exponent_matrices()


def _sigmoid(x):
    return 1.0 / (1.0 + jnp.exp(-x))


def _hgrn_kernel(q_ref, f_ref, i_ref, lb_ref, mat_ref, s0_ref, o_ref, st_ref, *, n_chunks):
    c, nl = HG_CHUNK, HG_LEVELS
    d = pl.program_id(1)

    @pl.when(pl.program_id(2) == 0)
    def _():
        st_ref[...] = s0_ref[...]

    row = lax.broadcasted_iota(jnp.int32, (c, HG_W), 0)
    pos = row + d * (c - 1 - 2 * row)
    r_i = lax.broadcasted_iota(jnp.int32, (c, c), 0)
    c_i = lax.broadcasted_iota(jnp.int32, (c, c), 1)
    xor_rc = r_i ^ c_i
    top_bit = sum((xor_rc >= (1 << j)).astype(jnp.int32) for j in range(1, nl))
    pair_level = jnp.where((r_i - c_i) * (1 - 2 * d) > 0, top_bit, -1)

    def chunk(ci, carry):
        cc = ci + d * (n_chunks - 1 - 2 * ci)
        r0 = pl.multiple_of(cc * c, c)
        qraw = q_ref[0, pl.ds(r0, c), :].astype(F32)
        fz = f_ref[0, pl.ds(r0, c), :].astype(F32)
        vb = i_ref[0, pl.ds(r0, c), :]
        v = vb.astype(F32)
        lb = lb_ref[...]
        sig = _sigmoid(fz)
        logf = jnp.log(jnp.maximum(lb + (1.0 - lb) * sig, TINY))
        key = (1.0 - lb) * (1.0 - sig)
        qh = qraw * _sigmoid(qraw) * (HG_KDIM ** -0.5)
        logf2 = logf * LOG2E
        hi = logf2.astype(BF16)
        lo = (logf2 - hi.astype(F32)).astype(BF16)
        hilo = jnp.concatenate([hi, lo], axis=0)

        dec_all = jnp.exp2(jnp.dot(mat_ref[0], hilo, preferred_element_type=F32))

        def decay(r_lo, r_hi):
            return dec_all[r_lo:r_hi]

        a = [jnp.zeros((c, c), F32)] * HG_HEADS
        for l in range(nl):
            x = (jnp.where(((pos >> l) & 1) == 1, qh, key) * decay(l * c, (l + 1) * c)).astype(BF16)
            for h in range(HG_HEADS):
                xh = x[:, h * LANES:(h + 1) * LANES]
                prod = lax.dot_general(xh, xh, (((1,), (1,)), ((), ())), preferred_element_type=F32)
                a[h] = jnp.where(pair_level == l, prod, a[h])
        q_in = (qh * decay(nl * c, (nl + 1) * c)).astype(BF16)
        k_out = (key * decay((nl + 1) * c, (nl + 2) * c)).astype(BF16)
        total = decay((nl + 2) * c, (nl + 2) * c + HG_TOTAL_ROWS)[0:1]
        qk = qh * key
        for h in range(HG_HEADS):
            lanes = slice(h * LANES, (h + 1) * LANES)
            st = st_ref[0, 0, h]
            o = (jnp.dot(a[h].astype(BF16), vb[:, lanes], preferred_element_type=F32)
                 + jnp.sum(qk[:, lanes], axis=1, keepdims=True) * v[:, lanes]
                 + lax.dot_general(q_in[:, lanes], st.astype(BF16), (((1,), (1,)), ((), ())),
                                   preferred_element_type=F32))
            o_ref[0, 0, pl.ds(r0, c), lanes] = o.astype(o_ref.dtype)
            st_ref[0, 0, h] = total[:, lanes] * st + lax.dot_general(
                vb[:, lanes], k_out[:, lanes], (((0,), (0,)), ((), ())), preferred_element_type=F32)
        return carry

    lax.fori_loop(0, n_chunks, chunk, 0, unroll=2 if n_chunks % 2 == 0 else 1)


def hgrn_scan(p3, lb, s0, tb):
    b, l, _ = p3.shape
    tb = min(tb, l)
    nb = l // tb
    blk = lambda d, n: n + d * (nb - 1 - 2 * n)
    wblk = HG_W
    kern = functools.partial(_hgrn_kernel, n_chunks=tb // HG_CHUNK)
    mats = jnp.asarray(_HG_EXP_MATS, BF16)
    return pl.pallas_call(
        kern,
        out_shape=(jax.ShapeDtypeStruct((2, b, l, HG_W), BF16),
                   jax.ShapeDtypeStruct((b, 2, HG_HEADS, HG_KDIM, HG_KDIM), F32)),
        grid=(b, 2, nb),
        in_specs=[
            pl.BlockSpec((1, tb, wblk), lambda bi, d, n: (bi, blk(d, n), COL["hq"] // wblk)),
            pl.BlockSpec((1, tb, wblk), lambda bi, d, n: (bi, blk(d, n), COL["ff"] // wblk + d)),
            pl.BlockSpec((1, tb, wblk), lambda bi, d, n: (bi, blk(d, n), COL["hi"] // wblk)),
            pl.BlockSpec((1, HG_W), lambda bi, d, n: (0, 0)),
            pl.BlockSpec((1,) + _HG_EXP_MATS.shape[1:], lambda bi, d, n: (d, 0, 0)),
            pl.BlockSpec((1, 1, HG_HEADS, HG_KDIM, HG_KDIM), lambda bi, d, n: (bi, d, 0, 0, 0)),
        ],
        out_specs=(
            pl.BlockSpec((1, 1, tb, HG_W), lambda bi, d, n: (d, bi, blk(d, n), 0)),
            pl.BlockSpec((1, 1, HG_HEADS, HG_KDIM, HG_KDIM), lambda bi, d, n: (bi, d, 0, 0, 0)),
        ),
        compiler_params=_cparams(("parallel", "arbitrary", "arbitrary")),
    )(p3, p3, p3, lb.reshape(1, HG_W), mats, s0)


_GROUP_ONES = np.kron(np.eye(LANES // HEAD_DIM, dtype=np.float32), np.ones((HEAD_DIM, HEAD_DIM), np.float32))
_GROUP_ONES2 = np.concatenate([_GROUP_ONES, _GROUP_ONES], axis=0)


def _group_ssq(x, ones2):
    sq = x * x
    hi = sq.astype(BF16)
    lo = (sq - hi.astype(F32)).astype(BF16)
    return jnp.dot(jnp.concatenate([hi, lo], axis=1), ones2, preferred_element_type=F32)


def _prep_tile(x, g, cos, sins, ones2, norm, rope):
    if norm:
        x = x * lax.rsqrt(_group_ssq(x, ones2) * (1.0 / HEAD_DIM) + EPS) * g
    if rope:
        lane = lax.broadcasted_iota(jnp.int32, x.shape, 1)
        first = (lane % HEAD_DIM) < (HEAD_DIM // 2)
        other = jnp.where(first, pltpu.roll(x, LANES - HEAD_DIM // 2, 1), pltpu.roll(x, HEAD_DIM // 2, 1))
        x = x * cos + other * sins
    return x


def _prep_kv_kernel(k_ref, v_ref, g_ref, cos_ref, sin_ref, ones_ref, kt_ref, vo_ref, kn_ref, *, norm, rope):
    k = _prep_tile(k_ref[0].astype(F32), g_ref[...], cos_ref[...], sin_ref[...], ones_ref[...], norm, rope)
    kb = k.astype(BF16).astype(F32)
    ssq = _group_ssq(kb, ones_ref[...])
    kn_ref[0, 0] = jnp.broadcast_to(jnp.max(ssq, axis=0, keepdims=True), (8, LANES))
    kt = kb.T
    row = lax.broadcasted_iota(jnp.int32, kt.shape, 0)
    v = v_ref[0]
    low = lax.broadcasted_iota(jnp.int32, v.shape, 1) < HEAD_DIM
    one = jnp.ones((), v.dtype)
    for g in range(N_KV):
        ktg = kt if g == 0 else jnp.concatenate([kt[HEAD_DIM:], kt[:HEAD_DIM]], axis=0)
        kt_ref[0, g] = jnp.where(row < HEAD_DIM, ktg, jnp.where(row == HEAD_DIM, 1.0, 0.0)).astype(BF16)
        vg = v if g == 0 else jnp.concatenate([v[:, HEAD_DIM:], v[:, :HEAD_DIM]], axis=1)
        vo_ref[0, g] = jnp.where(low, vg, one).astype(BF16)


def _prep_q_kernel(kmax_ref, x0_ref, x1_ref, x2_ref, x3_ref, g_ref, cos_ref, sin_ref, ones_ref, o_ref, u_ref,
                   *, norm, rope):
    b = pl.program_id(0)
    lane = lax.broadcasted_iota(jnp.int32, (x0_ref.shape[1], LANES), 1)
    umax = None
    for tile, x_ref in enumerate((x0_ref, x1_ref, x2_ref, x3_ref)):
        kv = (2 * tile) // GROUP
        x = x_ref[0].astype(F32)
        y = _prep_tile(x, g_ref[...], cos_ref[...], sin_ref[...], ones_ref[...], norm, rope) * (ATTN_SCALE * LOG2E)
        yb = y.astype(BF16).astype(F32)
        u2 = jnp.sqrt(_group_ssq(yb, ones_ref[...])) * kmax_ref[b * N_KV + kv]
        tmax = jnp.max(u2, axis=0, keepdims=True)
        umax = tmax if umax is None else jnp.maximum(umax, tmax)
        neg_shift = SHIFT_HEADROOM - u2
        y_sw = pltpu.roll(yb, HEAD_DIM, 1)
        ns_sw = pltpu.roll(neg_shift, HEAD_DIM, 1)
        for half in range(2):
            h = 2 * tile + half
            data = yb if half == 0 else y_sw
            ns = ns_sw if half == 0 else neg_shift
            o_ref[0, :, h * LANES:(h + 1) * LANES] = jnp.where(
                lane < HEAD_DIM, data, jnp.where(lane == HEAD_DIM, ns, 0.0)).astype(BF16)
    u_ref[0, 0] = jnp.broadcast_to(umax, (8, LANES))


def prep_kv(p3, col_k, col_v, g128, cos, sins, norm, rope, tq):
    b, l, _ = p3.shape
    tq = min(tq, l)
    ones2 = jnp.asarray(_GROUP_ONES2, BF16)
    return pl.pallas_call(
        functools.partial(_prep_kv_kernel, norm=norm, rope=rope),
        out_shape=(jax.ShapeDtypeStruct((b, N_KV, LANES, l), BF16),
                   jax.ShapeDtypeStruct((b, N_KV, l, LANES), BF16),
                   jax.ShapeDtypeStruct((b, l // tq, 8, LANES), F32)),
        grid=(b, l // tq),
        in_specs=[
            pl.BlockSpec((1, tq, KV_W), lambda bi, i: (bi, i, col_k // KV_W)),
            pl.BlockSpec((1, tq, KV_W), lambda bi, i: (bi, i, col_v // KV_W)),
            pl.BlockSpec((1, LANES), lambda bi, i: (0, 0)),
            pl.BlockSpec((tq, LANES), lambda bi, i: (i, 0)),
            pl.BlockSpec((tq, LANES), lambda bi, i: (i, 0)),
            pl.BlockSpec((2 * LANES, LANES), lambda bi, i: (0, 0)),
        ],
        out_specs=(pl.BlockSpec((1, N_KV, LANES, tq), lambda bi, i: (bi, 0, 0, i)),
                   pl.BlockSpec((1, N_KV, tq, LANES), lambda bi, i: (bi, 0, i, 0)),
                   pl.BlockSpec((1, 1, 8, LANES), lambda bi, i: (bi, i, 0, 0))),
        compiler_params=_cparams(("parallel", "parallel")),
    )(p3, p3, g128, cos, sins, ones2)


def prep_q(p3, col, g128, cos, sins, kmax, norm, rope, tq):
    b, l, _ = p3.shape
    tq = min(tq, l)
    ones2 = jnp.asarray(_GROUP_ONES2, BF16)
    return pl.pallas_call(
        functools.partial(_prep_q_kernel, norm=norm, rope=rope),
        out_shape=(jax.ShapeDtypeStruct((b, l, N_HEADS * LANES), BF16),
                   jax.ShapeDtypeStruct((b, l // tq, 8, LANES), F32)),
        grid_spec=pltpu.PrefetchScalarGridSpec(
            num_scalar_prefetch=1,
            grid=(b, l // tq),
            in_specs=[
                pl.BlockSpec((1, tq, LANES), lambda bi, i, s, t=t: (bi, i, col // LANES + t))
                for t in range(ATT_W // LANES)
            ] + [
                pl.BlockSpec((1, LANES), lambda bi, i, s: (0, 0)),
                pl.BlockSpec((tq, LANES), lambda bi, i, s: (i, 0)),
                pl.BlockSpec((tq, LANES), lambda bi, i, s: (i, 0)),
                pl.BlockSpec((2 * LANES, LANES), lambda bi, i, s: (0, 0)),
            ],
            out_specs=(pl.BlockSpec((1, tq, N_HEADS * LANES), lambda bi, i, s: (bi, i, 0)),
                       pl.BlockSpec((1, 1, 8, LANES), lambda bi, i, s: (bi, i, 0, 0))),
        ),
        compiler_params=_cparams(("parallel", "parallel")),
    )(kmax, p3, p3, p3, p3, g128, cos, sins, ones2)


def _key_norm_max(kn_list):
    kn = functools.reduce(jnp.maximum, [jnp.max(k, axis=(1, 2)) for k in kn_list])
    return jnp.sqrt(kn[:, ::HEAD_DIM]).reshape(-1)


def _attn_kernel(sink_ref, q_ref, kt_ref, v_ref, o_ref, *, tk, online, use_sink, band, seq):
    tq = q_ref.shape[1]
    rows = GROUP * tq
    lane = lax.broadcasted_iota(jnp.int32, (tq, LANES), 1)
    low = lane < HEAD_DIM
    if band is not None:
        width = tq + 2 * WINDOW
        start = pl.multiple_of(pl.program_id(1) * tq, tq)
        r = lax.broadcasted_iota(jnp.int32, (rows, width), 0) % tq
        kp = lax.broadcasted_iota(jnp.int32, (rows, width), 1)
        key_pos = start + kp - WINDOW
        mask = (jnp.abs(kp - WINDOW - r) <= WINDOW) & (key_pos >= 0) & (key_pos < seq)
    for g in range(N_KV):
        tiles = [q_ref[0, :, (g * GROUP + hh) * LANES:(g * GROUP + hh + 1) * LANES] for hh in range(GROUP)]
        qg = jnp.concatenate(tiles, axis=0)

        def scores(kt, qg=qg):
            return jnp.dot(qg, kt, preferred_element_type=F32)

        def update(carry, s, v, msk=None):
            if msk is not None:
                s = jnp.where(msk, s, MASK_VALUE)
            if online:
                m, acc = carry
                m_new = jnp.maximum(m, jnp.max(s, axis=1, keepdims=True))
                p = jnp.exp2(s - m_new).astype(BF16)
                return m_new, jnp.exp2(m - m_new) * acc + jnp.dot(p, v, preferred_element_type=F32)
            return carry + jnp.dot(jnp.exp2(s).astype(BF16), v, preferred_element_type=F32)

        acc0 = jnp.zeros((rows, LANES), F32)
        carry = (jnp.full((rows, 1), MASK_VALUE, F32), acc0) if online else acc0
        if band is None:
            def chunk(c, k0, g=g, scores=scores, update=update):
                return update(c, scores(kt_ref[0, g, :, pl.ds(k0, tk)]), v_ref[0, g, pl.ds(k0, tk), :])

            nk = kt_ref.shape[3] // tk
            for j in range(nk % 2):
                carry = chunk(carry, j * tk)

            def body(j, c, chunk=chunk, first=(nk % 2) * tk):
                k0 = pl.multiple_of(first + j * (2 * tk), tk)
                return chunk(chunk(c, k0), pl.multiple_of(k0 + tk, tk))

            carry = lax.fori_loop(0, nk // 2, body, carry)
        else:
            carry = update(carry, scores(kt_ref[0, g, :, 0:band]), v_ref[0, g, 0:band, :])
            b0 = pl.multiple_of(band + start, LANES)
            carry = update(carry, scores(kt_ref[0, g, :, pl.ds(b0, width)]), v_ref[0, g, pl.ds(b0, width), :], mask)
        acc = carry[1] if online else carry
        if use_sink:
            e = jnp.concatenate([t[:, HEAD_DIM:HEAD_DIM + 1].astype(F32) + sink_ref[g * GROUP + hh] * LOG2E
                                 for hh, t in enumerate(tiles)], axis=0)
            if online:
                e = e - carry[0]
            lane_r = lax.broadcasted_iota(jnp.int32, (rows, LANES), 1)
            acc = acc + jnp.where(lane_r >= HEAD_DIM, jnp.exp2(e), 0.0)
        o = acc * pltpu.roll(1.0 / acc, HEAD_DIM, 1)
        for pair in range(GROUP // 2):
            a = o[(2 * pair) * tq:(2 * pair + 1) * tq]
            b = o[(2 * pair + 1) * tq:(2 * pair + 2) * tq]
            t0 = (g * GROUP // 2 + pair) * LANES
            o_ref[0, :, t0:t0 + LANES] = jnp.where(low, a, pltpu.roll(b, HEAD_DIM, 1)).astype(o_ref.dtype)


def _attention(q, kt, v, sink, tq, tk, online, use_sink, band):
    b, l, _ = q.shape
    sk = kt.shape[3]
    tq = min(tq, l)
    tk = min(tk, sk)
    kern = functools.partial(_attn_kernel, tk=tk, online=online, use_sink=use_sink, band=band, seq=l)
    return pl.pallas_call(
        kern,
        out_shape=jax.ShapeDtypeStruct((b, l, ATT_W), BF16),
        grid_spec=pltpu.PrefetchScalarGridSpec(
            num_scalar_prefetch=1,
            grid=(b, l // tq),
            in_specs=[
                pl.BlockSpec((1, tq, N_HEADS * LANES), lambda bi, i, s: (bi, i, 0)),
                pl.BlockSpec((1, N_KV, LANES, sk), lambda bi, i, s: (bi, 0, 0, 0), pipeline_mode=pl.Buffered(1)),
                pl.BlockSpec((1, N_KV, sk, LANES), lambda bi, i, s: (bi, 0, 0, 0), pipeline_mode=pl.Buffered(1)),
            ],
            out_specs=pl.BlockSpec((1, tq, ATT_W), lambda bi, i, s: (bi, i, 0)),
        ),
        compiler_params=_cparams(("parallel", "arbitrary")),
    )(sink, q, kt, v)


def attention(q, kt, v, ubound, sink, tq, tk, use_sink=False, band=None):
    return lax.cond(ubound <= SAFE_BOUND,
                    lambda: _attention(q, kt, v, sink, tq, tk, False, use_sink, band),
                    lambda: _attention(q, kt, v, sink, tq, tk, True, use_sink, band))


def _merge_kernel(of_ref, ob_ref, hg_ref, b_ref, c_ref, ga_ref, gb_ref, gc_ref, x_ref,
                  wa_ref, wb_ref, wc_ref, wo_ref, hgn_ref, gate1_ref, n2_ref, sc2_ref, sh2_ref, wr_ref,
                  xo_ref, h2_ref, lg_ref):
    o = of_ref[0].astype(F32) + ob_ref[0].astype(F32)
    tiles = []
    for h in range(HG_HEADS):
        t = o[:, h * LANES:(h + 1) * LANES]
        ms = jnp.mean(t * t, axis=-1, keepdims=True)
        tiles.append(t * lax.rsqrt(ms + EPS) * hgn_ref[...])
    hg = hg_ref[...].astype(F32)
    a = (jnp.concatenate(tiles, axis=1) * (hg * _sigmoid(hg))).astype(BF16)
    merged = (_sigmoid(ga_ref[...].astype(F32)) * jnp.dot(a, wa_ref[...], preferred_element_type=F32)
              + _sigmoid(gb_ref[...].astype(F32)) * jnp.dot(b_ref[...], wb_ref[...], preferred_element_type=F32)
              + _sigmoid(gc_ref[...].astype(F32)) * jnp.dot(c_ref[...], wc_ref[...], preferred_element_type=F32))
    y = jnp.dot(merged.astype(BF16), wo_ref[...], preferred_element_type=F32)
    x = x_ref[...] + gate1_ref[0] * y
    xo_ref[...] = x
    ms = jnp.mean(x * x, axis=-1, keepdims=True)
    h2 = (x * lax.rsqrt(ms + EPS) * n2_ref[...]) * (1.0 + sc2_ref[0]) + sh2_ref[0]
    hi = h2.astype(BF16)
    lo = (h2 - hi.astype(F32)).astype(BF16)
    h2_ref[...] = h2
    lg_ref[...] = jnp.dot(jnp.concatenate([hi, lo, hi], axis=1), wr_ref[...], preferred_element_type=F32)


def merge(o_fb, p2d, b2d, c2d, x2d, rows_per_batch, wa, wb, wc, wo, hgn, gate1, n2, sc2, sh2, w_route, tm):
    t, d = x2d.shape
    tm = min(tm, rows_per_batch)
    per = rows_per_batch // tm
    row = lambda i: (i, 0)
    const = lambda i: (0, 0)
    bat = lambda i: (i // per, 0, 0)
    gcol = COL["gates"] // d
    return pl.pallas_call(
        _merge_kernel,
        out_shape=(jax.ShapeDtypeStruct((t, d), F32), jax.ShapeDtypeStruct((t, d), F32),
                   jax.ShapeDtypeStruct((t, LANES), F32)),
        grid=(t // tm,),
        in_specs=[
            pl.BlockSpec((1, tm, HG_W), lambda i: (0, i, 0)),
            pl.BlockSpec((1, tm, HG_W), lambda i: (1, i, 0)),
            pl.BlockSpec((tm, HG_W), lambda i: (i, COL["hg"] // HG_W)),
            pl.BlockSpec((tm, ATT_W), row),
            pl.BlockSpec((tm, ATT_W), row),
            pl.BlockSpec((tm, d), lambda i: (i, gcol)),
            pl.BlockSpec((tm, d), lambda i: (i, gcol + 1)),
            pl.BlockSpec((tm, d), lambda i: (i, gcol + 2)),
            pl.BlockSpec((tm, d), row),
            pl.BlockSpec((HG_W, d), const),
            pl.BlockSpec((ATT_W, d), const),
            pl.BlockSpec((ATT_W, d), const),
            pl.BlockSpec((d, d), const),
            pl.BlockSpec((1, LANES), const),
            pl.BlockSpec((1, 1, d), bat),
            pl.BlockSpec((1, d), const),
            pl.BlockSpec((1, 1, d), bat),
            pl.BlockSpec((1, 1, d), bat),
            pl.BlockSpec((3 * d, LANES), const),
        ],
        out_specs=(pl.BlockSpec((tm, d), row), pl.BlockSpec((tm, d), row), pl.BlockSpec((tm, LANES), row)),
        compiler_params=_cparams(("parallel",)),
    )(o_fb, o_fb, p2d, b2d, c2d, p2d, p2d, p2d, x2d, wa, wb, wc, wo, hgn.reshape(1, LANES),
      gate1, n2.reshape(1, d), sc2, sh2, w_route)


_STRICT_LOWER = np.tril(np.ones((ROUTE_ROWS, ROUTE_ROWS), np.float32), -1)
_NEG_BIG = -3.0e38


def _lane_argmax(x, lane):
    top = jnp.max(x, axis=1, keepdims=True)
    idx = jnp.min(jnp.where(x == top, lane, LANES), axis=1, keepdims=True)
    return top, idx


def _route_kernel(lg_ref, tri_ref, o_ref, cnt_ref, run_scr):
    @pl.when(pl.program_id(0) == 0)
    def _():
        run_scr[...] = jnp.zeros_like(run_scr)

    lg = lg_ref[...]
    lane = lax.broadcasted_iota(jnp.int32, lg.shape, 1)
    is_grp = lane < N_GROUPS
    gtop, gidx = _lane_argmax(jnp.where(is_grp, lg, _NEG_BIG), lane)
    grp_w = 1.0 / jnp.sum(jnp.where(is_grp, jnp.exp(lg - gtop), 0.0), axis=1, keepdims=True)
    lo = N_GROUPS + EXP_PER_GROUP * gidx
    x1 = jnp.where((lane >= lo) & (lane < lo + EXP_PER_GROUP), lg, _NEG_BIG)
    t1, i1 = _lane_argmax(x1, lane)
    t2, i2 = _lane_argmax(jnp.where(lane == i1, _NEG_BIG, x1), lane)
    r = jnp.exp(t2 - t1)
    w1 = grp_w / (1.0 + r)
    w2 = w1 * r
    e1 = i1 - N_GROUPS
    e2 = i2 - N_GROUPS
    hot1 = lane == e1
    hot2 = lane == e2
    hot = jnp.where(hot1 | hot2, 1.0, 0.0)
    before = run_scr[...] + jnp.dot(tri_ref[...], hot.astype(BF16), preferred_element_type=F32)
    rank1 = jnp.sum(jnp.where(hot1, before, 0.0), axis=1, keepdims=True)
    rank2 = jnp.sum(jnp.where(hot2, before, 0.0), axis=1, keepdims=True)
    run_scr[...] = run_scr[...] + jnp.sum(hot, axis=0, keepdims=True)
    cnt_ref[...] = jnp.broadcast_to(run_scr[...], cnt_ref.shape)
    out = jnp.where(lane == 0, e1.astype(F32), jnp.where(lane == 1, e2.astype(F32), 0.0))
    out = jnp.where(lane == 2, w1, jnp.where(lane == 3, w2, out))
    o_ref[...] = jnp.where(lane == 4, rank1, jnp.where(lane == 5, rank2, out))


def route(logits):
    t = logits.shape[0]
    return pl.pallas_call(
        _route_kernel,
        out_shape=(jax.ShapeDtypeStruct((t, LANES), F32), jax.ShapeDtypeStruct((8, LANES), F32)),
        grid=(t // ROUTE_ROWS,),
        in_specs=[pl.BlockSpec((ROUTE_ROWS, LANES), lambda i: (i, 0)),
                  pl.BlockSpec((ROUTE_ROWS, ROUTE_ROWS), lambda i: (0, 0))],
        out_specs=(pl.BlockSpec((ROUTE_ROWS, LANES), lambda i: (i, 0)),
                   pl.BlockSpec((8, LANES), lambda i: (0, 0))),
        scratch_shapes=[pltpu.VMEM((1, LANES), F32)],
        compiler_params=_cparams(("arbitrary",)),
    )(logits, jnp.asarray(_STRICT_LOWER, BF16))


def _row_copy(h_hbm, xbuf, sem, slot, row, tok):
    return pltpu.make_async_copy(h_hbm.at[pl.ds(tok, 1), :], xbuf.at[slot, pl.ds(row, 1), :], sem.at[slot])


def _expert_kernel(be_ref, cur_ref, nxt_ref, h_hbm, wg_ref, wu_ref, wd_ref, o_ref,
                   wg_s, wu_s, wd_s, xbuf, sem):
    i = pl.program_id(0)
    last = pl.num_programs(0) - 1
    slot = i % 2

    def start_block(idx_ref, s):
        for r in range(MOE_ROWS):
            _row_copy(h_hbm, xbuf, sem, s, r, idx_ref[0, 0, r]).start()

    def wait_block(s):
        for r in range(MOE_ROWS):
            _row_copy(h_hbm, xbuf, sem, s, r, 0).wait()

    @pl.when(i == 0)
    def _():
        start_block(cur_ref, 0)

    start_block(nxt_ref, 1 - slot)
    wait_block(slot)

    @pl.when((i == 0) | (be_ref[i] != be_ref[jnp.maximum(i - 1, 0)]))
    def _():
        wg_s[...] = wg_ref[0, 0].astype(BF16)
        wu_s[...] = wu_ref[0, 0].astype(BF16)
        wd_s[...] = wd_ref[0, 0].astype(BF16)

    x = xbuf[slot].astype(BF16)
    gte = jnp.dot(x, wg_s[...], preferred_element_type=F32)
    up = jnp.dot(x, wu_s[...], preferred_element_type=F32)
    hid = (gte * _sigmoid(gte) * up).astype(BF16)
    o_ref[...] = jnp.dot(hid, wd_s[...], preferred_element_type=F32).astype(o_ref.dtype)

    @pl.when(i == last)
    def _():
        wait_block(1 - slot)


def expert_ffn(h_all, src_tok, block_expert, wg, wu, wd, layer):
    d = h_all.shape[1]
    de = wg.shape[3]
    n_blocks = block_expert.shape[0]
    src3 = src_tok.reshape(n_blocks, 1, MOE_ROWS)
    return pl.pallas_call(
        _expert_kernel,
        out_shape=jax.ShapeDtypeStruct((n_blocks * MOE_ROWS, d), BF16),
        grid_spec=pltpu.PrefetchScalarGridSpec(
            num_scalar_prefetch=1,
            grid=(n_blocks,),
            in_specs=[
                pl.BlockSpec((1, 1, MOE_ROWS), lambda i, be: (i, 0, 0), memory_space=pltpu.SMEM),
                pl.BlockSpec((1, 1, MOE_ROWS), lambda i, be: (jnp.minimum(i + 1, n_blocks - 1), 0, 0),
                             memory_space=pltpu.SMEM),
                pl.BlockSpec(memory_space=pl.ANY),
                pl.BlockSpec((1, 1, d, de), lambda i, be: (layer, be[i], 0, 0)),
                pl.BlockSpec((1, 1, d, de), lambda i, be: (layer, be[i], 0, 0)),
                pl.BlockSpec((1, 1, de, d), lambda i, be: (layer, be[i], 0, 0)),
            ],
            out_specs=pl.BlockSpec((MOE_ROWS, d), lambda i, be: (i, 0)),
            scratch_shapes=[pltpu.VMEM((d, de), BF16), pltpu.VMEM((d, de), BF16), pltpu.VMEM((de, d), BF16),
                            pltpu.VMEM((2, MOE_ROWS, d), F32), pltpu.SemaphoreType.DMA((2,))],
        ),
        compiler_params=_cparams(("arbitrary",)),
    )(block_expert, src3, src3, h_all, wg, wu, wd)


def _combine_kernel(x_ref, r1_ref, r2_ref, info_ref, gate_ref, *rest, final):
    o_ref = rest[-1]
    info = info_ref[...]
    y = info[:, 2:3] * r1_ref[...].astype(F32) + info[:, 3:4] * r2_ref[...].astype(F32)
    x = x_ref[...] + gate_ref[0] * y
    if final:
        ms = jnp.mean(x * x, axis=-1, keepdims=True)
        x = x * lax.rsqrt(ms + EPS) * rest[0][...]
    o_ref[...] = x


def combine(x2d, r1, r2, info, row_off, gate, rows_per_batch, final_g=None):
    t, d = x2d.shape
    tm = min(ROUTE_ROWS, rows_per_batch)
    per = rows_per_batch // tm
    off = row_off // tm
    row = lambda i: (i, 0)
    shifted = lambda i: (i + off, 0)
    specs = [pl.BlockSpec((tm, d), row), pl.BlockSpec((tm, d), shifted), pl.BlockSpec((tm, d), shifted),
             pl.BlockSpec((tm, LANES), shifted), pl.BlockSpec((1, 1, d), lambda i: (i // per, 0, 0))]
    args = [x2d, r1, r2, info, gate]
    if final_g is not None:
        specs.append(pl.BlockSpec((1, d), lambda i: (0, 0)))
        args.append(final_g.reshape(1, d))
    return pl.pallas_call(
        functools.partial(_combine_kernel, final=final_g is not None),
        out_shape=jax.ShapeDtypeStruct((t, d), F32), grid=(t // tm,),
        in_specs=specs, out_specs=pl.BlockSpec((tm, d), row),
        compiler_params=_cparams(("parallel",)),
    )(*args)


def _dispatch(info, counts_row):
    n_tok = info.shape[0]
    expert = info[:, 0:TOP_K].astype(jnp.int32)
    rank = info[:, 4:4 + TOP_K].astype(jnp.int32)
    counts = counts_row[0, :N_EXPERTS].astype(jnp.int32)
    padded = (counts + MOE_ROWS - 1) // MOE_ROWS * MOE_ROWS
    pad_end = jnp.cumsum(padded)
    pad_start = pad_end - padded
    hot = expert[:, :, None] == jnp.arange(N_EXPERTS, dtype=jnp.int32)
    dest = jnp.sum(jnp.where(hot, pad_start, 0), axis=-1) + rank
    n_blocks = -(-(n_tok * TOP_K + N_EXPERTS * (MOE_ROWS - 1)) // MOE_ROWS)
    token = jnp.broadcast_to(jnp.arange(n_tok, dtype=jnp.int32)[:, None], dest.shape)
    src_tok = jnp.zeros((n_blocks * MOE_ROWS,), jnp.int32).at[dest.reshape(-1)].set(
        token.reshape(-1), unique_indices=True, mode="promise_in_bounds")
    block_start = jnp.arange(n_blocks, dtype=jnp.int32)[:, None] * MOE_ROWS
    block_expert = jnp.minimum(jnp.sum((pad_end[None, :] <= block_start).astype(jnp.int32), axis=1),
                               N_EXPERTS - 1)
    return src_tok, dest, block_expert


def _rope_tables(seq):
    n_rows = seq // GRID_W
    row = jnp.repeat(jnp.arange(n_rows), GRID_W).astype(F32)
    col = jnp.tile(jnp.arange(GRID_W), n_rows).astype(F32)
    axis_pairs = HEAD_DIM // 4
    inv = ROPE_THETA ** (-jnp.arange(axis_pairs, dtype=F32) / axis_pairs)
    ang = jnp.concatenate([row[:, None] * inv, col[:, None] * inv], axis=-1)
    cos, sin = jnp.cos(ang), jnp.sin(ang)
    cos128 = jnp.concatenate([cos, cos, cos, cos], axis=-1)
    sins128 = jnp.concatenate([-sin, sin, -sin, sin], axis=-1)
    return cos128, sins128


def kernel(x, c, ctx, c_ctx, w_mod, b_mod, norm1_g, norm2_g, w_in, hgrn_lb_logits, hgrn_out_norm_g,
           attn_q_norm_g, attn_k_norm_g, swa_sink, w_branch_a, w_branch_b, w_branch_c, w_out,
           w_group, w_router, w_exp_gate, w_exp_up, w_exp_down, final_norm_g):
    bsz, seq, d = x.shape
    n_ctx = ctx.shape[1]
    depth = w_mod.shape[0]
    cos, sins = _rope_tables(seq)
    lb_p = jax.nn.softmax(hgrn_lb_logits.astype(F32), axis=0)
    lower_bounds = jnp.cumsum(lb_p, axis=0) - lb_p[0]
    no_sink = jnp.zeros((N_HEADS,), F32)
    ones128 = jnp.ones((1, LANES), F32)

    xl = x.reshape(bsz * seq, d)
    xc = ctx.reshape(bsz * n_ctx, d)
    for layer in range(depth):
        ctx_out = layer < depth - 1
        mod_l = jax.nn.silu(c) @ w_mod[layer] + b_mod[layer]
        mod_c = jnp.broadcast_to(jax.nn.silu(c_ctx) @ w_mod[layer] + b_mod[layer], (bsz, 6 * d))
        ml = [m.reshape(bsz, 1, d) for m in jnp.split(mod_l, 6, axis=-1)]
        mc = [m.reshape(bsz, 1, d) for m in jnp.split(mod_c, 6, axis=-1)]

        w_in_b = w_in[layer].astype(BF16)
        p_l = inproj(xl, seq, norm1_g[layer], ml[1], ml[0], w_in_b, 1024, 3584)
        p_c = inproj(xc, n_ctx, norm1_g[layer], mc[1], mc[0], w_in_b, 256, 1024)
        p_l3 = p_l.reshape(bsz, seq, D_IN)
        p_c3 = p_c.reshape(bsz, n_ctx, D_IN)

        s0 = jnp.zeros((bsz, 2, HG_HEADS, HG_KDIM, HG_KDIM), F32)
        o_c, s_c = hgrn_scan(p_c3, lower_bounds[layer], s0, 256)
        o_l, _ = hgrn_scan(p_l3, lower_bounds[layer], s_c, 512)

        gq = jnp.tile(attn_q_norm_g[layer], 2).reshape(1, LANES)
        gk = jnp.tile(attn_k_norm_g[layer], 2).reshape(1, LANES)
        kt_l, v_l, kn_l = prep_kv(p_l3, COL["ak"], COL["av"], gk, cos, sins, True, True, 512)
        kt_c, v_c, kn_c = prep_kv(p_c3, COL["ak"], COL["av"], gk, cos, sins, True, False, 256)
        kmax = _key_norm_max([kn_l, kn_c])
        q_l, u_l = prep_q(p_l3, COL["aq"], gq, cos, sins, kmax, True, True, 512)
        kt_all = jnp.concatenate([kt_c, kt_l], axis=3)
        v_all = jnp.concatenate([v_c, v_l], axis=2)
        b_l = attention(q_l, kt_all, v_all, jnp.max(u_l), no_sink, 512, 1280)

        skt_l, sv_l, skn_l = prep_kv(p_l3, COL["sk"], COL["sv"], ones128, cos, sins, False, True, 512)
        skt_c, sv_c, skn_c = prep_kv(p_c3, COL["sk"], COL["sv"], ones128, cos, sins, False, False, 256)
        skmax = _key_norm_max([skn_l, skn_c])
        sq_l, su_l = prep_q(p_l3, COL["sq"], ones128, cos, sins, skmax, False, True, 512)
        zk = jnp.zeros((bsz, N_KV, LANES, WINDOW), BF16)
        zv = jnp.zeros((bsz, N_KV, WINDOW, LANES), BF16)
        skt_all = jnp.concatenate([skt_c, zk, skt_l, zk], axis=3)
        sv_all = jnp.concatenate([sv_c, zv, sv_l, zv], axis=2)
        c_l = attention(sq_l, skt_all, sv_all, jnp.max(su_l), swa_sink[layer], 256, 256,
                        use_sink=True, band=n_ctx)

        wa = w_branch_a[layer].astype(BF16)
        wb = w_branch_b[layer].astype(BF16)
        wc = w_branch_c[layer].astype(BF16)
        wo = w_out[layer].astype(BF16)
        w_r = jnp.concatenate(
            [w_group[layer], w_router[layer], jnp.zeros((d, LANES - N_GROUPS - N_EXPERTS), F32)], axis=1)
        w_r_hi = w_r.astype(BF16)
        w_r_lo = (w_r - w_r_hi.astype(F32)).astype(BF16)
        w_route = jnp.concatenate([w_r_hi, w_r_hi, w_r_lo], axis=0)
        xl, h_l, lg_l = merge(o_l.reshape(2, bsz * seq, HG_W), p_l, b_l.reshape(-1, ATT_W),
                              c_l.reshape(-1, ATT_W), xl, seq, wa, wb, wc, wo, hgrn_out_norm_g[layer],
                              ml[2], norm2_g[layer], ml[4], ml[3], w_route, 256)
        if ctx_out:
            q_c, u_c = prep_q(p_c3, COL["aq"], gq, cos, sins, kmax, True, False, 256)
            b_c = attention(q_c, kt_c, v_c, jnp.max(u_c), no_sink, 256, 256)
            sq_c, su_c = prep_q(p_c3, COL["sq"], ones128, cos, sins, skmax, False, False, 256)
            c_c = attention(sq_c, skt_c, sv_c, jnp.max(su_c), swa_sink[layer], 256, 256, use_sink=True)
            xc, h_c, lg_c = merge(o_c.reshape(2, bsz * n_ctx, HG_W), p_c, b_c.reshape(-1, ATT_W),
                                  c_c.reshape(-1, ATT_W), xc, n_ctx, wa, wb, wc, wo, hgrn_out_norm_g[layer],
                                  mc[2], norm2_g[layer], mc[4], mc[3], w_route, 256)
            h_all = jnp.concatenate([h_c, h_l], axis=0)
            lg_all = jnp.concatenate([lg_c, lg_l], axis=0)
        else:
            h_all, lg_all = h_l, lg_l

        info, counts_row = route(lg_all)
        src_tok, dest, block_expert = _dispatch(info, counts_row)
        out_rows = expert_ffn(h_all, src_tok, block_expert, w_exp_gate, w_exp_up, w_exp_down, layer)
        r1 = jnp.take(out_rows, dest[:, 0], axis=0, mode="clip")
        r2 = jnp.take(out_rows, dest[:, 1], axis=0, mode="clip")
        n_c = bsz * n_ctx if ctx_out else 0
        if ctx_out:
            xc = combine(xc, r1, r2, info, 0, mc[5], n_ctx)
        xl = combine(xl, r1, r2, info, n_c, ml[5], seq, final_g=final_norm_g if layer == depth - 1 else None)
    return xl.reshape(bsz, seq, d)
```

```python
import functools
import math

import numpy as np
import jax
import jax.numpy as jnp
from jax import lax
from jax.experimental import pallas as pl
from jax.experimental.pallas import tpu as pltpu

F32 = jnp.float32
BF16 = jnp.bfloat16

EPS = 1e-6
MASK_VALUE = -1e30
TINY = 1e-30
GRID_W = 64
ROPE_THETA = 10000.0

HG_HEADS = 4
HG_KDIM = 128
HG_W = HG_HEADS * HG_KDIM
HEAD_DIM = 64
N_HEADS = 8
N_KV = 2
GROUP = N_HEADS // N_KV
ATT_W = N_HEADS * HEAD_DIM
KV_W = N_KV * HEAD_DIM
WINDOW = 128
ATTN_SCALE = HEAD_DIM ** -0.5
N_GROUPS = 4
EXP_PER_GROUP = 8
N_EXPERTS = N_GROUPS * EXP_PER_GROUP
TOP_K = 2

LANES = 128
HG_CHUNK = 128
HG_LEVELS = int(math.log2(HG_CHUNK))
HG_TOTAL_ROWS = 16
MOE_ROWS = 256
ROUTE_ROWS = 512
VMEM_LIMIT = 56 * 1024 * 1024

LOG2E = 1.4426950408889634
SHIFT_HEADROOM = 57.0
SAFE_BOUND = 90.0


def _cparams(sem):
    return pltpu.CompilerParams(dimension_semantics=sem, vmem_limit_bytes=VMEM_LIMIT)


_SIZES = (HG_W, HG_W, HG_W, HG_W, HG_W, ATT_W, KV_W, KV_W, ATT_W, KV_W, KV_W, 3 * 1024)
_NAMES = ("hq", "ff", "fb", "hi", "hg", "aq", "ak", "av", "sq", "sk", "sv", "gates")
COL = {n: int(sum(_SIZES[:i])) for i, n in enumerate(_NAMES)}
D_IN = int(sum(_SIZES))


def _inproj_kernel(x_ref, g_ref, sc_ref, sh_ref, w_ref, o_ref, h_scr):
    @pl.when(pl.program_id(1) == 0)
    def _():
        x = x_ref[...]
        ms = jnp.mean(x * x, axis=-1, keepdims=True)
        y = x * lax.rsqrt(ms + EPS) * g_ref[...]
        h_scr[...] = (y * (1.0 + sc_ref[0]) + sh_ref[0]).astype(BF16)

    o_ref[...] = jnp.dot(h_scr[...], w_ref[...], preferred_element_type=F32).astype(o_ref.dtype)


def inproj(x2d, rows_per_batch, g, scale, shift, w_bf16, tm, tn):
    t, d = x2d.shape
    n = w_bf16.shape[1]
    tm = min(tm, rows_per_batch)
    per = rows_per_batch // tm
    return pl.pallas_call(
        _inproj_kernel,
        out_shape=jax.ShapeDtypeStruct((t, n), BF16),
        grid=(t // tm, n // tn),
        in_specs=[
            pl.BlockSpec((tm, d), lambda i, j: (i, 0)),
            pl.BlockSpec((1, d), lambda i, j: (0, 0)),
            pl.BlockSpec((1, 1, d), lambda i, j: (i // per, 0, 0)),
            pl.BlockSpec((1, 1, d), lambda i, j: (i // per, 0, 0)),
            pl.BlockSpec((d, tn), lambda i, j: (0, j)),
        ],
        out_specs=pl.BlockSpec((tm, tn), lambda i, j: (i, j)),
        scratch_shapes=[pltpu.VMEM((tm, d), BF16)],
        compiler_params=_cparams(("parallel", "arbitrary")),
    )(x2d, g.reshape(1, d), scale, shift, w_bf16)


def _hgrn_exponent_matrices():
    c, nl = HG_CHUNK, HG_LEVELS
    rows = (nl + 2) * c + HG_TOTAL_ROWS
    out = np.zeros((2, rows, c), np.float32)
    for d in range(2):
        pos = np.arange(c) if d == 0 else c - 1 - np.arange(c)
        for l in range(nl):
            m = 1 << l
            for t in range(c):
                p = pos[t]
                mid = (p // (2 * m)) * 2 * m + m
                if p >= mid:
                    sel = (pos >= mid) & (pos <= p)
                else:
                    sel = (pos > p) & (pos <= mid - 1)
                out[d, l * c + t, sel] = 1.0
        for t in range(c):
            out[d, nl * c + t, pos <= pos[t]] = 1.0
            out[d, (nl + 1) * c + t, pos > pos[t]] = 1.0
        out[d, (nl + 2) * c:, :] = 1.0
    return np.concatenate([out, out], axis=2)


_HG_EXP_MATS = _hgrn_exponent_matrices()


def _sigmoid(x):
    return 1.0 / (1.0 + jnp.exp(-x))


def _hgrn_kernel(q_ref, f_ref, i_ref, lb_ref, mat_ref, s0_ref, o_ref, st_ref, *, n_chunks):
    c, nl = HG_CHUNK, HG_LEVELS
    d = pl.program_id(1)

    @pl.when(pl.program_id(2) == 0)
    def _():
        st_ref[...] = s0_ref[...]

    row = lax.broadcasted_iota(jnp.int32, (c, HG_W), 0)
    pos = row + d * (c - 1 - 2 * row)
    r_i = lax.broadcasted_iota(jnp.int32, (c, c), 0)
    c_i = lax.broadcasted_iota(jnp.int32, (c, c), 1)
    xor_rc = r_i ^ c_i
    top_bit = sum((xor_rc >= (1 << j)).astype(jnp.int32) for j in range(1, nl))
    pair_level = jnp.where((r_i - c_i) * (1 - 2 * d) > 0, top_bit, -1)

    def chunk(ci, carry):
        cc = ci + d * (n_chunks - 1 - 2 * ci)
        r0 = pl.multiple_of(cc * c, c)
        qraw = q_ref[0, pl.ds(r0, c), :].astype(F32)
        fz = f_ref[0, pl.ds(r0, c), :].astype(F32)
        vb = i_ref[0, pl.ds(r0, c), :]
        v = vb.astype(F32)
        lb = lb_ref[...]
        sig = _sigmoid(fz)
        logf = jnp.log(jnp.maximum(lb + (1.0 - lb) * sig, TINY))
        key = (1.0 - lb) * (1.0 - sig)
        qh = qraw * _sigmoid(qraw) * (HG_KDIM ** -0.5)
        logf2 = logf * LOG2E
        hi = logf2.astype(BF16)
        lo = (logf2 - hi.astype(F32)).astype(BF16)
        hilo = jnp.concatenate([hi, lo], axis=0)

        dec_all = jnp.exp2(jnp.dot(mat_ref[0], hilo, preferred_element_type=F32))

        def decay(r_lo, r_hi):
            return dec_all[r_lo:r_hi]

        a = [jnp.zeros((c, c), F32)] * HG_HEADS
        for l in range(nl):
            x = (jnp.where(((pos >> l) & 1) == 1, qh, key) * decay(l * c, (l + 1) * c)).astype(BF16)
            for h in range(HG_HEADS):
                xh = x[:, h * LANES:(h + 1) * LANES]
                prod = lax.dot_general(xh, xh, (((1,), (1,)), ((), ())), preferred_element_type=F32)
                a[h] = jnp.where(pair_level == l, prod, a[h])
        q_in = (qh * decay(nl * c, (nl + 1) * c)).astype(BF16)
        k_out = (key * decay((nl + 1) * c, (nl + 2) * c)).astype(BF16)
        total = decay((nl + 2) * c, (nl + 2) * c + HG_TOTAL_ROWS)[0:1]
        qk = qh * key
        for h in range(HG_HEADS):
            lanes = slice(h * LANES, (h + 1) * LANES)
            st = st_ref[0, 0, h]
            o = (jnp.dot(a[h].astype(BF16), vb[:, lanes], preferred_element_type=F32)
                 + jnp.sum(qk[:, lanes], axis=1, keepdims=True) * v[:, lanes]
                 + lax.dot_general(q_in[:, lanes], st.astype(BF16), (((1,), (1,)), ((), ())),
                                   preferred_element_type=F32))
            o_ref[0, 0, pl.ds(r0, c), lanes] = o.astype(o_ref.dtype)
            st_ref[0, 0, h] = total[:, lanes] * st + lax.dot_general(
                vb[:, lanes], k_out[:, lanes], (((0,), (0,)), ((), ())), preferred_element_type=F32)
        return carry

    lax.fori_loop(0, n_chunks, chunk, 0, unroll=2 if n_chunks % 2 == 0 else 1)


def hgrn_scan(p3, lb, s0, tb):
    b, l, _ = p3.shape
    tb = min(tb, l)
    nb = l // tb
    blk = lambda d, n: n + d * (nb - 1 - 2 * n)
    wblk = HG_W
    kern = functools.partial(_hgrn_kernel, n_chunks=tb // HG_CHUNK)
    mats = jnp.asarray(_HG_EXP_MATS, BF16)
    return pl.pallas_call(
        kern,
        out_shape=(jax.ShapeDtypeStruct((2, b, l, HG_W), BF16),
                   jax.ShapeDtypeStruct((b, 2, HG_HEADS, HG_KDIM, HG_KDIM), F32)),
        grid=(b, 2, nb),
        in_specs=[
            pl.BlockSpec((1, tb, wblk), lambda bi, d, n: (bi, blk(d, n), COL["hq"] // wblk)),
            pl.BlockSpec((1, tb, wblk), lambda bi, d, n: (bi, blk(d, n), COL["ff"] // wblk + d)),
            pl.BlockSpec((1, tb, wblk), lambda bi, d, n: (bi, blk(d, n), COL["hi"] // wblk)),
            pl.BlockSpec((1, HG_W), lambda bi, d, n: (0, 0)),
            pl.BlockSpec((1,) + _HG_EXP_MATS.shape[1:], lambda bi, d, n: (d, 0, 0)),
            pl.BlockSpec((1, 1, HG_HEADS, HG_KDIM, HG_KDIM), lambda bi, d, n: (bi, d, 0, 0, 0)),
        ],
        out_specs=(
            pl.BlockSpec((1, 1, tb, HG_W), lambda bi, d, n: (d, bi, blk(d, n), 0)),
            pl.BlockSpec((1, 1, HG_HEADS, HG_KDIM, HG_KDIM), lambda bi, d, n: (bi, d, 0, 0, 0)),
        ),
        compiler_params=_cparams(("parallel", "arbitrary", "arbitrary")),
    )(p3, p3, p3, lb.reshape(1, HG_W), mats, s0)


_GROUP_ONES = np.kron(np.eye(LANES // HEAD_DIM, dtype=np.float32), np.ones((HEAD_DIM, HEAD_DIM), np.float32))
_GROUP_ONES2 = np.concatenate([_GROUP_ONES, _GROUP_ONES], axis=0)


def _group_ssq(x, ones2):
    sq = x * x
    hi = sq.astype(BF16)
    lo = (sq - hi.astype(F32)).astype(BF16)
    return jnp.dot(jnp.concatenate([hi, lo], axis=1), ones2, preferred_element_type=F32)


def _prep_tile(x, g, cos, sins, ones2, norm, rope):
    if norm:
        x = x * lax.rsqrt(_group_ssq(x, ones2) * (1.0 / HEAD_DIM) + EPS) * g
    if rope:
        lane = lax.broadcasted_iota(jnp.int32, x.shape, 1)
        first = (lane % HEAD_DIM) < (HEAD_DIM // 2)
        other = jnp.where(first, pltpu.roll(x, LANES - HEAD_DIM // 2, 1), pltpu.roll(x, HEAD_DIM // 2, 1))
        x = x * cos + other * sins
    return x


def _prep_kv_kernel(k_ref, v_ref, g_ref, cos_ref, sin_ref, ones_ref, kt_ref, vo_ref, kn_ref, *, norm, rope):
    k = _prep_tile(k_ref[0].astype(F32), g_ref[...], cos_ref[...], sin_ref[...], ones_ref[...], norm, rope)
    kb = k.astype(BF16).astype(F32)
    ssq = _group_ssq(kb, ones_ref[...])
    kn_ref[0, 0] = jnp.broadcast_to(jnp.max(ssq, axis=0, keepdims=True), (8, LANES))
    kt = kb.T
    row = lax.broadcasted_iota(jnp.int32, kt.shape, 0)
    v = v_ref[0]
    low = lax.broadcasted_iota(jnp.int32, v.shape, 1) < HEAD_DIM
    one = jnp.ones((), v.dtype)
    for g in range(N_KV):
        ktg = kt if g == 0 else jnp.concatenate([kt[HEAD_DIM:], kt[:HEAD_DIM]], axis=0)
        kt_ref[0, g] = jnp.where(row < HEAD_DIM, ktg, jnp.where(row == HEAD_DIM, 1.0, 0.0)).astype(BF16)
        vg = v if g == 0 else jnp.concatenate([v[:, HEAD_DIM:], v[:, :HEAD_DIM]], axis=1)
        vo_ref[0, g] = jnp.where(low, vg, one).astype(BF16)


def _prep_q_kernel(kmax_ref, x0_ref, x1_ref, x2_ref, x3_ref, g_ref, cos_ref, sin_ref, ones_ref, o_ref, u_ref,
                   *, norm, rope):
    b = pl.program_id(0)
    lane = lax.broadcasted_iota(jnp.int32, (x0_ref.shape[1], LANES), 1)
    umax = None
    for tile, x_ref in enumerate((x0_ref, x1_ref, x2_ref, x3_ref)):
        kv = (2 * tile) // GROUP
        x = x_ref[0].astype(F32)
        y = _prep_tile(x, g_ref[...], cos_ref[...], sin_ref[...], ones_ref[...], norm, rope) * (ATTN_SCALE * LOG2E)
        yb = y.astype(BF16).astype(F32)
        u2 = jnp.sqrt(_group_ssq(yb, ones_ref[...])) * kmax_ref[b * N_KV + kv]
        tmax = jnp.max(u2, axis=0, keepdims=True)
        umax = tmax if umax is None else jnp.maximum(umax, tmax)
        neg_shift = SHIFT_HEADROOM - u2
        y_sw = pltpu.roll(yb, HEAD_DIM, 1)
        ns_sw = pltpu.roll(neg_shift, HEAD_DIM, 1)
        for half in range(2):
            h = 2 * tile + half
            data = yb if half == 0 else y_sw
            ns = ns_sw if half == 0 else neg_shift
            o_ref[0, :, h * LANES:(h + 1) * LANES] = jnp.where(
                lane < HEAD_DIM, data, jnp.where(lane == HEAD_DIM, ns, 0.0)).astype(BF16)
    u_ref[0, 0] = jnp.broadcast_to(umax, (8, LANES))


def prep_kv(p3, col_k, col_v, g128, cos, sins, norm, rope, tq):
    b, l, _ = p3.shape
    tq = min(tq, l)
    ones2 = jnp.asarray(_GROUP_ONES2, BF16)
    return pl.pallas_call(
        functools.partial(_prep_kv_kernel, norm=norm, rope=rope),
        out_shape=(jax.ShapeDtypeStruct((b, N_KV, LANES, l), BF16),
                   jax.ShapeDtypeStruct((b, N_KV, l, LANES), BF16),
                   jax.ShapeDtypeStruct((b, l // tq, 8, LANES), F32)),
        grid=(b, l // tq),
        in_specs=[
            pl.BlockSpec((1, tq, KV_W), lambda bi, i: (bi, i, col_k // KV_W)),
            pl.BlockSpec((1, tq, KV_W), lambda bi, i: (bi, i, col_v // KV_W)),
            pl.BlockSpec((1, LANES), lambda bi, i: (0, 0)),
            pl.BlockSpec((tq, LANES), lambda bi, i: (i, 0)),
            pl.BlockSpec((tq, LANES), lambda bi, i: (i, 0)),
            pl.BlockSpec((2 * LANES, LANES), lambda bi, i: (0, 0)),
        ],
        out_specs=(pl.BlockSpec((1, N_KV, LANES, tq), lambda bi, i: (bi, 0, 0, i)),
                   pl.BlockSpec((1, N_KV, tq, LANES), lambda bi, i: (bi, 0, i, 0)),
                   pl.BlockSpec((1, 1, 8, LANES), lambda bi, i: (bi, i, 0, 0))),
        compiler_params=_cparams(("parallel", "parallel")),
    )(p3, p3, g128, cos, sins, ones2)


def prep_q(p3, col, g128, cos, sins, kmax, norm, rope, tq):
    b, l, _ = p3.shape
    tq = min(tq, l)
    ones2 = jnp.asarray(_GROUP_ONES2, BF16)
    return pl.pallas_call(
        functools.partial(_prep_q_kernel, norm=norm, rope=rope),
        out_shape=(jax.ShapeDtypeStruct((b, l, N_HEADS * LANES), BF16),
                   jax.ShapeDtypeStruct((b, l // tq, 8, LANES), F32)),
        grid_spec=pltpu.PrefetchScalarGridSpec(
            num_scalar_prefetch=1,
            grid=(b, l // tq),
            in_specs=[
                pl.BlockSpec((1, tq, LANES), lambda bi, i, s, t=t: (bi, i, col // LANES + t))
                for t in range(ATT_W // LANES)
            ] + [
                pl.BlockSpec((1, LANES), lambda bi, i, s: (0, 0)),
                pl.BlockSpec((tq, LANES), lambda bi, i, s: (i, 0)),
                pl.BlockSpec((tq, LANES), lambda bi, i, s: (i, 0)),
                pl.BlockSpec((2 * LANES, LANES), lambda bi, i, s: (0, 0)),
            ],
            out_specs=(pl.BlockSpec((1, tq, N_HEADS * LANES), lambda bi, i, s: (bi, i, 0)),
                       pl.BlockSpec((1, 1, 8, LANES), lambda bi, i, s: (bi, i, 0, 0))),
        ),
        compiler_params=_cparams(("parallel", "parallel")),
    )(kmax, p3, p3, p3, p3, g128, cos, sins, ones2)


def _key_norm_max(kn_list):
    kn = functools.reduce(jnp.maximum, [jnp.max(k, axis=(1, 2)) for k in kn_list])
    return jnp.sqrt(kn[:, ::HEAD_DIM]).reshape(-1)


def _attn_kernel(sink_ref, q_ref, kt_ref, v_ref, o_ref, *, tk, online, use_sink, band, seq):
    tq = q_ref.shape[1]
    rows = GROUP * tq
    lane = lax.broadcasted_iota(jnp.int32, (tq, LANES), 1)
    low = lane < HEAD_DIM
    if band is not None:
        width = tq + 2 * WINDOW
        start = pl.multiple_of(pl.program_id(1) * tq, tq)
        r = lax.broadcasted_iota(jnp.int32, (rows, width), 0) % tq
        kp = lax.broadcasted_iota(jnp.int32, (rows, width), 1)
        key_pos = start + kp - WINDOW
        mask = (jnp.abs(kp - WINDOW - r) <= WINDOW) & (key_pos >= 0) & (key_pos < seq)
    for g in range(N_KV):
        tiles = [q_ref[0, :, (g * GROUP + hh) * LANES:(g * GROUP + hh + 1) * LANES] for hh in range(GROUP)]
        qg = jnp.concatenate(tiles, axis=0)

        def scores(kt, qg=qg):
            return jnp.dot(qg, kt, preferred_element_type=F32)

        def update(carry, s, v, msk=None):
            if msk is not None:
                s = jnp.where(msk, s, MASK_VALUE)
            if online:
                m, acc = carry
                m_new = jnp.maximum(m, jnp.max(s, axis=1, keepdims=True))
                p = jnp.exp2(s - m_new).astype(BF16)
                return m_new, jnp.exp2(m - m_new) * acc + jnp.dot(p, v, preferred_element_type=F32)
            return carry + jnp.dot(jnp.exp2(s).astype(BF16), v, preferred_element_type=F32)

        acc0 = jnp.zeros((rows, LANES), F32)
        carry = (jnp.full((rows, 1), MASK_VALUE, F32), acc0) if online else acc0
        if band is None:
            def chunk(c, k0, g=g, scores=scores, update=update):
                return update(c, scores(kt_ref[0, g, :, pl.ds(k0, tk)]), v_ref[0, g, pl.ds(k0, tk), :])

            nk = kt_ref.shape[3] // tk
            for j in range(nk % 2):
                carry = chunk(carry, j * tk)

            def body(j, c, chunk=chunk, first=(nk % 2) * tk):
                k0 = pl.multiple_of(first + j * (2 * tk), tk)
                return chunk(chunk(c, k0), pl.multiple_of(k0 + tk, tk))

            carry = lax.fori_loop(0, nk // 2, body, carry)
        else:
            carry = update(carry, scores(kt_ref[0, g, :, 0:band]), v_ref[0, g, 0:band, :])
            b0 = pl.multiple_of(band + start, LANES)
            carry = update(carry, scores(kt_ref[0, g, :, pl.ds(b0, width)]), v_ref[0, g, pl.ds(b0, width), :], mask)
        acc = carry[1] if online else carry
        if use_sink:
            e = jnp.concatenate([t[:, HEAD_DIM:HEAD_DIM + 1].astype(F32) + sink_ref[g * GROUP + hh] * LOG2E
                                 for hh, t in enumerate(tiles)], axis=0)
            if online:
                e = e - carry[0]
            lane_r = lax.broadcasted_iota(jnp.int32, (rows, LANES), 1)
            acc = acc + jnp.where(lane_r >= HEAD_DIM, jnp.exp2(e), 0.0)
        o = acc * pltpu.roll(1.0 / acc, HEAD_DIM, 1)
        for pair in range(GROUP // 2):
            a = o[(2 * pair) * tq:(2 * pair + 1) * tq]
            b = o[(2 * pair + 1) * tq:(2 * pair + 2) * tq]
            t0 = (g * GROUP // 2 + pair) * LANES
            o_ref[0, :, t0:t0 + LANES] = jnp.where(low, a, pltpu.roll(b, HEAD_DIM, 1)).astype(o_ref.dtype)


def _attention(q, kt, v, sink, tq, tk, online, use_sink, band):
    b, l, _ = q.shape
    sk = kt.shape[3]
    tq = min(tq, l)
    tk = min(tk, sk)
    kern = functools.partial(_attn_kernel, tk=tk, online=online, use_sink=use_sink, band=band, seq=l)
    return pl.pallas_call(
        kern,
        out_shape=jax.ShapeDtypeStruct((b, l, ATT_W), BF16),
        grid_spec=pltpu.PrefetchScalarGridSpec(
            num_scalar_prefetch=1,
            grid=(b, l // tq),
            in_specs=[
                pl.BlockSpec((1, tq, N_HEADS * LANES), lambda bi, i, s: (bi, i, 0)),
                pl.BlockSpec((1, N_KV, LANES, sk), lambda bi, i, s: (bi, 0, 0, 0), pipeline_mode=pl.Buffered(1)),
                pl.BlockSpec((1, N_KV, sk, LANES), lambda bi, i, s: (bi, 0, 0, 0), pipeline_mode=pl.Buffered(1)),
            ],
            out_specs=pl.BlockSpec((1, tq, ATT_W), lambda bi, i, s: (bi, i, 0)),
        ),
        compiler_params=_cparams(("parallel", "arbitrary")),
    )(sink, q, kt, v)


def attention(q, kt, v, ubound, sink, tq, tk, use_sink=False, band=None):
    return lax.cond(ubound <= SAFE_BOUND,
                    lambda: _attention(q, kt, v, sink, tq, tk, False, use_sink, band),
                    lambda: _attention(q, kt, v, sink, tq, tk, True, use_sink, band))


def _merge_kernel(of_ref, ob_ref, hg_ref, b_ref, c_ref, ga_ref, gb_ref, gc_ref, x_ref,
                  wa_ref, wb_ref, wc_ref, wo_ref, hgn_ref, gate1_ref, n2_ref, sc2_ref, sh2_ref, wr_ref,
                  xo_ref, h2_ref, lg_ref):
    o = of_ref[0].astype(F32) + ob_ref[0].astype(F32)
    tiles = []
    for h in range(HG_HEADS):
        t = o[:, h * LANES:(h + 1) * LANES]
        ms = jnp.mean(t * t, axis=-1, keepdims=True)
        tiles.append(t * lax.rsqrt(ms + EPS) * hgn_ref[...])
    hg = hg_ref[...].astype(F32)
    a = (jnp.concatenate(tiles, axis=1) * (hg * _sigmoid(hg))).astype(BF16)
    merged = (_sigmoid(ga_ref[...].astype(F32)) * jnp.dot(a, wa_ref[...], preferred_element_type=F32)
              + _sigmoid(gb_ref[...].astype(F32)) * jnp.dot(b_ref[...], wb_ref[...], preferred_element_type=F32)
              + _sigmoid(gc_ref[...].astype(F32)) * jnp.dot(c_ref[...], wc_ref[...], preferred_element_type=F32))
    y = jnp.dot(merged.astype(BF16), wo_ref[...], preferred_element_type=F32)
    x = x_ref[...] + gate1_ref[0] * y
    xo_ref[...] = x
    ms = jnp.mean(x * x, axis=-1, keepdims=True)
    h2 = (x * lax.rsqrt(ms + EPS) * n2_ref[...]) * (1.0 + sc2_ref[0]) + sh2_ref[0]
    hi = h2.astype(BF16)
    lo = (h2 - hi.astype(F32)).astype(BF16)
    h2_ref[...] = h2
    lg_ref[...] = jnp.dot(jnp.concatenate([hi, lo, hi], axis=1), wr_ref[...], preferred_element_type=F32)


def merge(o_fb, p2d, b2d, c2d, x2d, rows_per_batch, wa, wb, wc, wo, hgn, gate1, n2, sc2, sh2, w_route, tm):
    t, d = x2d.shape
    tm = min(tm, rows_per_batch)
    per = rows_per_batch // tm
    row = lambda i: (i, 0)
    const = lambda i: (0, 0)
    bat = lambda i: (i // per, 0, 0)
    gcol = COL["gates"] // d
    return pl.pallas_call(
        _merge_kernel,
        out_shape=(jax.ShapeDtypeStruct((t, d), F32), jax.ShapeDtypeStruct((t, d), F32),
                   jax.ShapeDtypeStruct((t, LANES), F32)),
        grid=(t // tm,),
        in_specs=[
            pl.BlockSpec((1, tm, HG_W), lambda i: (0, i, 0)),
            pl.BlockSpec((1, tm, HG_W), lambda i: (1, i, 0)),
            pl.BlockSpec((tm, HG_W), lambda i: (i, COL["hg"] // HG_W)),
            pl.BlockSpec((tm, ATT_W), row),
            pl.BlockSpec((tm, ATT_W), row),
            pl.BlockSpec((tm, d), lambda i: (i, gcol)),
            pl.BlockSpec((tm, d), lambda i: (i, gcol + 1)),
            pl.BlockSpec((tm, d), lambda i: (i, gcol + 2)),
            pl.BlockSpec((tm, d), row),
            pl.BlockSpec((HG_W, d), const),
            pl.BlockSpec((ATT_W, d), const),
            pl.BlockSpec((ATT_W, d), const),
            pl.BlockSpec((d, d), const),
            pl.BlockSpec((1, LANES), const),
            pl.BlockSpec((1, 1, d), bat),
            pl.BlockSpec((1, d), const),
            pl.BlockSpec((1, 1, d), bat),
            pl.BlockSpec((1, 1, d), bat),
            pl.BlockSpec((3 * d, LANES), const),
        ],
        out_specs=(pl.BlockSpec((tm, d), row), pl.BlockSpec((tm, d), row), pl.BlockSpec((tm, LANES), row)),
        compiler_params=_cparams(("parallel",)),
    )(o_fb, o_fb, p2d, b2d, c2d, p2d, p2d, p2d, x2d, wa, wb, wc, wo, hgn.reshape(1, LANES),
      gate1, n2.reshape(1, d), sc2, sh2, w_route)


_STRICT_LOWER = np.tril(np.ones((ROUTE_ROWS, ROUTE_ROWS), np.float32), -1)
_NEG_BIG = -3.0e38


def _lane_argmax(x, lane):
    top = jnp.max(x, axis=1, keepdims=True)
    idx = jnp.min(jnp.where(x == top, lane, LANES), axis=1, keepdims=True)
    return top, idx


def _route_kernel(lg_ref, tri_ref, o_ref, cnt_ref, run_scr):
    @pl.when(pl.program_id(0) == 0)
    def _():
        run_scr[...] = jnp.zeros_like(run_scr)

    lg = lg_ref[...]
    lane = lax.broadcasted_iota(jnp.int32, lg.shape, 1)
    is_grp = lane < N_GROUPS
    gtop, gidx = _lane_argmax(jnp.where(is_grp, lg, _NEG_BIG), lane)
    grp_w = 1.0 / jnp.sum(jnp.where(is_grp, jnp.exp(lg - gtop), 0.0), axis=1, keepdims=True)
    lo = N_GROUPS + EXP_PER_GROUP * gidx
    x1 = jnp.where((lane >= lo) & (lane < lo + EXP_PER_GROUP), lg, _NEG_BIG)
    t1, i1 = _lane_argmax(x1, lane)
    t2, i2 = _lane_argmax(jnp.where(lane == i1, _NEG_BIG, x1), lane)
    r = jnp.exp(t2 - t1)
    w1 = grp_w / (1.0 + r)
    w2 = w1 * r
    e1 = i1 - N_GROUPS
    e2 = i2 - N_GROUPS
    hot1 = lane == e1
    hot2 = lane == e2
    hot = jnp.where(hot1 | hot2, 1.0, 0.0)
    before = run_scr[...] + jnp.dot(tri_ref[...], hot.astype(BF16), preferred_element_type=F32)
    rank1 = jnp.sum(jnp.where(hot1, before, 0.0), axis=1, keepdims=True)
    rank2 = jnp.sum(jnp.where(hot2, before, 0.0), axis=1, keepdims=True)
    run_scr[...] = run_scr[...] + jnp.sum(hot, axis=0, keepdims=True)
    cnt_ref[...] = jnp.broadcast_to(run_scr[...], cnt_ref.shape)
    out = jnp.where(lane == 0, e1.astype(F32), jnp.where(lane == 1, e2.astype(F32), 0.0))
    out = jnp.where(lane == 2, w1, jnp.where(lane == 3, w2, out))
    o_ref[...] = jnp.where(lane == 4, rank1, jnp.where(lane == 5, rank2, out))


def route(logits):
    t = logits.shape[0]
    return pl.pallas_call(
        _route_kernel,
        out_shape=(jax.ShapeDtypeStruct((t, LANES), F32), jax.ShapeDtypeStruct((8, LANES), F32)),
        grid=(t // ROUTE_ROWS,),
        in_specs=[pl.BlockSpec((ROUTE_ROWS, LANES), lambda i: (i, 0)),
                  pl.BlockSpec((ROUTE_ROWS, ROUTE_ROWS), lambda i: (0, 0))],
        out_specs=(pl.BlockSpec((ROUTE_ROWS, LANES), lambda i: (i, 0)),
                   pl.BlockSpec((8, LANES), lambda i: (0, 0))),
        scratch_shapes=[pltpu.VMEM((1, LANES), F32)],
        compiler_params=_cparams(("arbitrary",)),
    )(logits, jnp.asarray(_STRICT_LOWER, BF16))


def _row_copy(h_hbm, xbuf, sem, slot, row, tok):
    return pltpu.make_async_copy(h_hbm.at[pl.ds(tok, 1), :], xbuf.at[slot, pl.ds(row, 1), :], sem.at[slot])


def _expert_kernel(be_ref, cur_ref, nxt_ref, h_hbm, wg_ref, wu_ref, wd_ref, o_ref,
                   wg_s, wu_s, wd_s, xbuf, sem):
    i = pl.program_id(0)
    last = pl.num_programs(0) - 1
    slot = i % 2

    def start_block(idx_ref, s):
        for r in range(MOE_ROWS):
            _row_copy(h_hbm, xbuf, sem, s, r, idx_ref[0, 0, r]).start(priority=r % 2)

    def wait_block(s):
        for r in range(MOE_ROWS):
            _row_copy(h_hbm, xbuf, sem, s, r, 0).wait()

    @pl.when(i == 0)
    def _():
        start_block(cur_ref, 0)

    start_block(nxt_ref, 1 - slot)
    wait_block(slot)

    @pl.when((i == 0) | (be_ref[i] != be_ref[jnp.maximum(i - 1, 0)]))
    def _():
        wg_s[...] = wg_ref[0, 0].astype(BF16)
        wu_s[...] = wu_ref[0, 0].astype(BF16)
        wd_s[...] = wd_ref[0, 0].astype(BF16)

    x = xbuf[slot].astype(BF16)
    gte = jnp.dot(x, wg_s[...], preferred_element_type=F32)
    up = jnp.dot(x, wu_s[...], preferred_element_type=F32)
    hid = (gte * _sigmoid(gte) * up).astype(BF16)
    o_ref[...] = jnp.dot(hid, wd_s[...], preferred_element_type=F32).astype(o_ref.dtype)

    @pl.when(i == last)
    def _():
        wait_block(1 - slot)


def expert_ffn(h_all, src_tok, block_expert, wg, wu, wd, layer):
    d = h_all.shape[1]
    de = wg.shape[3]
    n_blocks = block_expert.shape[0]
    src3 = src_tok.reshape(n_blocks, 1, MOE_ROWS)
    return pl.pallas_call(
        _expert_kernel,
        out_shape=jax.ShapeDtypeStruct((n_blocks * MOE_ROWS, d), BF16),
        grid_spec=pltpu.PrefetchScalarGridSpec(
            num_scalar_prefetch=1,
            grid=(n_blocks,),
            in_specs=[
                pl.BlockSpec((1, 1, MOE_ROWS), lambda i, be: (i, 0, 0), memory_space=pltpu.SMEM),
                pl.BlockSpec((1, 1, MOE_ROWS), lambda i, be: (jnp.minimum(i + 1, n_blocks - 1), 0, 0),
                             memory_space=pltpu.SMEM),
                pl.BlockSpec(memory_space=pl.ANY),
                pl.BlockSpec((1, 1, d, de), lambda i, be: (layer, be[i], 0, 0)),
                pl.BlockSpec((1, 1, d, de), lambda i, be: (layer, be[i], 0, 0)),
                pl.BlockSpec((1, 1, de, d), lambda i, be: (layer, be[i], 0, 0)),
            ],
            out_specs=pl.BlockSpec((MOE_ROWS, d), lambda i, be: (i, 0)),
            scratch_shapes=[pltpu.VMEM((d, de), BF16), pltpu.VMEM((d, de), BF16), pltpu.VMEM((de, d), BF16),
                            pltpu.VMEM((2, MOE_ROWS, d), F32), pltpu.SemaphoreType.DMA((2,))],
        ),
        compiler_params=_cparams(("arbitrary",)),
    )(block_expert, src3, src3, h_all, wg, wu, wd)


def _combine_kernel(x_ref, r1_ref, r2_ref, info_ref, gate_ref, *rest, final):
    o_ref = rest[-1]
    info = info_ref[...]
    y = info[:, 2:3] * r1_ref[...].astype(F32) + info[:, 3:4] * r2_ref[...].astype(F32)
    x = x_ref[...] + gate_ref[0] * y
    if final:
        ms = jnp.mean(x * x, axis=-1, keepdims=True)
        x = x * lax.rsqrt(ms + EPS) * rest[0][...]
    o_ref[...] = x


def combine(x2d, r1, r2, info, row_off, gate, rows_per_batch, final_g=None):
    t, d = x2d.shape
    tm = min(ROUTE_ROWS, rows_per_batch)
    per = rows_per_batch // tm
    off = row_off // tm
    row = lambda i: (i, 0)
    shifted = lambda i: (i + off, 0)
    specs = [pl.BlockSpec((tm, d), row), pl.BlockSpec((tm, d), shifted), pl.BlockSpec((tm, d), shifted),
             pl.BlockSpec((tm, LANES), shifted), pl.BlockSpec((1, 1, d), lambda i: (i // per, 0, 0))]
    args = [x2d, r1, r2, info, gate]
    if final_g is not None:
        specs.append(pl.BlockSpec((1, d), lambda i: (0, 0)))
        args.append(final_g.reshape(1, d))
    return pl.pallas_call(
        functools.partial(_combine_kernel, final=final_g is not None),
        out_shape=jax.ShapeDtypeStruct((t, d), F32), grid=(t // tm,),
        in_specs=specs, out_specs=pl.BlockSpec((tm, d), row),
        compiler_params=_cparams(("parallel",)),
    )(*args)


def _dispatch(info, counts_row):
    n_tok = info.shape[0]
    expert = info[:, 0:TOP_K].astype(jnp.int32)
    rank = info[:, 4:4 + TOP_K].astype(jnp.int32)
    counts = counts_row[0, :N_EXPERTS].astype(jnp.int32)
    padded = (counts + MOE_ROWS - 1) // MOE_ROWS * MOE_ROWS
    pad_end = jnp.cumsum(padded)
    pad_start = pad_end - padded
    hot = expert[:, :, None] == jnp.arange(N_EXPERTS, dtype=jnp.int32)
    dest = jnp.sum(jnp.where(hot, pad_start, 0), axis=-1) + rank
    n_blocks = -(-(n_tok * TOP_K + N_EXPERTS * (MOE_ROWS - 1)) // MOE_ROWS)
    token = jnp.broadcast_to(jnp.arange(n_tok, dtype=jnp.int32)[:, None], dest.shape)
    src_tok = jnp.zeros((n_blocks * MOE_ROWS,), jnp.int32).at[dest.reshape(-1)].set(
        token.reshape(-1), unique_indices=True, mode="promise_in_bounds")
    block_start = jnp.arange(n_blocks, dtype=jnp.int32)[:, None] * MOE_ROWS
    block_expert = jnp.minimum(jnp.sum((pad_end[None, :] <= block_start).astype(jnp.int32), axis=1),
                               N_EXPERTS - 1)
    return src_tok, dest, block_expert


def _rope_tables(seq):
    n_rows = seq // GRID_W
    row = jnp.repeat(jnp.arange(n_rows), GRID_W).astype(F32)
    col = jnp.tile(jnp.arange(GRID_W), n_rows).astype(F32)
    axis_pairs = HEAD_DIM // 4
    inv = ROPE_THETA ** (-jnp.arange(axis_pairs, dtype=F32) / axis_pairs)
    ang = jnp.concatenate([row[:, None] * inv, col[:, None] * inv], axis=-1)
    cos, sin = jnp.cos(ang), jnp.sin(ang)
    cos128 = jnp.concatenate([cos, cos, cos, cos], axis=-1)
    sins128 = jnp.concatenate([-sin, sin, -sin, sin], axis=-1)
    return cos128, sins128


def kernel(x, c, ctx, c_ctx, w_mod, b_mod, norm1_g, norm2_g, w_in, hgrn_lb_logits, hgrn_out_norm_g,
           attn_q_norm_g, attn_k_norm_g, swa_sink, w_branch_a, w_branch_b, w_branch_c, w_out,
           w_group, w_router, w_exp_gate, w_exp_up, w_exp_down, final_norm_g):
    bsz, seq, d = x.shape
    n_ctx = ctx.shape[1]
    depth = w_mod.shape[0]
    cos, sins = _rope_tables(seq)
    lb_p = jax.nn.softmax(hgrn_lb_logits.astype(F32), axis=0)
    lower_bounds = jnp.cumsum(lb_p, axis=0) - lb_p[0]
    no_sink = jnp.zeros((N_HEADS,), F32)
    ones128 = jnp.ones((1, LANES), F32)

    xl = x.reshape(bsz * seq, d)
    xc = ctx.reshape(bsz * n_ctx, d)
    for layer in range(depth):
        ctx_out = layer < depth - 1
        mod_l = jax.nn.silu(c) @ w_mod[layer] + b_mod[layer]
        mod_c = jnp.broadcast_to(jax.nn.silu(c_ctx) @ w_mod[layer] + b_mod[layer], (bsz, 6 * d))
        ml = [m.reshape(bsz, 1, d) for m in jnp.split(mod_l, 6, axis=-1)]
        mc = [m.reshape(bsz, 1, d) for m in jnp.split(mod_c, 6, axis=-1)]

        w_in_b = w_in[layer].astype(BF16)
        p_l = inproj(xl, seq, norm1_g[layer], ml[1], ml[0], w_in_b, 1024, 3584)
        p_c = inproj(xc, n_ctx, norm1_g[layer], mc[1], mc[0], w_in_b, 256, 1024)
        p_l3 = p_l.reshape(bsz, seq, D_IN)
        p_c3 = p_c.reshape(bsz, n_ctx, D_IN)

        s0 = jnp.zeros((bsz, 2, HG_HEADS, HG_KDIM, HG_KDIM), F32)
        o_c, s_c = hgrn_scan(p_c3, lower_bounds[layer], s0, 256)
        o_l, _ = hgrn_scan(p_l3, lower_bounds[layer], s_c, 512)

        gq = jnp.tile(attn_q_norm_g[layer], 2).reshape(1, LANES)
        gk = jnp.tile(attn_k_norm_g[layer], 2).reshape(1, LANES)
        kt_l, v_l, kn_l = prep_kv(p_l3, COL["ak"], COL["av"], gk, cos, sins, True, True, 512)
        kt_c, v_c, kn_c = prep_kv(p_c3, COL["ak"], COL["av"], gk, cos, sins, True, False, 256)
        kmax = _key_norm_max([kn_l, kn_c])
        q_l, u_l = prep_q(p_l3, COL["aq"], gq, cos, sins, kmax, True, True, 512)
        kt_all = jnp.concatenate([kt_c, kt_l], axis=3)
        v_all = jnp.concatenate([v_c, v_l], axis=2)
        b_l = attention(q_l, kt_all, v_all, jnp.max(u_l), no_sink, 512, 1280)

        skt_l, sv_l, skn_l = prep_kv(p_l3, COL["sk"], COL["sv"], ones128, cos, sins, False, True, 512)
        skt_c, sv_c, skn_c = prep_kv(p_c3, COL["sk"], COL["sv"], ones128, cos, sins, False, False, 256)
        skmax = _key_norm_max([skn_l, skn_c])
        sq_l, su_l = prep_q(p_l3, COL["sq"], ones128, cos, sins, skmax, False, True, 512)
        zk = jnp.zeros((bsz, N_KV, LANES, WINDOW), BF16)
        zv = jnp.zeros((bsz, N_KV, WINDOW, LANES), BF16)
        skt_all = jnp.concatenate([skt_c, zk, skt_l, zk], axis=3)
        sv_all = jnp.concatenate([sv_c, zv, sv_l, zv], axis=2)
        c_l = attention(sq_l, skt_all, sv_all, jnp.max(su_l), swa_sink[layer], 256, 256,
                        use_sink=True, band=n_ctx)

        wa = w_branch_a[layer].astype(BF16)
        wb = w_branch_b[layer].astype(BF16)
        wc = w_branch_c[layer].astype(BF16)
        wo = w_out[layer].astype(BF16)
        w_r = jnp.concatenate(
            [w_group[layer], w_router[layer], jnp.zeros((d, LANES - N_GROUPS - N_EXPERTS), F32)], axis=1)
        w_r_hi = w_r.astype(BF16)
        w_r_lo = (w_r - w_r_hi.astype(F32)).astype(BF16)
        w_route = jnp.concatenate([w_r_hi, w_r_hi, w_r_lo], axis=0)
        xl, h_l, lg_l = merge(o_l.reshape(2, bsz * seq, HG_W), p_l, b_l.reshape(-1, ATT_W),
                              c_l.reshape(-1, ATT_W), xl, seq, wa, wb, wc, wo, hgrn_out_norm_g[layer],
                              ml[2], norm2_g[layer], ml[4], ml[3], w_route, 256)
        if ctx_out:
            q_c, u_c = prep_q(p_c3, COL["aq"], gq, cos, sins, kmax, True, False, 256)
            b_c = attention(q_c, kt_c, v_c, jnp.max(u_c), no_sink, 256, 256)
            sq_c, su_c = prep_q(p_c3, COL["sq"], ones128, cos, sins, skmax, False, False, 256)
            c_c = attention(sq_c, skt_c, sv_c, jnp.max(su_c), swa_sink[layer], 256, 256, use_sink=True)
            xc, h_c, lg_c = merge(o_c.reshape(2, bsz * n_ctx, HG_W), p_c, b_c.reshape(-1, ATT_W),
                                  c_c.reshape(-1, ATT_W), xc, n_ctx, wa, wb, wc, wo, hgrn_out_norm_g[layer],
                                  mc[2], norm2_g[layer], mc[4], mc[3], w_route, 256)
            h_all = jnp.concatenate([h_c, h_l], axis=0)
            lg_all = jnp.concatenate([lg_c, lg_l], axis=0)
        else:
            h_all, lg_all = h_l, lg_l

        info, counts_row = route(lg_all)
        src_tok, dest, block_expert = _dispatch(info, counts_row)
        out_rows = expert_ffn(h_all, src_tok, block_expert, w_exp_gate, w_exp_up, w_exp_down, layer)
        r1 = jnp.take(out_rows, dest[:, 0], axis=0, mode="clip")
        r2 = jnp.take(out_rows, dest[:, 1], axis=0, mode="clip")
        n_c = bsz * n_ctx if ctx_out else 0
        if ctx_out:
            xc = combine(xc, r1, r2, info, 0, mc[5], n_ctx)
        xl = combine(xl, r1, r2, info, n_c, ml[5], seq, final_g=final_norm_g if layer == depth - 1 else None)
    return xl.reshape(bsz, seq, d)
```

```python
import functools
import math

import numpy as np
import jax
import jax.numpy as jnp
from jax import lax
from jax.experimental import pallas as pl
from jax.experimental.pallas import tpu as pltpu

F32 = jnp.float32
BF16 = jnp.bfloat16

EPS = 1e-6
MASK_VALUE = -1e30
TINY = 1e-30
GRID_W = 64
ROPE_THETA = 10000.0

HG_HEADS = 4
HG_KDIM = 128
HG_W = HG_HEADS * HG_KDIM
HEAD_DIM = 64
N_HEADS = 8
N_KV = 2
GROUP = N_HEADS // N_KV
ATT_W = N_HEADS * HEAD_DIM
KV_W = N_KV * HEAD_DIM
WINDOW = 128
ATTN_SCALE = HEAD_DIM ** -0.5
N_GROUPS = 4
EXP_PER_GROUP = 8
N_EXPERTS = N_GROUPS * EXP_PER_GROUP
TOP_K = 2

LANES = 128
HG_CHUNK = 128
HG_LEVELS = int(math.log2(HG_CHUNK))
HG_TOTAL_ROWS = 16
MOE_ROWS = 256
ROUTE_ROWS = 512
VMEM_LIMIT = 56 * 1024 * 1024

LOG2E = 1.4426950408889634
SHIFT_HEADROOM = 57.0
SAFE_BOUND = 90.0


def _cparams(sem):
    return pltpu.CompilerParams(dimension_semantics=sem, vmem_limit_bytes=VMEM_LIMIT)


_SIZES = (HG_W, HG_W, HG_W, HG_W, HG_W, ATT_W, KV_W, KV_W, ATT_W, KV_W, KV_W, 3 * 1024)
_NAMES = ("hq", "ff", "fb", "hi", "hg", "aq", "ak", "av", "sq", "sk", "sv", "gates")
COL = {n: int(sum(_SIZES[:i])) for i, n in enumerate(_NAMES)}
D_IN = int(sum(_SIZES))


def _inproj_kernel(x_ref, g_ref, sc_ref, sh_ref, w_ref, o_ref, h_scr):
    @pl.when(pl.program_id(1) == 0)
    def _():
        x = x_ref[...]
        ms = jnp.mean(x * x, axis=-1, keepdims=True)
        y = x * lax.rsqrt(ms + EPS) * g_ref[...]
        h_scr[...] = (y * (1.0 + sc_ref[0]) + sh_ref[0]).astype(BF16)

    o_ref[...] = jnp.dot(h_scr[...], w_ref[...], preferred_element_type=F32).astype(o_ref.dtype)


def inproj(x2d, rows_per_batch, g, scale, shift, w_bf16, tm, tn):
    t, d = x2d.shape
    n = w_bf16.shape[1]
    tm = min(tm, rows_per_batch)
    per = rows_per_batch // tm
    return pl.pallas_call(
        _inproj_kernel,
        out_shape=jax.ShapeDtypeStruct((t, n), BF16),
        grid=(t // tm, n // tn),
        in_specs=[
            pl.BlockSpec((tm, d), lambda i, j: (i, 0)),
            pl.BlockSpec((1, d), lambda i, j: (0, 0)),
            pl.BlockSpec((1, 1, d), lambda i, j: (i // per, 0, 0)),
            pl.BlockSpec((1, 1, d), lambda i, j: (i // per, 0, 0)),
            pl.BlockSpec((d, tn), lambda i, j: (0, j)),
        ],
        out_specs=pl.BlockSpec((tm, tn), lambda i, j: (i, j)),
        scratch_shapes=[pltpu.VMEM((tm, d), BF16)],
        compiler_params=_cparams(("parallel", "arbitrary")),
    )(x2d, g.reshape(1, d), scale, shift, w_bf16)


def _hgrn_exponent_matrices():
    c, nl = HG_CHUNK, HG_LEVELS
    rows = (nl + 2) * c + HG_TOTAL_ROWS
    out = np.zeros((2, rows, c), np.float32)
    for d in range(2):
        pos = np.arange(c) if d == 0 else c - 1 - np.arange(c)
        for l in range(nl):
            m = 1 << l
            for t in range(c):
                p = pos[t]
                mid = (p // (2 * m)) * 2 * m + m
                if p >= mid:
                    sel = (pos >= mid) & (pos <= p)
                else:
                    sel = (pos > p) & (pos <= mid - 1)
                out[d, l * c + t, sel] = 1.0
        for t in range(c):
            out[d, nl * c + t, pos <= pos[t]] = 1.0
            out[d, (nl + 1) * c + t, pos > pos[t]] = 1.0
        out[d, (nl + 2) * c:, :] = 1.0
    return np.concatenate([out, out], axis=2)


_HG_EXP_MATS = _hgrn_exponent_matrices()


def _sigmoid(x):
    return 1.0 / (1.0 + jnp.exp(-x))


def _hgrn_kernel(q_ref, f_ref, i_ref, lb_ref, mat_ref, s0_ref, o_ref, st_ref, *, n_chunks):
    c, nl = HG_CHUNK, HG_LEVELS
    d = pl.program_id(1)

    @pl.when(pl.program_id(2) == 0)
    def _():
        st_ref[...] = s0_ref[...]

    row = lax.broadcasted_iota(jnp.int32, (c, HG_W), 0)
    pos = row + d * (c - 1 - 2 * row)
    r_i = lax.broadcasted_iota(jnp.int32, (c, c), 0)
    c_i = lax.broadcasted_iota(jnp.int32, (c, c), 1)
    xor_rc = r_i ^ c_i
    top_bit = sum((xor_rc >= (1 << j)).astype(jnp.int32) for j in range(1, nl))
    pair_level = jnp.where((r_i - c_i) * (1 - 2 * d) > 0, top_bit, -1)

    def chunk(ci, carry):
        cc = ci + d * (n_chunks - 1 - 2 * ci)
        r0 = pl.multiple_of(cc * c, c)
        qraw = q_ref[0, pl.ds(r0, c), :].astype(F32)
        fz = f_ref[0, pl.ds(r0, c), :].astype(F32)
        vb = i_ref[0, pl.ds(r0, c), :]
        v = vb.astype(F32)
        lb = lb_ref[...]
        sig = _sigmoid(fz)
        logf = jnp.log(jnp.maximum(lb + (1.0 - lb) * sig, TINY))
        key = (1.0 - lb) * (1.0 - sig)
        qh = qraw * _sigmoid(qraw) * (HG_KDIM ** -0.5)
        logf2 = logf * LOG2E
        hi = logf2.astype(BF16)
        lo = (logf2 - hi.astype(F32)).astype(BF16)
        hilo = jnp.concatenate([hi, lo], axis=0)

        dec_all = jnp.exp2(jnp.dot(mat_ref[0], hilo, preferred_element_type=F32))

        def decay(r_lo, r_hi):
            return dec_all[r_lo:r_hi]

        a = [jnp.zeros((c, c), F32)] * HG_HEADS
        for l in range(nl):
            x = (jnp.where(((pos >> l) & 1) == 1, qh, key) * decay(l * c, (l + 1) * c)).astype(BF16)
            for h in range(HG_HEADS):
                xh = x[:, h * LANES:(h + 1) * LANES]
                prod = lax.dot_general(xh, xh, (((1,), (1,)), ((), ())), preferred_element_type=F32)
                a[h] = jnp.where(pair_level == l, prod, a[h])
        q_in = (qh * decay(nl * c, (nl + 1) * c)).astype(BF16)
        k_out = (key * decay((nl + 1) * c, (nl + 2) * c)).astype(BF16)
        total = decay((nl + 2) * c, (nl + 2) * c + HG_TOTAL_ROWS)[0:1]
        qk = qh * key
        for h in range(HG_HEADS):
            lanes = slice(h * LANES, (h + 1) * LANES)
            st = st_ref[0, 0, h]
            o = (jnp.dot(a[h].astype(BF16), vb[:, lanes], preferred_element_type=F32)
                 + jnp.sum(qk[:, lanes], axis=1, keepdims=True) * v[:, lanes]
                 + lax.dot_general(q_in[:, lanes], st.astype(BF16), (((1,), (1,)), ((), ())),
                                   preferred_element_type=F32))
            o_ref[0, 0, pl.ds(r0, c), lanes] = o.astype(o_ref.dtype)
            st_ref[0, 0, h] = total[:, lanes] * st + lax.dot_general(
                vb[:, lanes], k_out[:, lanes], (((0,), (0,)), ((), ())), preferred_element_type=F32)
        return carry

    lax.fori_loop(0, n_chunks, chunk, 0, unroll=2 if n_chunks % 2 == 0 else 1)


def hgrn_scan(p3, lb, s0, tb):
    b, l, _ = p3.shape
    tb = min(tb, l)
    nb = l // tb
    blk = lambda d, n: n + d * (nb - 1 - 2 * n)
    wblk = HG_W
    kern = functools.partial(_hgrn_kernel, n_chunks=tb // HG_CHUNK)
    mats = jnp.asarray(_HG_EXP_MATS, BF16)
    return pl.pallas_call(
        kern,
        out_shape=(jax.ShapeDtypeStruct((2, b, l, HG_W), BF16),
                   jax.ShapeDtypeStruct((b, 2, HG_HEADS, HG_KDIM, HG_KDIM), F32)),
        grid=(b, 2, nb),
        in_specs=[
            pl.BlockSpec((1, tb, wblk), lambda bi, d, n: (bi, blk(d, n), COL["hq"] // wblk)),
            pl.BlockSpec((1, tb, wblk), lambda bi, d, n: (bi, blk(d, n), COL["ff"] // wblk + d)),
            pl.BlockSpec((1, tb, wblk), lambda bi, d, n: (bi, blk(d, n), COL["hi"] // wblk)),
            pl.BlockSpec((1, HG_W), lambda bi, d, n: (0, 0)),
            pl.BlockSpec((1,) + _HG_EXP_MATS.shape[1:], lambda bi, d, n: (d, 0, 0)),
            pl.BlockSpec((1, 1, HG_HEADS, HG_KDIM, HG_KDIM), lambda bi, d, n: (bi, d, 0, 0, 0)),
        ],
        out_specs=(
            pl.BlockSpec((1, 1, tb, HG_W), lambda bi, d, n: (d, bi, blk(d, n), 0)),
            pl.BlockSpec((1, 1, HG_HEADS, HG_KDIM, HG_KDIM), lambda bi, d, n: (bi, d, 0, 0, 0)),
        ),
        compiler_params=_cparams(("parallel", "arbitrary", "arbitrary")),
    )(p3, p3, p3, lb.reshape(1, HG_W), mats, s0)


_GROUP_ONES = np.kron(np.eye(LANES // HEAD_DIM, dtype=np.float32), np.ones((HEAD_DIM, HEAD_DIM), np.float32))
_GROUP_ONES2 = np.concatenate([_GROUP_ONES, _GROUP_ONES], axis=0)


def _group_ssq(x, ones2):
    sq = x * x
    hi = sq.astype(BF16)
    lo = (sq - hi.astype(F32)).astype(BF16)
    return jnp.dot(jnp.concatenate([hi, lo], axis=1), ones2, preferred_element_type=F32)


def _prep_tile(x, g, cos, sins, ones2, norm, rope):
    if norm:
        x = x * lax.rsqrt(_group_ssq(x, ones2) * (1.0 / HEAD_DIM) + EPS) * g
    if rope:
        lane = lax.broadcasted_iota(jnp.int32, x.shape, 1)
        first = (lane % HEAD_DIM) < (HEAD_DIM // 2)
        other = jnp.where(first, pltpu.roll(x, LANES - HEAD_DIM // 2, 1), pltpu.roll(x, HEAD_DIM // 2, 1))
        x = x * cos + other * sins
    return x


def _prep_kv_kernel(k_ref, v_ref, g_ref, cos_ref, sin_ref, ones_ref, kt_ref, vo_ref, kn_ref, *, norm, rope):
    k = _prep_tile(k_ref[0].astype(F32), g_ref[...], cos_ref[...], sin_ref[...], ones_ref[...], norm, rope)
    kb = k.astype(BF16).astype(F32)
    ssq = _group_ssq(kb, ones_ref[...])
    kn_ref[0, 0] = jnp.broadcast_to(jnp.max(ssq, axis=0, keepdims=True), (8, LANES))
    kt = kb.T
    row = lax.broadcasted_iota(jnp.int32, kt.shape, 0)
    v = v_ref[0]
    low = lax.broadcasted_iota(jnp.int32, v.shape, 1) < HEAD_DIM
    one = jnp.ones((), v.dtype)
    for g in range(N_KV):
        ktg = kt if g == 0 else jnp.concatenate([kt[HEAD_DIM:], kt[:HEAD_DIM]], axis=0)
        kt_ref[0, g] = jnp.where(row < HEAD_DIM, ktg, jnp.where(row == HEAD_DIM, 1.0, 0.0)).astype(BF16)
        vg = v if g == 0 else jnp.concatenate([v[:, HEAD_DIM:], v[:, :HEAD_DIM]], axis=1)
        vo_ref[0, g] = jnp.where(low, vg, one).astype(BF16)


def _prep_q_kernel(kmax_ref, x0_ref, x1_ref, x2_ref, x3_ref, g_ref, cos_ref, sin_ref, ones_ref, o_ref, u_ref,
                   *, norm, rope):
    b = pl.program_id(0)
    lane = lax.broadcasted_iota(jnp.int32, (x0_ref.shape[1], LANES), 1)
    umax = None
    for tile, x_ref in enumerate((x0_ref, x1_ref, x2_ref, x3_ref)):
        kv = (2 * tile) // GROUP
        x = x_ref[0].astype(F32)
        y = _prep_tile(x, g_ref[...], cos_ref[...], sin_ref[...], ones_ref[...], norm, rope) * (ATTN_SCALE * LOG2E)
        yb = y.astype(BF16).astype(F32)
        u2 = jnp.sqrt(_group_ssq(yb, ones_ref[...])) * kmax_ref[b * N_KV + kv]
        tmax = jnp.max(u2, axis=0, keepdims=True)
        umax = tmax if umax is None else jnp.maximum(umax, tmax)
        neg_shift = SHIFT_HEADROOM - u2
        y_sw = pltpu.roll(yb, HEAD_DIM, 1)
        ns_sw = pltpu.roll(neg_shift, HEAD_DIM, 1)
        for half in range(2):
            h = 2 * tile + half
            data = yb if half == 0 else y_sw
            ns = ns_sw if half == 0 else neg_shift
            o_ref[0, :, h * LANES:(h + 1) * LANES] = jnp.where(
                lane < HEAD_DIM, data, jnp.where(lane == HEAD_DIM, ns, 0.0)).astype(BF16)
    u_ref[0, 0] = jnp.broadcast_to(umax, (8, LANES))


def prep_kv(p3, col_k, col_v, g128, cos, sins, norm, rope, tq):
    b, l, _ = p3.shape
    tq = min(tq, l)
    ones2 = jnp.asarray(_GROUP_ONES2, BF16)
    return pl.pallas_call(
        functools.partial(_prep_kv_kernel, norm=norm, rope=rope),
        out_shape=(jax.ShapeDtypeStruct((b, N_KV, LANES, l), BF16),
                   jax.ShapeDtypeStruct((b, N_KV, l, LANES), BF16),
                   jax.ShapeDtypeStruct((b, l // tq, 8, LANES), F32)),
        grid=(b, l // tq),
        in_specs=[
            pl.BlockSpec((1, tq, KV_W), lambda bi, i: (bi, i, col_k // KV_W)),
            pl.BlockSpec((1, tq, KV_W), lambda bi, i: (bi, i, col_v // KV_W)),
            pl.BlockSpec((1, LANES), lambda bi, i: (0, 0)),
            pl.BlockSpec((tq, LANES), lambda bi, i: (i, 0)),
            pl.BlockSpec((tq, LANES), lambda bi, i: (i, 0)),
            pl.BlockSpec((2 * LANES, LANES), lambda bi, i: (0, 0)),
        ],
        out_specs=(pl.BlockSpec((1, N_KV, LANES, tq), lambda bi, i: (bi, 0, 0, i)),
                   pl.BlockSpec((1, N_KV, tq, LANES), lambda bi, i: (bi, 0, i, 0)),
                   pl.BlockSpec((1, 1, 8, LANES), lambda bi, i: (bi, i, 0, 0))),
        compiler_params=_cparams(("parallel", "parallel")),
    )(p3, p3, g128, cos, sins, ones2)


def prep_q(p3, col, g128, cos, sins, kmax, norm, rope, tq):
    b, l, _ = p3.shape
    tq = min(tq, l)
    ones2 = jnp.asarray(_GROUP_ONES2, BF16)
    return pl.pallas_call(
        functools.partial(_prep_q_kernel, norm=norm, rope=rope),
        out_shape=(jax.ShapeDtypeStruct((b, l, N_HEADS * LANES), BF16),
                   jax.ShapeDtypeStruct((b, l // tq, 8, LANES), F32)),
        grid_spec=pltpu.PrefetchScalarGridSpec(
            num_scalar_prefetch=1,
            grid=(b, l // tq),
            in_specs=[
                pl.BlockSpec((1, tq, LANES), lambda bi, i, s, t=t: (bi, i, col // LANES + t))
                for t in range(ATT_W // LANES)
            ] + [
                pl.BlockSpec((1, LANES), lambda bi, i, s: (0, 0)),
                pl.BlockSpec((tq, LANES), lambda bi, i, s: (i, 0)),
                pl.BlockSpec((tq, LANES), lambda bi, i, s: (i, 0)),
                pl.BlockSpec((2 * LANES, LANES), lambda bi, i, s: (0, 0)),
            ],
            out_specs=(pl.BlockSpec((1, tq, N_HEADS * LANES), lambda bi, i, s: (bi, i, 0)),
                       pl.BlockSpec((1, 1, 8, LANES), lambda bi, i, s: (bi, i, 0, 0))),
        ),
        compiler_params=_cparams(("parallel", "parallel")),
    )(kmax, p3, p3, p3, p3, g128, cos, sins, ones2)


def _key_norm_max(kn_list):
    kn = functools.reduce(jnp.maximum, [jnp.max(k, axis=(1, 2)) for k in kn_list])
    return jnp.sqrt(kn[:, ::HEAD_DIM]).reshape(-1)


def _attn_kernel(sink_ref, q_ref, kt_ref, v_ref, o_ref, *, tk, online, use_sink, band, seq):
    tq = q_ref.shape[1]
    rows = GROUP * tq
    lane = lax.broadcasted_iota(jnp.int32, (tq, LANES), 1)
    low = lane < HEAD_DIM
    if band is not None:
        width = tq + 2 * WINDOW
        start = pl.multiple_of(pl.program_id(1) * tq, tq)
        r = lax.broadcasted_iota(jnp.int32, (rows, width), 0) % tq
        kp = lax.broadcasted_iota(jnp.int32, (rows, width), 1)
        key_pos = start + kp - WINDOW
        mask = (jnp.abs(kp - WINDOW - r) <= WINDOW) & (key_pos >= 0) & (key_pos < seq)
    for g in range(N_KV):
        tiles = [q_ref[0, :, (g * GROUP + hh) * LANES:(g * GROUP + hh + 1) * LANES] for hh in range(GROUP)]
        qg = jnp.concatenate(tiles, axis=0)

        def scores(kt, qg=qg):
            return jnp.dot(qg, kt, preferred_element_type=F32)

        def update(carry, s, v, msk=None):
            if msk is not None:
                s = jnp.where(msk, s, MASK_VALUE)
            if online:
                m, acc = carry
                m_new = jnp.maximum(m, jnp.max(s, axis=1, keepdims=True))
                p = jnp.exp2(s - m_new).astype(BF16)
                return m_new, jnp.exp2(m - m_new) * acc + jnp.dot(p, v, preferred_element_type=F32)
            return carry + jnp.dot(jnp.exp2(s).astype(BF16), v, preferred_element_type=F32)

        acc0 = jnp.zeros((rows, LANES), F32)
        carry = (jnp.full((rows, 1), MASK_VALUE, F32), acc0) if online else acc0
        if band is None:
            def chunk(c, k0, g=g, scores=scores, update=update):
                return update(c, scores(kt_ref[0, g, :, pl.ds(k0, tk)]), v_ref[0, g, pl.ds(k0, tk), :])

            nk = kt_ref.shape[3] // tk
            for j in range(nk % 2):
                carry = chunk(carry, j * tk)

            def body(j, c, chunk=chunk, first=(nk % 2) * tk):
                k0 = pl.multiple_of(first + j * (2 * tk), tk)
                return chunk(chunk(c, k0), pl.multiple_of(k0 + tk, tk))

            carry = lax.fori_loop(0, nk // 2, body, carry)
        else:
            carry = update(carry, scores(kt_ref[0, g, :, 0:band]), v_ref[0, g, 0:band, :])
            b0 = pl.multiple_of(band + start, LANES)
            carry = update(carry, scores(kt_ref[0, g, :, pl.ds(b0, width)]), v_ref[0, g, pl.ds(b0, width), :], mask)
        acc = carry[1] if online else carry
        if use_sink:
            e = jnp.concatenate([t[:, HEAD_DIM:HEAD_DIM + 1].astype(F32) + sink_ref[g * GROUP + hh] * LOG2E
                                 for hh, t in enumerate(tiles)], axis=0)
            if online:
                e = e - carry[0]
            lane_r = lax.broadcasted_iota(jnp.int32, (rows, LANES), 1)
            acc = acc + jnp.where(lane_r >= HEAD_DIM, jnp.exp2(e), 0.0)
        o = acc * pltpu.roll(1.0 / acc, HEAD_DIM, 1)
        for pair in range(GROUP // 2):
            a = o[(2 * pair) * tq:(2 * pair + 1) * tq]
            b = o[(2 * pair + 1) * tq:(2 * pair + 2) * tq]
            t0 = (g * GROUP // 2 + pair) * LANES
            o_ref[0, :, t0:t0 + LANES] = jnp.where(low, a, pltpu.roll(b, HEAD_DIM, 1)).astype(o_ref.dtype)


def _attention(q, kt, v, sink, tq, tk, online, use_sink, band):
    b, l, _ = q.shape
    sk = kt.shape[3]
    tq = min(tq, l)
    tk = min(tk, sk)
    kern = functools.partial(_attn_kernel, tk=tk, online=online, use_sink=use_sink, band=band, seq=l)
    return pl.pallas_call(
        kern,
        out_shape=jax.ShapeDtypeStruct((b, l, ATT_W), BF16),
        grid_spec=pltpu.PrefetchScalarGridSpec(
            num_scalar_prefetch=1,
            grid=(b, l // tq),
            in_specs=[
                pl.BlockSpec((1, tq, N_HEADS * LANES), lambda bi, i, s: (bi, i, 0)),
                pl.BlockSpec((1, N_KV, LANES, sk), lambda bi, i, s: (bi, 0, 0, 0), pipeline_mode=pl.Buffered(1)),
                pl.BlockSpec((1, N_KV, sk, LANES), lambda bi, i, s: (bi, 0, 0, 0), pipeline_mode=pl.Buffered(1)),
            ],
            out_specs=pl.BlockSpec((1, tq, ATT_W), lambda bi, i, s: (bi, i, 0)),
        ),
        compiler_params=_cparams(("parallel", "arbitrary")),
    )(sink, q, kt, v)


def attention(q, kt, v, ubound, sink, tq, tk, use_sink=False, band=None):
    return lax.cond(ubound <= SAFE_BOUND,
                    lambda: _attention(q, kt, v, sink, tq, tk, False, use_sink, band),
                    lambda: _attention(q, kt, v, sink, tq, tk, True, use_sink, band))


def _merge_kernel(of_ref, ob_ref, hg_ref, b_ref, c_ref, ga_ref, gb_ref, gc_ref, x_ref,
                  wa_ref, wb_ref, wc_ref, wo_ref, hgn_ref, gate1_ref, n2_ref, sc2_ref, sh2_ref, wr_ref,
                  xo_ref, h2_ref, lg_ref):
    o = of_ref[0].astype(F32) + ob_ref[0].astype(F32)
    tiles = []
    for h in range(HG_HEADS):
        t = o[:, h * LANES:(h + 1) * LANES]
        ms = jnp.mean(t * t, axis=-1, keepdims=True)
        tiles.append(t * lax.rsqrt(ms + EPS) * hgn_ref[...])
    hg = hg_ref[...].astype(F32)
    a = (jnp.concatenate(tiles, axis=1) * (hg * _sigmoid(hg))).astype(BF16)
    merged = (_sigmoid(ga_ref[...].astype(F32)) * jnp.dot(a, wa_ref[...], preferred_element_type=F32)
              + _sigmoid(gb_ref[...].astype(F32)) * jnp.dot(b_ref[...], wb_ref[...], preferred_element_type=F32)
              + _sigmoid(gc_ref[...].astype(F32)) * jnp.dot(c_ref[...], wc_ref[...], preferred_element_type=F32))
    y = jnp.dot(merged.astype(BF16), wo_ref[...], preferred_element_type=F32)
    x = x_ref[...] + gate1_ref[0] * y
    xo_ref[...] = x
    ms = jnp.mean(x * x, axis=-1, keepdims=True)
    h2 = (x * lax.rsqrt(ms + EPS) * n2_ref[...]) * (1.0 + sc2_ref[0]) + sh2_ref[0]
    hi = h2.astype(BF16)
    lo = (h2 - hi.astype(F32)).astype(BF16)
    h2_ref[...] = hi
    lg_ref[...] = jnp.dot(jnp.concatenate([hi, lo, hi], axis=1), wr_ref[...], preferred_element_type=F32)


def merge(o_fb, p2d, b2d, c2d, x2d, rows_per_batch, wa, wb, wc, wo, hgn, gate1, n2, sc2, sh2, w_route, tm):
    t, d = x2d.shape
    tm = min(tm, rows_per_batch)
    per = rows_per_batch // tm
    row = lambda i: (i, 0)
    const = lambda i: (0, 0)
    bat = lambda i: (i // per, 0, 0)
    gcol = COL["gates"] // d
    return pl.pallas_call(
        _merge_kernel,
        out_shape=(jax.ShapeDtypeStruct((t, d), F32), jax.ShapeDtypeStruct((t, d), BF16),
                   jax.ShapeDtypeStruct((t, LANES), F32)),
        grid=(t // tm,),
        in_specs=[
            pl.BlockSpec((1, tm, HG_W), lambda i: (0, i, 0)),
            pl.BlockSpec((1, tm, HG_W), lambda i: (1, i, 0)),
            pl.BlockSpec((tm, HG_W), lambda i: (i, COL["hg"] // HG_W)),
            pl.BlockSpec((tm, ATT_W), row),
            pl.BlockSpec((tm, ATT_W), row),
            pl.BlockSpec((tm, d), lambda i: (i, gcol)),
            pl.BlockSpec((tm, d), lambda i: (i, gcol + 1)),
            pl.BlockSpec((tm, d), lambda i: (i, gcol + 2)),
            pl.BlockSpec((tm, d), row),
            pl.BlockSpec((HG_W, d), const),
            pl.BlockSpec((ATT_W, d), const),
            pl.BlockSpec((ATT_W, d), const),
            pl.BlockSpec((d, d), const),
            pl.BlockSpec((1, LANES), const),
            pl.BlockSpec((1, 1, d), bat),
            pl.BlockSpec((1, d), const),
            pl.BlockSpec((1, 1, d), bat),
            pl.BlockSpec((1, 1, d), bat),
            pl.BlockSpec((3 * d, LANES), const),
        ],
        out_specs=(pl.BlockSpec((tm, d), row), pl.BlockSpec((tm, d), row), pl.BlockSpec((tm, LANES), row)),
        compiler_params=_cparams(("parallel",)),
    )(o_fb, o_fb, p2d, b2d, c2d, p2d, p2d, p2d, x2d, wa, wb, wc, wo, hgn.reshape(1, LANES),
      gate1, n2.reshape(1, d), sc2, sh2, w_route)


_STRICT_LOWER = np.tril(np.ones((ROUTE_ROWS, ROUTE_ROWS), np.float32), -1)
_NEG_BIG = -3.0e38


def _lane_argmax(x, lane):
    top = jnp.max(x, axis=1, keepdims=True)
    idx = jnp.min(jnp.where(x == top, lane, LANES), axis=1, keepdims=True)
    return top, idx


def _route_kernel(lg_ref, tri_ref, o_ref, cnt_ref, run_scr):
    @pl.when(pl.program_id(0) == 0)
    def _():
        run_scr[...] = jnp.zeros_like(run_scr)

    lg = lg_ref[...]
    lane = lax.broadcasted_iota(jnp.int32, lg.shape, 1)
    is_grp = lane < N_GROUPS
    gtop, gidx = _lane_argmax(jnp.where(is_grp, lg, _NEG_BIG), lane)
    grp_w = 1.0 / jnp.sum(jnp.where(is_grp, jnp.exp(lg - gtop), 0.0), axis=1, keepdims=True)
    lo = N_GROUPS + EXP_PER_GROUP * gidx
    x1 = jnp.where((lane >= lo) & (lane < lo + EXP_PER_GROUP), lg, _NEG_BIG)
    t1, i1 = _lane_argmax(x1, lane)
    t2, i2 = _lane_argmax(jnp.where(lane == i1, _NEG_BIG, x1), lane)
    r = jnp.exp(t2 - t1)
    w1 = grp_w / (1.0 + r)
    w2 = w1 * r
    e1 = i1 - N_GROUPS
    e2 = i2 - N_GROUPS
    hot1 = lane == e1
    hot2 = lane == e2
    hot = jnp.where(hot1 | hot2, 1.0, 0.0)
    before = run_scr[...] + jnp.dot(tri_ref[...], hot.astype(BF16), preferred_element_type=F32)
    rank1 = jnp.sum(jnp.where(hot1, before, 0.0), axis=1, keepdims=True)
    rank2 = jnp.sum(jnp.where(hot2, before, 0.0), axis=1, keepdims=True)
    run_scr[...] = run_scr[...] + jnp.sum(hot, axis=0, keepdims=True)
    cnt_ref[...] = jnp.broadcast_to(run_scr[...], cnt_ref.shape)
    out = jnp.where(lane == 0, e1.astype(F32), jnp.where(lane == 1, e2.astype(F32), 0.0))
    out = jnp.where(lane == 2, w1, jnp.where(lane == 3, w2, out))
    o_ref[...] = jnp.where(lane == 4, rank1, jnp.where(lane == 5, rank2, out))


def route(logits):
    t = logits.shape[0]
    return pl.pallas_call(
        _route_kernel,
        out_shape=(jax.ShapeDtypeStruct((t, LANES), F32), jax.ShapeDtypeStruct((8, LANES), F32)),
        grid=(t // ROUTE_ROWS,),
        in_specs=[pl.BlockSpec((ROUTE_ROWS, LANES), lambda i: (i, 0)),
                  pl.BlockSpec((ROUTE_ROWS, ROUTE_ROWS), lambda i: (0, 0))],
        out_specs=(pl.BlockSpec((ROUTE_ROWS, LANES), lambda i: (i, 0)),
                   pl.BlockSpec((8, LANES), lambda i: (0, 0))),
        scratch_shapes=[pltpu.VMEM((1, LANES), F32)],
        compiler_params=_cparams(("arbitrary",)),
    )(logits, jnp.asarray(_STRICT_LOWER, BF16))


def _expert_kernel(be_ref, x_ref, wg_ref, wu_ref, wd_ref, *rest):
    o_ref, wg_s, wu_s, wd_s = rest[-4:]
    i = pl.program_id(0)

    @pl.when((i == 0) | (be_ref[i] != be_ref[jnp.maximum(i - 1, 0)]))
    def _():
        wg_s[...] = wg_ref[0, 0].astype(BF16)
        wu_s[...] = wu_ref[0, 0].astype(BF16)
        wd_s[...] = wd_ref[0, 0].astype(BF16)

    x = x_ref[...]
    gte = jnp.dot(x, wg_s[...], preferred_element_type=F32)
    up = jnp.dot(x, wu_s[...], preferred_element_type=F32)
    hid = (gte * _sigmoid(gte) * up).astype(BF16)
    o_ref[...] = jnp.dot(hid, wd_s[...], preferred_element_type=F32).astype(o_ref.dtype)


def expert_ffn(rows, block_expert, wg, wu, wd, layer, out_prev, block_off, n_total):
    n, d = rows.shape
    de = wg.shape[3]
    in_specs = [
        pl.BlockSpec((MOE_ROWS, d), lambda i, be: (i, 0)),
        pl.BlockSpec((1, 1, d, de), lambda i, be: (layer, be[i], 0, 0)),
        pl.BlockSpec((1, 1, d, de), lambda i, be: (layer, be[i], 0, 0)),
        pl.BlockSpec((1, 1, de, d), lambda i, be: (layer, be[i], 0, 0)),
    ]
    args = [block_expert, rows, wg, wu, wd]
    aliases = {}
    if out_prev is not None:
        in_specs.append(pl.BlockSpec(memory_space=pl.ANY))
        args.append(out_prev)
        aliases = {5: 0}
    return pl.pallas_call(
        _expert_kernel,
        out_shape=jax.ShapeDtypeStruct((n_total, d), BF16),
        grid_spec=pltpu.PrefetchScalarGridSpec(
            num_scalar_prefetch=1,
            grid=(n // MOE_ROWS,),
            in_specs=in_specs,
            out_specs=pl.BlockSpec((MOE_ROWS, d), lambda i, be: (i + block_off, 0)),
            scratch_shapes=[pltpu.VMEM((d, de), BF16), pltpu.VMEM((d, de), BF16), pltpu.VMEM((de, d), BF16)],
        ),
        input_output_aliases=aliases,
        compiler_params=_cparams(("arbitrary",)),
    )(*args)


MOE_PARTS = 4


def experts_overlapped(h_all, src_tok, block_expert, wg, wu, wd, layer):
    n_blocks = block_expert.shape[0]
    n_total = n_blocks * MOE_ROWS
    bounds = [n_blocks * k // MOE_PARTS for k in range(MOE_PARTS + 1)]
    out = None
    for b0, b1 in zip(bounds[:-1], bounds[1:]):
        rows = jnp.take(h_all, src_tok[b0 * MOE_ROWS:b1 * MOE_ROWS], axis=0, mode="clip")
        out = expert_ffn(rows, block_expert[b0:b1], wg, wu, wd, layer, out, b0, n_total)
    return out


def _combine_kernel(x_ref, r1_ref, r2_ref, info_ref, gate_ref, *rest, final):
    o_ref = rest[-1]
    info = info_ref[...]
    y = info[:, 2:3] * r1_ref[...].astype(F32) + info[:, 3:4] * r2_ref[...].astype(F32)
    x = x_ref[...] + gate_ref[0] * y
    if final:
        ms = jnp.mean(x * x, axis=-1, keepdims=True)
        x = x * lax.rsqrt(ms + EPS) * rest[0][...]
    o_ref[...] = x


def combine(x2d, r1, r2, info, row_off, gate, rows_per_batch, final_g=None):
    t, d = x2d.shape
    tm = min(ROUTE_ROWS, rows_per_batch)
    per = rows_per_batch // tm
    off = row_off // tm
    row = lambda i: (i, 0)
    shifted = lambda i: (i + off, 0)
    specs = [pl.BlockSpec((tm, d), row), pl.BlockSpec((tm, d), shifted), pl.BlockSpec((tm, d), shifted),
             pl.BlockSpec((tm, LANES), shifted), pl.BlockSpec((1, 1, d), lambda i: (i // per, 0, 0))]
    args = [x2d, r1, r2, info, gate]
    if final_g is not None:
        specs.append(pl.BlockSpec((1, d), lambda i: (0, 0)))
        args.append(final_g.reshape(1, d))
    return pl.pallas_call(
        functools.partial(_combine_kernel, final=final_g is not None),
        out_shape=jax.ShapeDtypeStruct((t, d), F32), grid=(t // tm,),
        in_specs=specs, out_specs=pl.BlockSpec((tm, d), row),
        compiler_params=_cparams(("parallel",)),
    )(*args)


def _dispatch(info, counts_row):
    n_tok = info.shape[0]
    expert = info[:, 0:TOP_K].astype(jnp.int32)
    rank = info[:, 4:4 + TOP_K].astype(jnp.int32)
    counts = counts_row[0, :N_EXPERTS].astype(jnp.int32)
    padded = (counts + MOE_ROWS - 1) // MOE_ROWS * MOE_ROWS
    pad_end = jnp.cumsum(padded)
    pad_start = pad_end - padded
    hot = expert[:, :, None] == jnp.arange(N_EXPERTS, dtype=jnp.int32)
    dest = jnp.sum(jnp.where(hot, pad_start, 0), axis=-1) + rank
    n_blocks = -(-(n_tok * TOP_K + N_EXPERTS * (MOE_ROWS - 1)) // MOE_ROWS)
    token = jnp.broadcast_to(jnp.arange(n_tok, dtype=jnp.int32)[:, None], dest.shape)
    src_tok = jnp.zeros((n_blocks * MOE_ROWS,), jnp.int32).at[dest.reshape(-1)].set(
        token.reshape(-1), unique_indices=True, mode="promise_in_bounds")
    block_start = jnp.arange(n_blocks, dtype=jnp.int32)[:, None] * MOE_ROWS
    block_expert = jnp.minimum(jnp.sum((pad_end[None, :] <= block_start).astype(jnp.int32), axis=1),
                               N_EXPERTS - 1)
    return src_tok, dest, block_expert


def _rope_tables(seq):
    n_rows = seq // GRID_W
    row = jnp.repeat(jnp.arange(n_rows), GRID_W).astype(F32)
    col = jnp.tile(jnp.arange(GRID_W), n_rows).astype(F32)
    axis_pairs = HEAD_DIM // 4
    inv = ROPE_THETA ** (-jnp.arange(axis_pairs, dtype=F32) / axis_pairs)
    ang = jnp.concatenate([row[:, None] * inv, col[:, None] * inv], axis=-1)
    cos, sin = jnp.cos(ang), jnp.sin(ang)
    cos128 = jnp.concatenate([cos, cos, cos, cos], axis=-1)
    sins128 = jnp.concatenate([-sin, sin, -sin, sin], axis=-1)
    return cos128, sins128


def kernel(x, c, ctx, c_ctx, w_mod, b_mod, norm1_g, norm2_g, w_in, hgrn_lb_logits, hgrn_out_norm_g,
           attn_q_norm_g, attn_k_norm_g, swa_sink, w_branch_a, w_branch_b, w_branch_c, w_out,
           w_group, w_router, w_exp_gate, w_exp_up, w_exp_down, final_norm_g):
    bsz, seq, d = x.shape
    n_ctx = ctx.shape[1]
    depth = w_mod.shape[0]
    cos, sins = _rope_tables(seq)
    lb_p = jax.nn.softmax(hgrn_lb_logits.astype(F32), axis=0)
    lower_bounds = jnp.cumsum(lb_p, axis=0) - lb_p[0]
    no_sink = jnp.zeros((N_HEADS,), F32)
    ones128 = jnp.ones((1, LANES), F32)

    xl = x.reshape(bsz * seq, d)
    xc = ctx.reshape(bsz * n_ctx, d)
    for layer in range(depth):
        ctx_out = layer < depth - 1
        mod_l = jax.nn.silu(c) @ w_mod[layer] + b_mod[layer]
        mod_c = jnp.broadcast_to(jax.nn.silu(c_ctx) @ w_mod[layer] + b_mod[layer], (bsz, 6 * d))
        ml = [m.reshape(bsz, 1, d) for m in jnp.split(mod_l, 6, axis=-1)]
        mc = [m.reshape(bsz, 1, d) for m in jnp.split(mod_c, 6, axis=-1)]

        w_in_b = w_in[layer].astype(BF16)
        p_l = inproj(xl, seq, norm1_g[layer], ml[1], ml[0], w_in_b, 1024, 3584)
        p_c = inproj(xc, n_ctx, norm1_g[layer], mc[1], mc[0], w_in_b, 256, 1024)
        p_l3 = p_l.reshape(bsz, seq, D_IN)
        p_c3 = p_c.reshape(bsz, n_ctx, D_IN)

        s0 = jnp.zeros((bsz, 2, HG_HEADS, HG_KDIM, HG_KDIM), F32)
        o_c, s_c = hgrn_scan(p_c3, lower_bounds[layer], s0, 256)
        o_l, _ = hgrn_scan(p_l3, lower_bounds[layer], s_c, 1024)

        gq = jnp.tile(attn_q_norm_g[layer], 2).reshape(1, LANES)
        gk = jnp.tile(attn_k_norm_g[layer], 2).reshape(1, LANES)
        kt_l, v_l, kn_l = prep_kv(p_l3, COL["ak"], COL["av"], gk, cos, sins, True, True, 512)
        kt_c, v_c, kn_c = prep_kv(p_c3, COL["ak"], COL["av"], gk, cos, sins, True, False, 256)
        kmax = _key_norm_max([kn_l, kn_c])
        q_l, u_l = prep_q(p_l3, COL["aq"], gq, cos, sins, kmax, True, True, 512)
        kt_all = jnp.concatenate([kt_c, kt_l], axis=3)
        v_all = jnp.concatenate([v_c, v_l], axis=2)
        b_l = attention(q_l, kt_all, v_all, jnp.max(u_l), no_sink, 512, 1280)

        skt_l, sv_l, skn_l = prep_kv(p_l3, COL["sk"], COL["sv"], ones128, cos, sins, False, True, 512)
        skt_c, sv_c, skn_c = prep_kv(p_c3, COL["sk"], COL["sv"], ones128, cos, sins, False, False, 256)
        skmax = _key_norm_max([skn_l, skn_c])
        sq_l, su_l = prep_q(p_l3, COL["sq"], ones128, cos, sins, skmax, False, True, 512)
        zk = jnp.zeros((bsz, N_KV, LANES, WINDOW), BF16)
        zv = jnp.zeros((bsz, N_KV, WINDOW, LANES), BF16)
        skt_all = jnp.concatenate([skt_c, zk, skt_l, zk], axis=3)
        sv_all = jnp.concatenate([sv_c, zv, sv_l, zv], axis=2)
        c_l = attention(sq_l, skt_all, sv_all, jnp.max(su_l), swa_sink[layer], 256, 256,
                        use_sink=True, band=n_ctx)

        wa = w_branch_a[layer].astype(BF16)
        wb = w_branch_b[layer].astype(BF16)
        wc = w_branch_c[layer].astype(BF16)
        wo = w_out[layer].astype(BF16)
        w_r = jnp.concatenate(
            [w_group[layer], w_router[layer], jnp.zeros((d, LANES - N_GROUPS - N_EXPERTS), F32)], axis=1)
        w_r_hi = w_r.astype(BF16)
        w_r_lo = (w_r - w_r_hi.astype(F32)).astype(BF16)
        w_route = jnp.concatenate([w_r_hi, w_r_hi, w_r_lo], axis=0)
        xl, h_l, lg_l = merge(o_l.reshape(2, bsz * seq, HG_W), p_l, b_l.reshape(-1, ATT_W),
                              c_l.reshape(-1, ATT_W), xl, seq, wa, wb, wc, wo, hgrn_out_norm_g[layer],
                              ml[2], norm2_g[layer], ml[4], ml[3], w_route, 512)
        if ctx_out:
            q_c, u_c = prep_q(p_c3, COL["aq"], gq, cos, sins, kmax, True, False, 256)
            b_c = attention(q_c, kt_c, v_c, jnp.max(u_c), no_sink, 256, 256)
            sq_c, su_c = prep_q(p_c3, COL["sq"], ones128, cos, sins, skmax, False, False, 256)
            c_c = attention(sq_c, skt_c, sv_c, jnp.max(su_c), swa_sink[layer], 256, 256, use_sink=True)
            xc, h_c, lg_c = merge(o_c.reshape(2, bsz * n_ctx, HG_W), p_c, b_c.reshape(-1, ATT_W),
                                  c_c.reshape(-1, ATT_W), xc, n_ctx, wa, wb, wc, wo, hgrn_out_norm_g[layer],
                                  mc[2], norm2_g[layer], mc[4], mc[3], w_route, 256)
            h_all = jnp.concatenate([h_c, h_l], axis=0)
            lg_all = jnp.concatenate([lg_c, lg_l], axis=0)
        else:
            h_all, lg_all = h_l, lg_l

        info, counts_row = route(lg_all)
        src_tok, dest, block_expert = _dispatch(info, counts_row)
        out_rows = experts_overlapped(h_all, src_tok, block_expert, w_exp_gate, w_exp_up, w_exp_down, layer)
        r1 = jnp.take(out_rows, dest[:, 0], axis=0, mode="clip")
        r2 = jnp.take(out_rows, dest[:, 1], axis=0, mode="clip")
        n_c = bsz * n_ctx if ctx_out else 0
        if ctx_out:
            xc = combine(xc, r1, r2, info, 0, mc[5], n_ctx)
        xl = combine(xl, r1, r2, info, n_c, ml[5], seq, final_g=final_norm_g if layer == depth - 1 else None)
    return xl.reshape(bsz, seq, d)
```
